```python
import jax, jax.numpy as jnp
from jax import lax
import numpy as np

D_MODEL = 1024
BATCH = 16
SEQ = 2048
DEPTH = 4

EPS = 1e-6
HEAD_DIM = 64
GLA_HEADS = 4
GLA_DK = 48
GLA_DV = 96
GLA_WIDTH = GLA_HEADS * GLA_DV
GLA_GATE_RANK = 16
GLA_GATE_TEMP = 16.0
GLA_CHUNK = 64
LRU_WIDTH = 256
LRU_BLOCKS = 8
LRU_BLOCK = LRU_WIDTH // LRU_BLOCKS
LRU_C = 8.0
LRU_CONV = 4
SWA_HEADS = 6
SWA_WIDTH = SWA_HEADS * HEAD_DIM
DILATED_PATTERNS = ((128, 1), (512, 4), (2048, 16))
SWA_BLOCK = 128
ROPE_THETA = 500000.0
ROPE_DIM = HEAD_DIM // 4
MIX_WIDTH = GLA_WIDTH + LRU_WIDTH + SWA_WIDTH
FFN_DIM = 2816
FFN_CONV = 3
IN_SPLITS = (GLA_HEADS * GLA_DK, GLA_HEADS * GLA_DK, GLA_WIDTH, GLA_WIDTH, GLA_GATE_RANK,
             LRU_WIDTH, LRU_WIDTH, SWA_WIDTH, SWA_WIDTH, SWA_WIDTH)
IN_COLS = sum(IN_SPLITS)

kernel_name = 'hymba_style_gla_rglru_dilated_swa_trunk'


def rms_norm(x, g):
    xf = x.astype(jnp.float32)
    y = xf * lax.rsqrt(jnp.mean(xf * xf, axis=-1, keepdims=True) + EPS)
    return (y * g).astype(x.dtype)


def causal_dwconv(x, w, b):
    K = w.shape[0]
    S = x.shape[1]
    xp = jnp.pad(x, ((0, 0), (K - 1, 0), (0, 0)))
    y = b + xp[:, K - 1:K - 1 + S] * w[K - 1]
    for k in range(K - 1):
        y = y + xp[:, k:k + S] * w[k]
    return y


def rope_tables(S):
    inv = ROPE_THETA ** (-jnp.arange(0, ROPE_DIM, 2, dtype=jnp.float32) / ROPE_DIM)
    ang = jnp.arange(S, dtype=jnp.float32)[:, None] * inv[None, :]
    return jnp.cos(ang), jnp.sin(ang)


def partial_rope(x, cos, sin):
    xr, xpass = x[..., :ROPE_DIM], x[..., ROPE_DIM:]
    x1, x2 = xr[..., :ROPE_DIM // 2], xr[..., ROPE_DIM // 2:]
    c = cos[None, :, None, :]
    s = sin[None, :, None, :]
    rot = jnp.concatenate([x1 * c - x2 * s, x2 * c + x1 * s], axis=-1).astype(x.dtype)
    return jnp.concatenate([rot, xpass], axis=-1)


def gla_chunked(q, k, v, g):
    B, S, H, dk = q.shape
    dv = v.shape[-1]
    C = GLA_CHUNK
    n = S // C

    def to_chunks(t):
        return t.astype(jnp.float32).reshape(B, n, C, H, t.shape[-1]).transpose(1, 0, 3, 2, 4)

    qc = to_chunks(q) * (dk ** -0.5)
    kc, vc, gc = to_chunks(k), to_chunks(v), to_chunks(g)
    causal = jnp.tril(jnp.ones((C, C), dtype=bool))

    def step(state, inp):
        qb, kb, vb, gb = inp
        b = jnp.cumsum(gb, axis=2)
        o_inter = jnp.einsum('bhik,bhkv->bhiv', qb * jnp.exp(b), state)
        diff = b[:, :, :, None, :] - b[:, :, None, :, :]
        decay = jnp.exp(jnp.where(causal[:, :, None], diff, -jnp.inf))
        scores = jnp.einsum('bhik,bhjk,bhijk->bhij', qb, kb, decay)
        o_intra = jnp.einsum('bhij,bhjv->bhiv', scores, vb)
        b_last = b[:, :, -1]
        k_dec = kb * jnp.exp(b_last[:, :, None, :] - b)
        state = jnp.exp(b_last)[..., None] * state + jnp.einsum('bhjk,bhjv->bhkv', k_dec, vb)
        return state, o_inter + o_intra

    state0 = jnp.zeros((B, H, dk, dv), jnp.float32)
    _, out = lax.scan(step, state0, (qc, kc, vc, gc))
    return out.transpose(1, 0, 3, 2, 4).reshape(B, S, H, dv)


def rg_lru(x, w_a, b_a, w_x, b_x, lam):
    B, S, W = x.shape
    xf = x.astype(jnp.float32)
    xb = xf.reshape(B, S, LRU_BLOCKS, LRU_BLOCK)
    r = jax.nn.sigmoid(jnp.einsum('bsni,nij->bsnj', xb, w_a).reshape(B, S, W) + b_a)
    i = jax.nn.sigmoid(jnp.einsum('bsni,nij->bsnj', xb, w_x).reshape(B, S, W) + b_x)
    log_a = -LRU_C * r * jax.nn.softplus(-lam)
    a = jnp.exp(log_a)
    u = jnp.sqrt(-jnp.expm1(2.0 * log_a)) * (i * xf)

    def combine(c1, c2):
        a1, b1 = c1
        a2, b2 = c2
        return a1 * a2, a2 * b1 + b2

    _, h = lax.associative_scan(combine, (a, u), axis=1)
    return h.astype(x.dtype)


def banded_causal_attention(q, k, v, span):
    N, L, H, hd = q.shape
    nb = -(-L // SWA_BLOCK)
    Lp = nb * SWA_BLOCK
    pad = ((0, 0), (0, Lp - L), (0, 0), (0, 0))
    q, k, v = (jnp.pad(t, pad) for t in (q, k, v))
    qb = q.reshape(N, nb, SWA_BLOCK, H, hd)

    def with_prev(t):
        tb = t.reshape(N, nb, SWA_BLOCK, H, hd)
        prev = jnp.pad(tb, ((0, 0), (1, 0), (0, 0), (0, 0), (0, 0)))[:, :-1]
        return jnp.concatenate([prev, tb], axis=2)

    kb, vb = with_prev(k), with_prev(v)
    s = jnp.einsum('nbqhd,nbkhd->nbhqk', qb, kb).astype(jnp.float32) * (hd ** -0.5)
    qi = jnp.arange(SWA_BLOCK)[:, None]
    kj = jnp.arange(2 * SWA_BLOCK)[None, :]
    dist = qi + SWA_BLOCK - kj
    blk = jnp.arange(nb)[:, None, None]
    valid = (dist >= 0) & (dist <= span) & (blk * SWA_BLOCK + kj - SWA_BLOCK >= 0)
    s = jnp.where(valid[None, :, None], s, -jnp.inf)
    m = jnp.max(s, axis=-1, keepdims=True)
    p = jnp.exp(s - m)
    denom = jnp.sum(p, axis=-1, keepdims=True)
    o = jnp.einsum('nbhqk,nbkhd->nbqhd', p / denom, vb.astype(jnp.float32))
    lse = (m + jnp.log(denom))[..., 0].transpose(0, 1, 3, 2)
    return o.reshape(N, Lp, H, hd)[:, :L], lse.reshape(N, Lp, H)[:, :L]


def dilated_attention(q, k, v):
    B, S, H, hd = q.shape
    outs, lses = [], []
    for window, dil in DILATED_PATTERNS:
        L = S // dil

        def to_sub(t):
            return t.reshape(B, L, dil, H, hd).transpose(0, 2, 1, 3, 4).reshape(B * dil, L, H, hd)

        o, lse = banded_causal_attention(to_sub(q), to_sub(k), to_sub(v), window // dil)
        outs.append(o.reshape(B, dil, L, H, hd).transpose(0, 2, 1, 3, 4).reshape(B, S, H, hd))
        lses.append(lse.reshape(B, dil, L, H).transpose(0, 2, 1, 3).reshape(B, S, H))
    w = jax.nn.softmax(jnp.stack(lses, axis=0), axis=0)
    return jnp.einsum('pbsh,pbshd->bshd', w, jnp.stack(outs, axis=0))


def _normal(k, shape, scale):
    return jax.random.normal(k, shape, jnp.float32) * scale


def _gain(k, shape):
    return 1.0 + 0.1 * jax.random.normal(k, shape, jnp.float32)


def setup_inputs(seed: int = 0) -> dict:
    key = jax.random.key(seed)
    ks = jax.random.split(key, 22)
    a0 = jax.random.uniform(ks[12], (DEPTH, LRU_WIDTH), jnp.float32, minval=0.9, maxval=0.999)
    a_base = a0 ** (1.0 / LRU_C)
    lru_lambda = jnp.log(a_base) - jnp.log1p(-a_base)
    return {
        'x': jax.random.normal(ks[0], (BATCH, SEQ, D_MODEL), jnp.float32),
        'norm_mix': _gain(ks[1], (DEPTH, D_MODEL)),
        'w_in': _normal(ks[2], (DEPTH, D_MODEL, IN_COLS), D_MODEL ** -0.5),
        'gla_w_gate': _normal(ks[3], (DEPTH, GLA_GATE_RANK, GLA_HEADS * GLA_DK), GLA_GATE_RANK ** -0.5),
        'gla_b_gate': _normal(ks[4], (DEPTH, GLA_HEADS * GLA_DK), 0.1),
        'gla_norm': _gain(ks[5], (DEPTH, GLA_DV)),
        'lru_conv_w': _normal(ks[6], (DEPTH, LRU_CONV, LRU_WIDTH), LRU_CONV ** -0.5),
        'lru_conv_b': _normal(ks[7], (DEPTH, LRU_WIDTH), 0.02),
        'lru_w_a': _normal(ks[8], (DEPTH, LRU_BLOCKS, LRU_BLOCK, LRU_BLOCK), LRU_BLOCK ** -0.5),
        'lru_b_a': _normal(ks[9], (DEPTH, LRU_WIDTH), 0.1),
        'lru_w_x': _normal(ks[10], (DEPTH, LRU_BLOCKS, LRU_BLOCK, LRU_BLOCK), LRU_BLOCK ** -0.5),
        'lru_b_x': _normal(ks[11], (DEPTH, LRU_WIDTH), 0.1),
        'lru_lambda': lru_lambda,
        'lru_norm': _gain(ks[13], (DEPTH, LRU_WIDTH)),
        'swa_norm': _gain(ks[14], (DEPTH, SWA_WIDTH)),
        'w_out': _normal(ks[15], (DEPTH, MIX_WIDTH, D_MODEL), MIX_WIDTH ** -0.5),
        'norm_ffn': _gain(ks[16], (DEPTH, D_MODEL)),
        'ffn_w_up': _normal(ks[17], (DEPTH, D_MODEL, 2 * FFN_DIM), D_MODEL ** -0.5),
        'ffn_conv_w': _normal(ks[18], (DEPTH, FFN_CONV, 2 * FFN_DIM), FFN_CONV ** -0.5),
        'ffn_conv_b': _normal(ks[19], (DEPTH, 2 * FFN_DIM), 0.02),
        'ffn_w_down': _normal(ks[20], (DEPTH, FFN_DIM, D_MODEL), FFN_DIM ** -0.5),
        'norm_final': _gain(ks[21], (D_MODEL,)),
    }


def reference(x, norm_mix, w_in, gla_w_gate, gla_b_gate, gla_norm, lru_conv_w, lru_conv_b,
              lru_w_a, lru_b_a, lru_w_x, lru_b_x, lru_lambda, lru_norm, swa_norm, w_out,
              norm_ffn, ffn_w_up, ffn_conv_w, ffn_conv_b, ffn_w_down, norm_final):
    B, S, _ = x.shape
    cos, sin = rope_tables(S)
    split_points = np.cumsum(IN_SPLITS)[:-1].tolist()
    for l in range(DEPTH):
        h = rms_norm(x, norm_mix[l])
        proj = h @ w_in[l]
        a_q, a_k, a_v, a_r, a_glr, b_gate, b_in, c_q, c_k, c_v = jnp.split(proj, split_points, axis=-1)

        g = jax.nn.log_sigmoid((a_glr @ gla_w_gate[l] + gla_b_gate[l]).astype(jnp.float32)) / GLA_GATE_TEMP
        o_a = gla_chunked(a_q.reshape(B, S, GLA_HEADS, GLA_DK), a_k.reshape(B, S, GLA_HEADS, GLA_DK),
                          a_v.reshape(B, S, GLA_HEADS, GLA_DV), g.reshape(B, S, GLA_HEADS, GLA_DK))
        o_a = rms_norm(o_a.astype(x.dtype), gla_norm[l]).reshape(B, S, GLA_WIDTH) * jax.nn.silu(a_r)

        u = causal_dwconv(b_in, lru_conv_w[l], lru_conv_b[l])
        h_rec = rg_lru(u, lru_w_a[l], lru_b_a[l], lru_w_x[l], lru_b_x[l], lru_lambda[l])
        o_b = rms_norm(h_rec * jax.nn.gelu(b_gate), lru_norm[l])

        q = partial_rope(c_q.reshape(B, S, SWA_HEADS, HEAD_DIM), cos, sin)
        k = partial_rope(c_k.reshape(B, S, SWA_HEADS, HEAD_DIM), cos, sin)
        o_c = dilated_attention(q, k, c_v.reshape(B, S, SWA_HEADS, HEAD_DIM))
        o_c = rms_norm(o_c.astype(x.dtype).reshape(B, S, SWA_WIDTH), swa_norm[l])

        x = x + jnp.concatenate([o_a, o_b, o_c], axis=-1) @ w_out[l]

        h = rms_norm(x, norm_ffn[l])
        up = causal_dwconv(h @ ffn_w_up[l], ffn_conv_w[l], ffn_conv_b[l])
        u_val, u_gate = jnp.split(up, 2, axis=-1)
        x = x + (jax.nn.gelu(u_gate) * u_val) @ ffn_w_down[l]
    return rms_norm(x, norm_final)
```

```python
import functools

import jax
import jax.numpy as jnp
import numpy as np
from jax import lax
from jax.experimental import pallas as pl
from jax.experimental.pallas import tpu as pltpu

F32 = jnp.float32
BF16 = jnp.bfloat16

D_MODEL = 1024
EPS = 1e-6
HEAD_DIM = 64
GLA_HEADS = 4
GLA_DK = 48
GLA_DV = 96
GLA_QK = GLA_HEADS * GLA_DK
GLA_QK_PAD = 256
GLA_WIDTH = GLA_HEADS * GLA_DV
GLA_GATE_RANK = 16
GLA_GATE_TEMP = 16.0
GLA_CHUNK = 64
LRU_WIDTH = 256
LRU_BLOCKS = 8
LRU_C = 8.0
LRU_CONV = 4
LRU_TILE = 256
SWA_HEADS = 6
SWA_WIDTH = SWA_HEADS * HEAD_DIM
SWA_BLOCK = 128
SWA_SLABS = SWA_WIDTH // 128
DILATIONS = (1, 4, 16)
ROPE_THETA = 500000.0
ROPE_DIM = HEAD_DIM // 4
FFN_DIM = 2816
FFN_CONV = 3
FFN_CHUNK = 256
LANES = 128
GLR_PAD = 128

_C_AQ = 0
_C_AK = _C_AQ + GLA_QK_PAD
_C_AV = _C_AK + GLA_QK_PAD
_C_AR = _C_AV + GLA_WIDTH
_C_BG = _C_AR + GLA_WIDTH
_C_BI = _C_BG + LRU_WIDTH
_C_CQ = _C_BI + LRU_WIDTH
_C_CK = _C_CQ + SWA_WIDTH
_C_CV = _C_CK + SWA_WIDTH
_C_GLR = _C_CV + SWA_WIDTH
IN_COLS_PACKED = _C_GLR + GLR_PAD

TOKEN_TILE = 512
V7X_VMEM_BYTES = 64 * 1024 * 1024
NEG_BIG = -1e30


def _params(semantics, vmem_mib):
    assert vmem_mib * 1024 * 1024 < V7X_VMEM_BYTES
    return pltpu.CompilerParams(dimension_semantics=semantics,
                                vmem_limit_bytes=vmem_mib * 1024 * 1024)


def _resident(shape):
    nd = len(shape)
    return pl.BlockSpec(shape, lambda *_: (0,) * nd, pipeline_mode=pl.Buffered(1))


def _rms_scale(x):
    return lax.rsqrt(jnp.mean(x * x, axis=-1, keepdims=True) + EPS)


def _gelu_tanh(x):
    return x * (0.5 * (1.0 + jnp.tanh(np.sqrt(2.0 / np.pi).astype(np.float32) * (x + 0.044715 * (x * x * x)))))


def _dot(a, b):
    return jnp.dot(a, b, preferred_element_type=F32)


def _dot_nt(a, b):
    return lax.dot_general(a, b, (((1,), (1,)), ((), ())), preferred_element_type=F32)


def _dot_tn(a, b):
    return lax.dot_general(a, b, (((0,), (0,)), ((), ())), preferred_element_type=F32)


def _inproj_kernel(x_ref, gn_ref, w_ref, wg_ref, bg_ref, cos_ref, sa_ref, sb_ref,
                   aq_ref, ak_ref, ag_ref, av_ref, ar_ref, bgate_ref, bin_ref,
                   cq_ref, ck_ref, cv_ref):
    x = x_ref[...]
    h = (x * _rms_scale(x) * gn_ref[...]).astype(BF16)

    def proj(c0, width):
        return _dot(h, w_ref[:, c0:c0 + width])

    aq_ref[...] = proj(_C_AQ, GLA_QK_PAD)
    ak_ref[...] = proj(_C_AK, GLA_QK_PAD)
    av_ref[...] = proj(_C_AV, GLA_WIDTH)
    ar_ref[...] = proj(_C_AR, GLA_WIDTH)
    bgate_ref[...] = proj(_C_BG, LRU_WIDTH)
    bin_ref[...] = proj(_C_BI, LRU_WIDTH)

    glr = proj(_C_GLR, GLR_PAD).astype(BF16)
    z = _dot(glr, wg_ref[...]) + bg_ref[...]
    log_sig = jnp.minimum(z, 0.0) - jnp.log(1.0 + jnp.exp(-jnp.abs(z)))
    ag_ref[...] = log_sig * (1.0 / GLA_GATE_TEMP)

    cos = cos_ref[...]
    sa = sa_ref[...]
    sb = sb_ref[...]

    def rope(t):
        return t * cos + pltpu.roll(t, LANES - ROPE_DIM // 2, 1) * sa + pltpu.roll(t, ROPE_DIM // 2, 1) * sb

    cq = proj(_C_CQ, SWA_WIDTH) * (HEAD_DIM ** -0.5)
    ck = proj(_C_CK, SWA_WIDTH)
    cv = proj(_C_CV, SWA_WIDTH)
    for s in range(SWA_SLABS):
        sl = slice(s * LANES, (s + 1) * LANES)
        cq_ref[s] = rope(cq[:, sl])
        ck_ref[s] = rope(ck[:, sl])
        cv_ref[s] = cv[:, sl]


def _inproj(x, gn, w, wg, bg, cos, sa, sb, seq):
    n = x.shape[0]
    tm = TOKEN_TILE
    tiles_per_seq = seq // tm
    row = lambda width: pl.BlockSpec((tm, width), lambda i: (i, 0))
    slab = pl.BlockSpec((SWA_SLABS, tm, LANES), lambda i: (0, i, 0))
    table = pl.BlockSpec((tm, LANES), lambda i: (i % tiles_per_seq, 0))
    out_shapes = (
        jax.ShapeDtypeStruct((n, GLA_QK_PAD), F32), jax.ShapeDtypeStruct((n, GLA_QK_PAD), F32),
        jax.ShapeDtypeStruct((n, GLA_QK_PAD), F32), jax.ShapeDtypeStruct((n, GLA_WIDTH), F32),
        jax.ShapeDtypeStruct((n, GLA_WIDTH), F32), jax.ShapeDtypeStruct((n, LRU_WIDTH), F32),
        jax.ShapeDtypeStruct((n, LRU_WIDTH), F32),
        jax.ShapeDtypeStruct((SWA_SLABS, n, LANES), F32), jax.ShapeDtypeStruct((SWA_SLABS, n, LANES), F32),
        jax.ShapeDtypeStruct((SWA_SLABS, n, LANES), F32),
    )
    return pl.pallas_call(
        _inproj_kernel,
        grid=(n // tm,),
        in_specs=[row(D_MODEL), _resident((1, D_MODEL)), _resident((D_MODEL, IN_COLS_PACKED)),
                  _resident((GLR_PAD, GLA_QK_PAD)), _resident((1, GLA_QK_PAD)), table, table, table],
        out_specs=(row(GLA_QK_PAD), row(GLA_QK_PAD), row(GLA_QK_PAD), row(GLA_WIDTH), row(GLA_WIDTH),
                   row(LRU_WIDTH), row(LRU_WIDTH), slab, slab, slab),
        out_shape=out_shapes,
        compiler_params=_params(("parallel",), 48),
        name="inproj",
    )(x, gn, w, wg, bg, cos, sa, sb)


def _gla_kernel(q_ref, k_ref, g_ref, v_ref, r_ref, gn_ref, o_ref, st_ref, *, seq):
    c = GLA_CHUNK
    st_ref[...] = jnp.zeros_like(st_ref)

    ri = lax.broadcasted_iota(jnp.int32, (c, c), 0)
    ci = lax.broadcasted_iota(jnp.int32, (c, c), 1)
    causal = ci <= ri
    tri = causal.astype(F32)
    lane_q = lax.broadcasted_iota(jnp.int32, (1, GLA_QK_PAD), 1)
    lane_v = lax.broadcasted_iota(jnp.int32, (1, GLA_WIDTH), 1)
    q_masks = [((lane_q >= h * GLA_DK) & (lane_q < (h + 1) * GLA_DK)).astype(F32) for h in range(GLA_HEADS)]
    v_masks = [((lane_v >= h * GLA_DV) & (lane_v < (h + 1) * GLA_DV)).astype(F32) for h in range(GLA_HEADS)]
    sr = lax.broadcasted_iota(jnp.int32, (GLA_WIDTH, GLA_QK_PAD), 0)
    sc = lax.broadcasted_iota(jnp.int32, (GLA_WIDTH, GLA_QK_PAD), 1)
    same_head = jnp.zeros((GLA_WIDTH, GLA_QK_PAD), jnp.bool_)
    for h in range(GLA_HEADS):
        same_head = same_head | ((sr >= h * GLA_DV) & (sr < (h + 1) * GLA_DV)
                                 & (sc >= h * GLA_DK) & (sc < (h + 1) * GLA_DK))
    state_mask = same_head.astype(F32)
    er = lax.broadcasted_iota(jnp.int32, (GLA_WIDTH, GLA_WIDTH), 0)
    ec = lax.broadcasted_iota(jnp.int32, (GLA_WIDTH, GLA_WIDTH), 1)
    head_sum = jnp.zeros((GLA_WIDTH, GLA_WIDTH), jnp.bool_)
    for h in range(GLA_HEADS):
        head_sum = head_sum | ((er >= h * GLA_DV) & (er < (h + 1) * GLA_DV)
                               & (ec >= h * GLA_DV) & (ec < (h + 1) * GLA_DV))
    head_sum = head_sum.astype(BF16)
    gn = gn_ref[...]

    def chunk(i, carry):
        rows = pl.ds(pl.multiple_of(i * c, c), c)
        q = q_ref[rows, :] * (GLA_DK ** -0.5)
        k = k_ref[rows, :]
        g = g_ref[rows, :]
        vb = v_ref[rows, :].astype(BF16)
        b = jnp.dot(tri, g, precision=lax.Precision.HIGHEST, preferred_element_type=F32)
        b_last = b[c - 1:c, :]
        b_mid = b[c // 2 - 1:c // 2, :]
        qe = q * jnp.exp(b - b_mid)
        ke = (k * jnp.exp(b_mid - b)).astype(BF16)
        qi = (q * jnp.exp(b)).astype(BF16)
        kd = (k * jnp.exp(b_last - b)).astype(BF16)
        st = st_ref[...]
        o = _dot_nt(qi, st.astype(BF16))
        for h in range(GLA_HEADS):
            s = _dot_nt((qe * q_masks[h]).astype(BF16), ke)
            s = jnp.where(causal, s, 0.0)
            o = o + _dot(s.astype(BF16), vb) * v_masks[h]
        st_ref[...] = st * jnp.exp(b_last) + _dot_tn(vb, kd) * state_mask
        o2 = o * o
        hi = o2.astype(BF16)
        lo = (o2 - hi.astype(F32)).astype(BF16)
        ms = (_dot(hi, head_sum) + _dot(lo, head_sum)) * (1.0 / GLA_DV)
        r = r_ref[rows, :]
        y = o * lax.rsqrt(ms + EPS) * gn * (r * jax.nn.sigmoid(r))
        o_ref[rows, :] = y.astype(o_ref.dtype)
        return carry

    lax.fori_loop(0, seq // c, chunk, 0)


def _gla(aq, ak, ag, av, ar, gn, seq):
    n = aq.shape[0]
    blk = lambda width: pl.BlockSpec((seq, width), lambda b: (b, 0))
    return pl.pallas_call(
        functools.partial(_gla_kernel, seq=seq),
        grid=(n // seq,),
        in_specs=[blk(GLA_QK_PAD), blk(GLA_QK_PAD), blk(GLA_QK_PAD), blk(GLA_WIDTH), blk(GLA_WIDTH),
                  _resident((1, GLA_WIDTH))],
        out_specs=blk(GLA_WIDTH),
        out_shape=jax.ShapeDtypeStruct((n, GLA_WIDTH), BF16),
        scratch_shapes=[pltpu.VMEM((GLA_WIDTH, GLA_QK_PAD), F32)],
        compiler_params=_params(("parallel",), 40),
        name="gla",
    )(aq, ak, ag, av, ar, gn)


def _lru_kernel(gate_ref, x_ref, cw_ref, cb_ref, wa_ref, ba_ref, wx_ref, bx_ref, lam_ref, gn_ref,
                o_ref, xp_ref, hc_ref, *, seq):
    t = LRU_TILE
    pad = 8
    xp_ref[0:pad, :] = jnp.zeros((pad, LRU_WIDTH), F32)
    xp_ref[pad:, :] = x_ref[...]
    hc_ref[...] = jnp.zeros_like(hc_ref)
    neg_lam = -lam_ref[...]
    softplus = jnp.maximum(neg_lam, 0.0) + jnp.log(1.0 + jnp.exp(-jnp.abs(neg_lam)))
    cw = cw_ref[...]
    row = lax.broadcasted_iota(jnp.int32, (t, 1), 0)

    def tile(i, carry):
        t0 = pl.multiple_of(i * t, t)
        xt = xp_ref[pl.ds(t0, t + pad), :]
        u = cb_ref[...] + xt[pad:, :] * cw[LRU_CONV - 1:LRU_CONV, :]
        for kk in range(LRU_CONV - 1):
            u = u + pltpu.roll(xt, LRU_CONV - 1 - kk, 0)[pad:, :] * cw[kk:kk + 1, :]
        ub = u.astype(BF16)
        r = jax.nn.sigmoid(_dot(ub, wa_ref[...]) + ba_ref[...])
        ig = jax.nn.sigmoid(_dot(ub, wx_ref[...]) + bx_ref[...])
        log_a = (-LRU_C) * r * softplus
        a = jnp.exp(log_a)
        th = jnp.tanh(log_a)
        h = jnp.sqrt(-2.0 * th / (1.0 - th)) * (ig * u)
        shift = 1
        while shift < t:
            keep = row >= shift
            h_prev = jnp.where(keep, pltpu.roll(h, shift, 0), 0.0)
            a_prev = jnp.where(keep, pltpu.roll(a, shift, 0), 1.0)
            h = h + a * h_prev
            a = a * a_prev
            shift *= 2
        h = h + a * hc_ref[...]
        hc_ref[...] = h[t - 1:t, :]
        y = h * _gelu_tanh(gate_ref[pl.ds(t0, t), :])
        o_ref[pl.ds(t0, t), :] = (y * _rms_scale(y) * gn_ref[...]).astype(o_ref.dtype)
        return carry

    lax.fori_loop(0, seq // t, tile, 0)


def _lru(bgate, bin_, cw, cb, wa, ba, wx, bx, lam, gn, seq):
    n = bgate.shape[0]
    blk = pl.BlockSpec((seq, LRU_WIDTH), lambda b: (b, 0))
    vec = _resident((1, LRU_WIDTH))
    mat = _resident((LRU_WIDTH, LRU_WIDTH))
    return pl.pallas_call(
        functools.partial(_lru_kernel, seq=seq),
        grid=(n // seq,),
        in_specs=[blk, blk, _resident((LRU_CONV, LRU_WIDTH)), vec, mat, vec, mat, vec, vec, vec],
        out_specs=blk,
        out_shape=jax.ShapeDtypeStruct((n, LRU_WIDTH), BF16),
        scratch_shapes=[pltpu.VMEM((seq + 8, LRU_WIDTH), F32), pltpu.VMEM((1, LRU_WIDTH), F32)],
        compiler_params=_params(("parallel",), 32),
        name="lru",
    )(bgate, bin_, cw, cb, wa, ba, wx, bx, lam, gn)


def _swa_kernel(q_ref, k_ref, v_ref, gn_ref, o_ref, acc_ref, m_ref, l_ref, *, seq):
    blk = SWA_BLOCK
    lane = lax.broadcasted_iota(jnp.int32, (1, LANES), 1)
    head0 = lane < HEAD_DIM
    qi = lax.broadcasted_iota(jnp.int32, (blk, 2 * blk), 0)
    kj = lax.broadcasted_iota(jnp.int32, (blk, 2 * blk), 1)
    mask_prev = (kj < blk) & (kj >= qi)
    mask_cur = (kj >= blk) & (kj - blk <= qi)
    qi1 = lax.broadcasted_iota(jnp.int32, (blk, blk), 0)
    kj1 = lax.broadcasted_iota(jnp.int32, (blk, blk), 1)
    mask_single = kj1 <= qi1

    def attend(qs, kcat, vcat, mask):
        kb = kcat.astype(BF16)
        vb = vcat.astype(BF16)
        parts = []
        for first in (True, False):
            hm = head0 if first else jnp.logical_not(head0)
            s = _dot_nt(jnp.where(hm, qs, 0.0).astype(BF16), kb)
            s = jnp.where(mask, s, NEG_BIG)
            m = jnp.max(s, axis=-1, keepdims=True)
            p = jnp.exp(s - m)
            l = jnp.sum(p, axis=-1, keepdims=True)
            parts.append((m, l, _dot(p.astype(BF16), vb)))
        (m0, l0, pv0), (m1, l1, pv1) = parts
        return jnp.where(head0, m0, m1), jnp.where(head0, l0, l1), jnp.where(head0, pv0, pv1)

    for d in DILATIONS:
        nb = seq // d // blk

        def rows_of(start, d=d):
            return pl.ds(start, blk) if d == 1 else pl.ds(start, blk, stride=d)

        def residue(r, carry, d=d, nb=nb, rows_of=rows_of):
            for s in range(SWA_SLABS):
                def block(bi, carry2, s=s):
                    cur = rows_of(r + d * blk * bi)
                    qs = q_ref[s, cur, :]
                    kc = k_ref[s, cur, :]
                    vc = v_ref[s, cur, :]
                    if nb > 1:
                        prev = rows_of(r + d * blk * jnp.maximum(bi - 1, 0))
                        kcat = jnp.concatenate([k_ref[s, prev, :], kc], axis=0)
                        vcat = jnp.concatenate([v_ref[s, prev, :], vc], axis=0)
                        mask = mask_cur | (mask_prev & (bi > 0))
                    else:
                        kcat, vcat, mask = kc, vc, mask_single
                    m_new, l_new, pv = attend(qs, kcat, vcat, mask)
                    if d == DILATIONS[0]:
                        m_ref[s, cur, :] = m_new
                        l_ref[s, cur, :] = l_new
                        acc_ref[s, cur, :] = pv
                    else:
                        m_old = m_ref[s, cur, :]
                        m_tot = jnp.maximum(m_old, m_new)
                        w_old = jnp.exp(m_old - m_tot)
                        w_new = jnp.exp(m_new - m_tot)
                        m_ref[s, cur, :] = m_tot
                        l_ref[s, cur, :] = w_old * l_ref[s, cur, :] + w_new * l_new
                        acc_ref[s, cur, :] = w_old * acc_ref[s, cur, :] + w_new * pv
                    return carry2
                lax.fori_loop(0, nb, block, 0)
            return carry

        lax.fori_loop(0, d, residue, 0)

    tile = 256

    def finish(i, carry):
        rows = pl.ds(pl.multiple_of(i * tile, tile), tile)
        outs = [acc_ref[s, rows, :] / l_ref[s, rows, :] for s in range(SWA_SLABS)]
        ss = sum(jnp.sum(o * o, axis=-1, keepdims=True) for o in outs)
        scale = lax.rsqrt(ss * (1.0 / SWA_WIDTH) + EPS)
        for s in range(SWA_SLABS):
            sl = slice(s * LANES, (s + 1) * LANES)
            o_ref[rows, sl] = (outs[s] * scale * gn_ref[:, sl]).astype(o_ref.dtype)
        return carry

    lax.fori_loop(0, seq // tile, finish, 0)


def _swa(cq, ck, cv, gn, seq):
    n = cq.shape[1]
    slab = pl.BlockSpec((SWA_SLABS, seq, LANES), lambda b: (0, b, 0))
    return pl.pallas_call(
        functools.partial(_swa_kernel, seq=seq),
        grid=(n // seq,),
        in_specs=[slab, slab, slab, _resident((1, SWA_WIDTH))],
        out_specs=pl.BlockSpec((seq, SWA_WIDTH), lambda b: (b, 0)),
        out_shape=jax.ShapeDtypeStruct((n, SWA_WIDTH), BF16),
        scratch_shapes=[pltpu.VMEM((SWA_SLABS, seq, LANES), F32)] * 3,
        compiler_params=_params(("parallel",), 48),
        name="swa",
    )(cq, ck, cv, gn)


def _ffn_kernel(x_ref, oa_ref, ob_ref, oc_ref, woa_ref, wob_ref, woc_ref, gn_ref, wup_ref, cw_ref, cb_ref,
                wdn_ref, gfin_ref, out_ref, carry_ref, *, tiles_per_seq, final):
    tm = x_ref.shape[0]

    @pl.when(pl.program_id(0) % tiles_per_seq == 0)
    def _():
        carry_ref[...] = jnp.zeros_like(carry_ref)

    x1 = (x_ref[...] + _dot(oa_ref[...], woa_ref[...]) + _dot(ob_ref[...], wob_ref[...])
          + _dot(oc_ref[...], woc_ref[...]))
    h = (x1 * _rms_scale(x1) * gn_ref[...]).astype(BF16)
    row = lax.broadcasted_iota(jnp.int32, (tm, 1), 0)

    def conv_up(c0):
        cols = slice(c0, c0 + FFN_CHUNK)
        up = _dot(h, wup_ref[:, cols])
        tail = carry_ref[:, cols]
        back1 = jnp.where(row == 0, tail[7:8, :], pltpu.roll(up, 1, 0))
        back2 = jnp.where(row == 0, tail[6:7, :], jnp.where(row == 1, tail[7:8, :], pltpu.roll(up, 2, 0)))
        carry_ref[:, cols] = up[tm - 8:tm, :]
        cw = cw_ref[:, cols]
        return cb_ref[:, cols] + up * cw[2:3, :] + back2 * cw[0:1, :] + back1 * cw[1:2, :]

    acc = jnp.zeros((tm, D_MODEL), F32)
    for j in range(FFN_DIM // FFN_CHUNK):
        val = conv_up(j * FFN_CHUNK)
        gate = conv_up(FFN_DIM + j * FFN_CHUNK)
        act = (_gelu_tanh(gate) * val).astype(BF16)
        acc = acc + _dot(act, wdn_ref[j * FFN_CHUNK:(j + 1) * FFN_CHUNK, :])
    x2 = x1 + acc
    if final:
        x2 = x2 * _rms_scale(x2) * gfin_ref[...]
    out_ref[...] = x2


def _ffn(x, oa, ob, oc, woa, wob, woc, gn, wup, cw, cb, wdn, gfin, seq, final):
    n = x.shape[0]
    tm = TOKEN_TILE
    row = lambda width: pl.BlockSpec((tm, width), lambda i: (i, 0))
    return pl.pallas_call(
        functools.partial(_ffn_kernel, tiles_per_seq=seq // tm, final=final),
        grid=(n // tm,),
        in_specs=[row(D_MODEL), row(GLA_WIDTH), row(LRU_WIDTH), row(SWA_WIDTH),
                  _resident((GLA_WIDTH, D_MODEL)), _resident((LRU_WIDTH, D_MODEL)), _resident((SWA_WIDTH, D_MODEL)),
                  _resident((1, D_MODEL)), _resident((D_MODEL, 2 * FFN_DIM)), _resident((FFN_CONV, 2 * FFN_DIM)),
                  _resident((1, 2 * FFN_DIM)), _resident((FFN_DIM, D_MODEL)), _resident((1, D_MODEL))],
        out_specs=row(D_MODEL),
        out_shape=jax.ShapeDtypeStruct((n, D_MODEL), F32),
        scratch_shapes=[pltpu.VMEM((8, 2 * FFN_DIM), F32)],
        compiler_params=_params(("arbitrary",), 56),
        name="ffn",
    )(x, oa, ob, oc, woa, wob, woc, gn, wup, cw, cb, wdn, gfin)


def _rope_tables(seq):
    half = ROPE_DIM // 2
    inv = ROPE_THETA ** (-jnp.arange(0, ROPE_DIM, 2, dtype=F32) / ROPE_DIM)
    ang = jnp.arange(seq, dtype=F32)[:, None] * inv[None, :]
    cos, sin = jnp.cos(ang), jnp.sin(ang)
    ones = jnp.ones((seq, HEAD_DIM - ROPE_DIM), F32)
    zeros = jnp.zeros((seq, HEAD_DIM - ROPE_DIM), F32)
    zh = jnp.zeros((seq, half), F32)
    per_head = lambda parts: jnp.tile(jnp.concatenate(parts, axis=1), (1, LANES // HEAD_DIM))
    return per_head([cos, cos, ones]), per_head([-sin, zh, zeros]), per_head([zh, sin, zeros])


def _pack_w_in(w):
    z = lambda width: jnp.zeros((w.shape[0], width), w.dtype)
    o_glr = 2 * GLA_QK + 2 * GLA_WIDTH
    return jnp.concatenate([
        w[:, 0:GLA_QK], z(GLA_QK_PAD - GLA_QK), w[:, GLA_QK:2 * GLA_QK], z(GLA_QK_PAD - GLA_QK),
        w[:, 2 * GLA_QK:o_glr], w[:, o_glr + GLA_GATE_RANK:], w[:, o_glr:o_glr + GLA_GATE_RANK],
        z(GLR_PAD - GLA_GATE_RANK)], axis=1).astype(BF16)


def _block_diag(w):
    nb, bs, _ = w.shape
    eye = jnp.eye(nb, dtype=w.dtype)
    return jnp.einsum('nij,nm->nimj', w, eye).reshape(nb * bs, nb * bs)


def kernel(x, norm_mix, w_in, gla_w_gate, gla_b_gate, gla_norm, lru_conv_w, lru_conv_b, lru_w_a, lru_b_a,
           lru_w_x, lru_b_x, lru_lambda, lru_norm, swa_norm, w_out, norm_ffn, ffn_w_up, ffn_conv_w, ffn_conv_b,
           ffn_w_down, norm_final):
    batch, seq, _ = x.shape
    depth = w_in.shape[0]
    assert seq % TOKEN_TILE == 0 and seq % (SWA_BLOCK * DILATIONS[-1]) == 0
    cos, sa, sb = _rope_tables(seq)
    xf = x.reshape(batch * seq, D_MODEL)
    row = lambda v: v.reshape(1, -1)
    for l in range(depth):
        wg = jnp.zeros((GLR_PAD, GLA_QK_PAD), F32).at[:GLA_GATE_RANK, :GLA_QK].set(gla_w_gate[l]).astype(BF16)
        bg = jnp.zeros((1, GLA_QK_PAD), F32).at[0, :GLA_QK].set(gla_b_gate[l])
        aq, ak, ag, av, ar, bgate, bin_, cq, ck, cv = _inproj(
            xf, row(norm_mix[l]), _pack_w_in(w_in[l]), wg, bg, cos, sa, sb, seq)
        o_a = _gla(aq, ak, ag, av, ar, row(jnp.tile(gla_norm[l], GLA_HEADS)), seq)
        o_b = _lru(bgate, bin_, lru_conv_w[l], row(lru_conv_b[l]), _block_diag(lru_w_a[l]).astype(BF16),
                   row(lru_b_a[l]), _block_diag(lru_w_x[l]).astype(BF16), row(lru_b_x[l]),
                   row(lru_lambda[l]), row(lru_norm[l]), seq)
        o_c = _swa(cq, ck, cv, row(swa_norm[l]), seq)
        wo = w_out[l].astype(BF16)
        xf = _ffn(xf, o_a, o_b, o_c, wo[:GLA_WIDTH], wo[GLA_WIDTH:GLA_WIDTH + LRU_WIDTH],
                  wo[GLA_WIDTH + LRU_WIDTH:], row(norm_ffn[l]), ffn_w_up[l].astype(BF16), ffn_conv_w[l],
                  row(ffn_conv_b[l]), ffn_w_down[l].astype(BF16), row(norm_final), seq, l == depth - 1)
    return xf.reshape(batch, seq, D_MODEL)
```

```python
import functools

import jax
import jax.numpy as jnp
import numpy as np
from jax import lax
from jax.experimental import pallas as pl
from jax.experimental.pallas import tpu as pltpu

F32 = jnp.float32
BF16 = jnp.bfloat16

D_MODEL = 1024
EPS = 1e-6
HEAD_DIM = 64
GLA_HEADS = 4
GLA_DK = 48
GLA_DV = 96
GLA_QK = GLA_HEADS * GLA_DK
GLA_QK_PAD = 256
GLA_WIDTH = GLA_HEADS * GLA_DV
GLA_GATE_RANK = 16
GLA_GATE_TEMP = 16.0
GLA_CHUNK = 64
LRU_WIDTH = 256
LRU_BLOCKS = 8
LRU_C = 8.0
LRU_CONV = 4
LRU_TILE = 256
SWA_HEADS = 6
SWA_WIDTH = SWA_HEADS * HEAD_DIM
SWA_BLOCK = 128
SWA_SLABS = SWA_WIDTH // 128
DILATIONS = (1, 4, 16)
ROPE_THETA = 500000.0
ROPE_DIM = HEAD_DIM // 4
FFN_DIM = 2816
FFN_CONV = 3
FFN_CHUNK = 256
LANES = 128
GLR_PAD = 128

_C_AQ = 0
_C_AK = _C_AQ + GLA_QK_PAD
_C_AV = _C_AK + GLA_QK_PAD
_C_AR = _C_AV + GLA_WIDTH
_C_BG = _C_AR + GLA_WIDTH
_C_BI = _C_BG + LRU_WIDTH
_C_CQ = _C_BI + LRU_WIDTH
_C_CK = _C_CQ + SWA_WIDTH
_C_CV = _C_CK + SWA_WIDTH
_C_GLR = _C_CV + SWA_WIDTH
IN_COLS_PACKED = _C_GLR + GLR_PAD

TOKEN_TILE = 512
V7X_VMEM_BYTES = 64 * 1024 * 1024
NEG_BIG = -1e30


def _params(semantics, vmem_mib):
    assert vmem_mib * 1024 * 1024 < V7X_VMEM_BYTES
    return pltpu.CompilerParams(dimension_semantics=semantics,
                                vmem_limit_bytes=vmem_mib * 1024 * 1024)


def _resident(shape):
    nd = len(shape)
    return pl.BlockSpec(shape, lambda *_: (0,) * nd, pipeline_mode=pl.Buffered(1))


def _rms_scale(x):
    return lax.rsqrt(jnp.mean(x * x, axis=-1, keepdims=True) + EPS)


def _gelu_tanh(x):
    return x * (0.5 * (1.0 + jnp.tanh(np.sqrt(2.0 / np.pi).astype(np.float32) * (x + 0.044715 * (x * x * x)))))


def _dot(a, b):
    return jnp.dot(a, b, preferred_element_type=F32)


def _dot_nt(a, b):
    return lax.dot_general(a, b, (((1,), (1,)), ((), ())), preferred_element_type=F32)


def _dot_tn(a, b):
    return lax.dot_general(a, b, (((0,), (0,)), ((), ())), preferred_element_type=F32)


def _inproj_kernel(x_ref, gn_ref, w_ref, wg_ref, bg_ref, cos_ref, sa_ref, sb_ref,
                   aq_ref, ak_ref, ag_ref, av_ref, ar_ref, bgate_ref, bin_ref,
                   cq_ref, ck_ref, cv_ref):
    x = x_ref[...]
    h = (x * _rms_scale(x) * gn_ref[...]).astype(BF16)

    def proj(c0, width):
        return _dot(h, w_ref[:, c0:c0 + width])

    aq_ref[...] = proj(_C_AQ, GLA_QK_PAD)
    ak_ref[...] = proj(_C_AK, GLA_QK_PAD)
    av_ref[...] = proj(_C_AV, GLA_WIDTH)
    ar_ref[...] = proj(_C_AR, GLA_WIDTH)
    bgate_ref[...] = proj(_C_BG, LRU_WIDTH)
    bin_ref[...] = proj(_C_BI, LRU_WIDTH)

    glr = proj(_C_GLR, GLR_PAD).astype(BF16)
    z = _dot(glr, wg_ref[...]) + bg_ref[...]
    log_sig = jnp.minimum(z, 0.0) - jnp.log(1.0 + jnp.exp(-jnp.abs(z)))
    ag_ref[...] = log_sig * (1.0 / GLA_GATE_TEMP)

    cos = cos_ref[...]
    sa = sa_ref[...]
    sb = sb_ref[...]

    def rope(t):
        return t * cos + pltpu.roll(t, LANES - ROPE_DIM // 2, 1) * sa + pltpu.roll(t, ROPE_DIM // 2, 1) * sb

    cq = proj(_C_CQ, SWA_WIDTH) * (HEAD_DIM ** -0.5)
    ck = proj(_C_CK, SWA_WIDTH)
    cv = proj(_C_CV, SWA_WIDTH)
    for s in range(SWA_SLABS):
        sl = slice(s * LANES, (s + 1) * LANES)
        cq_ref[s] = rope(cq[:, sl])
        ck_ref[s] = rope(ck[:, sl])
        cv_ref[s] = cv[:, sl]


def _inproj(x, gn, w, wg, bg, cos, sa, sb, seq):
    n = x.shape[0]
    tm = TOKEN_TILE
    tiles_per_seq = seq // tm
    row = lambda width: pl.BlockSpec((tm, width), lambda i: (i, 0))
    slab = pl.BlockSpec((SWA_SLABS, tm, LANES), lambda i: (0, i, 0))
    table = pl.BlockSpec((tm, LANES), lambda i: (i % tiles_per_seq, 0))
    out_shapes = (
        jax.ShapeDtypeStruct((n, GLA_QK_PAD), F32), jax.ShapeDtypeStruct((n, GLA_QK_PAD), F32),
        jax.ShapeDtypeStruct((n, GLA_QK_PAD), F32), jax.ShapeDtypeStruct((n, GLA_WIDTH), F32),
        jax.ShapeDtypeStruct((n, GLA_WIDTH), F32), jax.ShapeDtypeStruct((n, LRU_WIDTH), F32),
        jax.ShapeDtypeStruct((n, LRU_WIDTH), F32),
        jax.ShapeDtypeStruct((SWA_SLABS, n, LANES), F32), jax.ShapeDtypeStruct((SWA_SLABS, n, LANES), F32),
        jax.ShapeDtypeStruct((SWA_SLABS, n, LANES), F32),
    )
    return pl.pallas_call(
        _inproj_kernel,
        grid=(n // tm,),
        in_specs=[row(D_MODEL), _resident((1, D_MODEL)), _resident((D_MODEL, IN_COLS_PACKED)),
                  _resident((GLR_PAD, GLA_QK_PAD)), _resident((1, GLA_QK_PAD)), table, table, table],
        out_specs=(row(GLA_QK_PAD), row(GLA_QK_PAD), row(GLA_QK_PAD), row(GLA_WIDTH), row(GLA_WIDTH),
                   row(LRU_WIDTH), row(LRU_WIDTH), slab, slab, slab),
        out_shape=out_shapes,
        compiler_params=_params(("parallel",), 48),
        name="inproj",
    )(x, gn, w, wg, bg, cos, sa, sb)


def _gla_kernel(q_ref, k_ref, g_ref, v_ref, r_ref, gn_ref, o_ref, st_ref, *, seq):
    c = GLA_CHUNK
    st_ref[...] = jnp.zeros_like(st_ref)

    ri = lax.broadcasted_iota(jnp.int32, (c, c), 0)
    ci = lax.broadcasted_iota(jnp.int32, (c, c), 1)
    causal = ci <= ri
    tri = causal.astype(F32)
    lane_q = lax.broadcasted_iota(jnp.int32, (1, GLA_QK_PAD), 1)
    lane_v = lax.broadcasted_iota(jnp.int32, (1, GLA_WIDTH), 1)
    q_masks = [((lane_q >= h * GLA_DK) & (lane_q < (h + 1) * GLA_DK)).astype(F32) for h in range(GLA_HEADS)]
    v_masks = [((lane_v >= h * GLA_DV) & (lane_v < (h + 1) * GLA_DV)).astype(F32) for h in range(GLA_HEADS)]
    sr = lax.broadcasted_iota(jnp.int32, (GLA_WIDTH, GLA_QK_PAD), 0)
    sc = lax.broadcasted_iota(jnp.int32, (GLA_WIDTH, GLA_QK_PAD), 1)
    same_head = jnp.zeros((GLA_WIDTH, GLA_QK_PAD), jnp.bool_)
    for h in range(GLA_HEADS):
        same_head = same_head | ((sr >= h * GLA_DV) & (sr < (h + 1) * GLA_DV)
                                 & (sc >= h * GLA_DK) & (sc < (h + 1) * GLA_DK))
    state_mask = same_head.astype(F32)
    er = lax.broadcasted_iota(jnp.int32, (GLA_WIDTH, GLA_WIDTH), 0)
    ec = lax.broadcasted_iota(jnp.int32, (GLA_WIDTH, GLA_WIDTH), 1)
    head_sum = jnp.zeros((GLA_WIDTH, GLA_WIDTH), jnp.bool_)
    for h in range(GLA_HEADS):
        head_sum = head_sum | ((er >= h * GLA_DV) & (er < (h + 1) * GLA_DV)
                               & (ec >= h * GLA_DV) & (ec < (h + 1) * GLA_DV))
    head_sum = head_sum.astype(BF16)
    gn = gn_ref[...]

    def chunk(i, carry):
        rows = pl.ds(pl.multiple_of(i * c, c), c)
        q = q_ref[rows, :] * (GLA_DK ** -0.5)
        k = k_ref[rows, :]
        g = g_ref[rows, :]
        vb = v_ref[rows, :].astype(BF16)
        b = jnp.dot(tri, g, precision=lax.Precision.HIGHEST, preferred_element_type=F32)
        b_last = b[c - 1:c, :]
        b_mid = b[c // 2 - 1:c // 2, :]
        qe = q * jnp.exp(b - b_mid)
        ke = (k * jnp.exp(b_mid - b)).astype(BF16)
        qi = (q * jnp.exp(b)).astype(BF16)
        kd = (k * jnp.exp(b_last - b)).astype(BF16)
        st = st_ref[...]
        o = _dot_nt(qi, st.astype(BF16))
        for h in range(GLA_HEADS):
            s = _dot_nt((qe * q_masks[h]).astype(BF16), ke)
            s = jnp.where(causal, s, 0.0)
            o = o + _dot(s.astype(BF16), vb) * v_masks[h]
        st_ref[...] = st * jnp.exp(b_last) + _dot_tn(vb, kd) * state_mask
        o2 = o * o
        hi = o2.astype(BF16)
        lo = (o2 - hi.astype(F32)).astype(BF16)
        ms = (_dot(hi, head_sum) + _dot(lo, head_sum)) * (1.0 / GLA_DV)
        r = r_ref[rows, :]
        y = o * lax.rsqrt(ms + EPS) * gn * (r * jax.nn.sigmoid(r))
        o_ref[rows, :] = y.astype(o_ref.dtype)
        return carry

    lax.fori_loop(0, seq // c, chunk, 0, unroll=2)


def _gla(aq, ak, ag, av, ar, gn, seq):
    n = aq.shape[0]
    blk = lambda width: pl.BlockSpec((seq, width), lambda b: (b, 0))
    return pl.pallas_call(
        functools.partial(_gla_kernel, seq=seq),
        grid=(n // seq,),
        in_specs=[blk(GLA_QK_PAD), blk(GLA_QK_PAD), blk(GLA_QK_PAD), blk(GLA_WIDTH), blk(GLA_WIDTH),
                  _resident((1, GLA_WIDTH))],
        out_specs=blk(GLA_WIDTH),
        out_shape=jax.ShapeDtypeStruct((n, GLA_WIDTH), BF16),
        scratch_shapes=[pltpu.VMEM((GLA_WIDTH, GLA_QK_PAD), F32)],
        compiler_params=_params(("parallel",), 40),
        name="gla",
    )(aq, ak, ag, av, ar, gn)


def _lru_kernel(gate_ref, x_ref, cw_ref, cb_ref, wa_ref, ba_ref, wx_ref, bx_ref, lam_ref, gn_ref,
                o_ref, xp_ref, hc_ref, *, seq):
    t = LRU_TILE
    pad = 8
    xp_ref[0:pad, :] = jnp.zeros((pad, LRU_WIDTH), F32)
    xp_ref[pad:, :] = x_ref[...]
    hc_ref[...] = jnp.zeros_like(hc_ref)
    neg_lam = -lam_ref[...]
    softplus = jnp.maximum(neg_lam, 0.0) + jnp.log(1.0 + jnp.exp(-jnp.abs(neg_lam)))
    cw = cw_ref[...]
    row = lax.broadcasted_iota(jnp.int32, (t, 1), 0)

    def tile(i, carry):
        t0 = pl.multiple_of(i * t, t)
        xt = xp_ref[pl.ds(t0, t + pad), :]
        u = cb_ref[...] + xt[pad:, :] * cw[LRU_CONV - 1:LRU_CONV, :]
        for kk in range(LRU_CONV - 1):
            u = u + pltpu.roll(xt, LRU_CONV - 1 - kk, 0)[pad:, :] * cw[kk:kk + 1, :]
        ub = u.astype(BF16)
        r = jax.nn.sigmoid(_dot(ub, wa_ref[...]) + ba_ref[...])
        ig = jax.nn.sigmoid(_dot(ub, wx_ref[...]) + bx_ref[...])
        log_a = (-LRU_C) * r * softplus
        a = jnp.exp(log_a)
        th = jnp.tanh(log_a)
        h = jnp.sqrt(-2.0 * th / (1.0 - th)) * (ig * u)
        shift = 1
        while shift < t:
            keep = row >= shift
            h_prev = jnp.where(keep, pltpu.roll(h, shift, 0), 0.0)
            a_prev = jnp.where(keep, pltpu.roll(a, shift, 0), 1.0)
            h = h + a * h_prev
            a = a * a_prev
            shift *= 2
        h = h + a * hc_ref[...]
        hc_ref[...] = h[t - 1:t, :]
        y = h * _gelu_tanh(gate_ref[pl.ds(t0, t), :])
        o_ref[pl.ds(t0, t), :] = (y * _rms_scale(y) * gn_ref[...]).astype(o_ref.dtype)
        return carry

    lax.fori_loop(0, seq // t, tile, 0)


def _lru(bgate, bin_, cw, cb, wa, ba, wx, bx, lam, gn, seq):
    n = bgate.shape[0]
    blk = pl.BlockSpec((seq, LRU_WIDTH), lambda b: (b, 0))
    vec = _resident((1, LRU_WIDTH))
    mat = _resident((LRU_WIDTH, LRU_WIDTH))
    return pl.pallas_call(
        functools.partial(_lru_kernel, seq=seq),
        grid=(n // seq,),
        in_specs=[blk, blk, _resident((LRU_CONV, LRU_WIDTH)), vec, mat, vec, mat, vec, vec, vec],
        out_specs=blk,
        out_shape=jax.ShapeDtypeStruct((n, LRU_WIDTH), BF16),
        scratch_shapes=[pltpu.VMEM((seq + 8, LRU_WIDTH), F32), pltpu.VMEM((1, LRU_WIDTH), F32)],
        compiler_params=_params(("parallel",), 32),
        name="lru",
    )(bgate, bin_, cw, cb, wa, ba, wx, bx, lam, gn)


def _swa_kernel(q_ref, k_ref, v_ref, gn_ref, o_ref, acc_ref, m_ref, l_ref, *, seq):
    blk = SWA_BLOCK
    lane = lax.broadcasted_iota(jnp.int32, (1, LANES), 1)
    head0 = lane < HEAD_DIM
    qi = lax.broadcasted_iota(jnp.int32, (blk, 2 * blk), 0)
    kj = lax.broadcasted_iota(jnp.int32, (blk, 2 * blk), 1)
    mask_prev = (kj < blk) & (kj >= qi)
    mask_cur = (kj >= blk) & (kj - blk <= qi)
    qi1 = lax.broadcasted_iota(jnp.int32, (blk, blk), 0)
    kj1 = lax.broadcasted_iota(jnp.int32, (blk, blk), 1)
    mask_single = kj1 <= qi1

    def attend(qs, kcat, vcat, mask):
        kb = kcat.astype(BF16)
        vb = vcat.astype(BF16)
        parts = []
        for first in (True, False):
            hm = head0 if first else jnp.logical_not(head0)
            s = _dot_nt(jnp.where(hm, qs, 0.0).astype(BF16), kb)
            s = jnp.where(mask, s, NEG_BIG)
            m = jnp.max(s, axis=-1, keepdims=True)
            p = jnp.exp(s - m)
            l = jnp.sum(p, axis=-1, keepdims=True)
            parts.append((m, l, _dot(p.astype(BF16), vb)))
        (m0, l0, pv0), (m1, l1, pv1) = parts
        return jnp.where(head0, m0, m1), jnp.where(head0, l0, l1), jnp.where(head0, pv0, pv1)

    for d in DILATIONS:
        nb = seq // d // blk

        def rows_of(start, d=d):
            return pl.ds(start, blk) if d == 1 else pl.ds(start, blk, stride=d)

        nb_bits = nb.bit_length() - 1
        assert nb == 1 << nb_bits

        def step(it, carry, d=d, nb=nb, nb_bits=nb_bits, rows_of=rows_of):
            r = lax.shift_right_logical(it, nb_bits)
            bi = it & (nb - 1)
            cur = rows_of(r + d * blk * bi)
            if nb > 1:
                prev = rows_of(r + d * blk * jnp.maximum(bi - 1, 0))
                mask = mask_cur | (mask_prev & (bi > 0))
            else:
                mask = mask_single
            loaded = []
            for s in range(SWA_SLABS):
                qs = q_ref[s, cur, :]
                kcat = k_ref[s, cur, :]
                vcat = v_ref[s, cur, :]
                if nb > 1:
                    kcat = jnp.concatenate([k_ref[s, prev, :], kcat], axis=0)
                    vcat = jnp.concatenate([v_ref[s, prev, :], vcat], axis=0)
                old = None if d == DILATIONS[0] else (m_ref[s, cur, :], l_ref[s, cur, :], acc_ref[s, cur, :])
                loaded.append((qs, kcat, vcat, old))
            results = []
            for qs, kcat, vcat, old in loaded:
                m_new, l_new, pv = attend(qs, kcat, vcat, mask)
                if old is not None:
                    m_old, l_old, acc_old = old
                    m_tot = jnp.maximum(m_old, m_new)
                    w_old = jnp.exp(m_old - m_tot)
                    w_new = jnp.exp(m_new - m_tot)
                    m_new, l_new, pv = m_tot, w_old * l_old + w_new * l_new, w_old * acc_old + w_new * pv
                results.append((m_new, l_new, pv))
            for s, (m_new, l_new, pv) in enumerate(results):
                m_ref[s, cur, :] = m_new
                l_ref[s, cur, :] = l_new
                acc_ref[s, cur, :] = pv
            return carry

        lax.fori_loop(0, d * nb, step, 0, unroll=2)

    tile = 256

    def finish(i, carry):
        rows = pl.ds(pl.multiple_of(i * tile, tile), tile)
        outs = [acc_ref[s, rows, :] / l_ref[s, rows, :] for s in range(SWA_SLABS)]
        ss = sum(jnp.sum(o * o, axis=-1, keepdims=True) for o in outs)
        scale = lax.rsqrt(ss * (1.0 / SWA_WIDTH) + EPS)
        for s in range(SWA_SLABS):
            sl = slice(s * LANES, (s + 1) * LANES)
            o_ref[rows, sl] = (outs[s] * scale * gn_ref[:, sl]).astype(o_ref.dtype)
        return carry

    lax.fori_loop(0, seq // tile, finish, 0)


def _swa(cq, ck, cv, gn, seq):
    n = cq.shape[1]
    slab = pl.BlockSpec((SWA_SLABS, seq, LANES), lambda b: (0, b, 0))
    return pl.pallas_call(
        functools.partial(_swa_kernel, seq=seq),
        grid=(n // seq,),
        in_specs=[slab, slab, slab, _resident((1, SWA_WIDTH))],
        out_specs=pl.BlockSpec((seq, SWA_WIDTH), lambda b: (b, 0)),
        out_shape=jax.ShapeDtypeStruct((n, SWA_WIDTH), BF16),
        scratch_shapes=[pltpu.VMEM((SWA_SLABS, seq, LANES), F32)] * 3,
        compiler_params=_params(("parallel",), 48),
        name="swa",
    )(cq, ck, cv, gn)


def _ffn_kernel(x_ref, oa_ref, ob_ref, oc_ref, woa_ref, wob_ref, woc_ref, gn_ref, wup_ref, cw_ref, cb_ref,
                wdn_ref, gfin_ref, out_ref, carry_ref, *, tiles_per_seq, final):
    tm = x_ref.shape[0]

    @pl.when(pl.program_id(0) % tiles_per_seq == 0)
    def _():
        carry_ref[...] = jnp.zeros_like(carry_ref)

    x1 = (x_ref[...] + _dot(oa_ref[...], woa_ref[...]) + _dot(ob_ref[...], wob_ref[...])
          + _dot(oc_ref[...], woc_ref[...]))
    h = (x1 * _rms_scale(x1) * gn_ref[...]).astype(BF16)
    row = lax.broadcasted_iota(jnp.int32, (tm, 1), 0)

    def conv_up(c0):
        cols = slice(c0, c0 + FFN_CHUNK)
        up = _dot(h, wup_ref[:, cols])
        tail = carry_ref[:, cols]
        back1 = jnp.where(row == 0, tail[7:8, :], pltpu.roll(up, 1, 0))
        back2 = jnp.where(row == 0, tail[6:7, :], jnp.where(row == 1, tail[7:8, :], pltpu.roll(up, 2, 0)))
        carry_ref[:, cols] = up[tm - 8:tm, :]
        cw = cw_ref[:, cols]
        return cb_ref[:, cols] + up * cw[2:3, :] + back2 * cw[0:1, :] + back1 * cw[1:2, :]

    acc = jnp.zeros((tm, D_MODEL), F32)
    for j in range(FFN_DIM // FFN_CHUNK):
        val = conv_up(j * FFN_CHUNK)
        gate = conv_up(FFN_DIM + j * FFN_CHUNK)
        act = (_gelu_tanh(gate) * val).astype(BF16)
        acc = acc + _dot(act, wdn_ref[j * FFN_CHUNK:(j + 1) * FFN_CHUNK, :])
    x2 = x1 + acc
    if final:
        x2 = x2 * _rms_scale(x2) * gfin_ref[...]
    out_ref[...] = x2


def _ffn(x, oa, ob, oc, woa, wob, woc, gn, wup, cw, cb, wdn, gfin, seq, final):
    n = x.shape[0]
    tm = TOKEN_TILE
    row = lambda width: pl.BlockSpec((tm, width), lambda i: (i, 0))
    return pl.pallas_call(
        functools.partial(_ffn_kernel, tiles_per_seq=seq // tm, final=final),
        grid=(n // tm,),
        in_specs=[row(D_MODEL), row(GLA_WIDTH), row(LRU_WIDTH), row(SWA_WIDTH),
                  _resident((GLA_WIDTH, D_MODEL)), _resident((LRU_WIDTH, D_MODEL)), _resident((SWA_WIDTH, D_MODEL)),
                  _resident((1, D_MODEL)), _resident((D_MODEL, 2 * FFN_DIM)), _resident((FFN_CONV, 2 * FFN_DIM)),
                  _resident((1, 2 * FFN_DIM)), _resident((FFN_DIM, D_MODEL)), _resident((1, D_MODEL))],
        out_specs=row(D_MODEL),
        out_shape=jax.ShapeDtypeStruct((n, D_MODEL), F32),
        scratch_shapes=[pltpu.VMEM((8, 2 * FFN_DIM), F32)],
        compiler_params=_params(("arbitrary",), 56),
        name="ffn",
    )(x, oa, ob, oc, woa, wob, woc, gn, wup, cw, cb, wdn, gfin)


def _rope_tables(seq):
    half = ROPE_DIM // 2
    inv = ROPE_THETA ** (-jnp.arange(0, ROPE_DIM, 2, dtype=F32) / ROPE_DIM)
    ang = jnp.arange(seq, dtype=F32)[:, None] * inv[None, :]
    cos, sin = jnp.cos(ang), jnp.sin(ang)
    ones = jnp.ones((seq, HEAD_DIM - ROPE_DIM), F32)
    zeros = jnp.zeros((seq, HEAD_DIM - ROPE_DIM), F32)
    zh = jnp.zeros((seq, half), F32)
    per_head = lambda parts: jnp.tile(jnp.concatenate(parts, axis=1), (1, LANES // HEAD_DIM))
    return per_head([cos, cos, ones]), per_head([-sin, zh, zeros]), per_head([zh, sin, zeros])


def _pack_w_in(w):
    z = lambda width: jnp.zeros((w.shape[0], width), w.dtype)
    o_glr = 2 * GLA_QK + 2 * GLA_WIDTH
    return jnp.concatenate([
        w[:, 0:GLA_QK], z(GLA_QK_PAD - GLA_QK), w[:, GLA_QK:2 * GLA_QK], z(GLA_QK_PAD - GLA_QK),
        w[:, 2 * GLA_QK:o_glr], w[:, o_glr + GLA_GATE_RANK:], w[:, o_glr:o_glr + GLA_GATE_RANK],
        z(GLR_PAD - GLA_GATE_RANK)], axis=1).astype(BF16)


def _block_diag(w):
    nb, bs, _ = w.shape
    eye = jnp.eye(nb, dtype=w.dtype)
    return jnp.einsum('nij,nm->nimj', w, eye).reshape(nb * bs, nb * bs)


def kernel(x, norm_mix, w_in, gla_w_gate, gla_b_gate, gla_norm, lru_conv_w, lru_conv_b, lru_w_a, lru_b_a,
           lru_w_x, lru_b_x, lru_lambda, lru_norm, swa_norm, w_out, norm_ffn, ffn_w_up, ffn_conv_w, ffn_conv_b,
           ffn_w_down, norm_final):
    batch, seq, _ = x.shape
    depth = w_in.shape[0]
    assert seq % TOKEN_TILE == 0 and seq % (SWA_BLOCK * DILATIONS[-1]) == 0
    cos, sa, sb = _rope_tables(seq)
    xf = x.reshape(batch * seq, D_MODEL)
    row = lambda v: v.reshape(1, -1)
    for l in range(depth):
        wg = jnp.zeros((GLR_PAD, GLA_QK_PAD), F32).at[:GLA_GATE_RANK, :GLA_QK].set(gla_w_gate[l]).astype(BF16)
        bg = jnp.zeros((1, GLA_QK_PAD), F32).at[0, :GLA_QK].set(gla_b_gate[l])
        aq, ak, ag, av, ar, bgate, bin_, cq, ck, cv = _inproj(
            xf, row(norm_mix[l]), _pack_w_in(w_in[l]), wg, bg, cos, sa, sb, seq)
        o_a = _gla(aq, ak, ag, av, ar, row(jnp.tile(gla_norm[l], GLA_HEADS)), seq)
        o_b = _lru(bgate, bin_, lru_conv_w[l], row(lru_conv_b[l]), _block_diag(lru_w_a[l]).astype(BF16),
                   row(lru_b_a[l]), _block_diag(lru_w_x[l]).astype(BF16), row(lru_b_x[l]),
                   row(lru_lambda[l]), row(lru_norm[l]), seq)
        o_c = _swa(cq, ck, cv, row(swa_norm[l]), seq)
        wo = w_out[l].astype(BF16)
        xf = _ffn(xf, o_a, o_b, o_c, wo[:GLA_WIDTH], wo[GLA_WIDTH:GLA_WIDTH + LRU_WIDTH],
                  wo[GLA_WIDTH + LRU_WIDTH:], row(norm_ffn[l]), ffn_w_up[l].astype(BF16), ffn_conv_w[l],
                  row(ffn_conv_b[l]), ffn_w_down[l].astype(BF16), row(norm_final), seq, l == depth - 1)
    return xf.reshape(batch, seq, D_MODEL)
```

```python
import functools

import jax
import jax.numpy as jnp
import numpy as np
from jax import lax
from jax.experimental import pallas as pl
from jax.experimental.pallas import tpu as pltpu

F32 = jnp.float32
BF16 = jnp.bfloat16

D_MODEL = 1024
EPS = 1e-6
HEAD_DIM = 64
LANES = 128
GLA_HEADS = 4
GLA_DK = 48
GLA_DV = 96
GLA_QK = GLA_HEADS * GLA_DK
GLA_QK_PAD = 256
GLA_WIDTH = GLA_HEADS * GLA_DV
GLA_V_PAD = GLA_HEADS * LANES
GLA_GATE_RANK = 16
GLA_GATE_TEMP = 16.0
GLA_CHUNK = 64
LRU_WIDTH = 256
LRU_BLOCKS = 8
LRU_C = 8.0
LRU_CONV = 4
LRU_TILE = 256
SWA_HEADS = 6
SWA_WIDTH = SWA_HEADS * HEAD_DIM
SWA_BLOCK = 128
SWA_SLABS = SWA_WIDTH // LANES
SWA_RES = 16
ROPE_THETA = 500000.0
ROPE_DIM = HEAD_DIM // 4
FFN_DIM = 2816
FFN_CONV = 3
FFN_CHUNK = 256
GLR_PAD = 128

_C_AQ = 0
_C_AK = _C_AQ + GLA_QK_PAD
_C_AV = _C_AK + GLA_QK_PAD
_C_AR = _C_AV + GLA_V_PAD
_C_BG = _C_AR + GLA_V_PAD
_C_BI = _C_BG + LRU_WIDTH
_C_CQ = _C_BI + LRU_WIDTH
_C_CK = _C_CQ + SWA_WIDTH
_C_CV = _C_CK + SWA_WIDTH
_C_GLR = _C_CV + SWA_WIDTH
IN_COLS_PACKED = _C_GLR + GLR_PAD

TOKEN_TILE = 512
V7X_VMEM_BYTES = 64 * 1024 * 1024
NEG_BIG = -1e30


def _params(semantics, vmem_mib):
    assert vmem_mib * 1024 * 1024 < V7X_VMEM_BYTES
    return pltpu.CompilerParams(dimension_semantics=semantics,
                                vmem_limit_bytes=vmem_mib * 1024 * 1024)


def _resident(shape):
    nd = len(shape)
    return pl.BlockSpec(shape, lambda *_: (0,) * nd, pipeline_mode=pl.Buffered(1))


def _rms_scale(x):
    return lax.rsqrt(jnp.mean(x * x, axis=-1, keepdims=True) + EPS)


def _gelu_tanh(x):
    c = float(np.sqrt(2.0 / np.pi))
    return x * (0.5 + 0.5 * jnp.tanh(x * (c + (0.044715 * c) * (x * x))))


def _dot(a, b):
    return jnp.dot(a, b, preferred_element_type=F32)


def _dot_nt(a, b):
    return lax.dot_general(a, b, (((1,), (1,)), ((), ())), preferred_element_type=F32)


def _dot_tn(a, b):
    return lax.dot_general(a, b, (((0,), (0,)), ((), ())), preferred_element_type=F32)


def _inproj_kernel(x_ref, gn_ref, w_ref, wg_ref, bg_ref, cos_ref, sa_ref, sb_ref,
                   qe_ref, ke_ref, qi_ref, kd_ref, dec_ref, av_ref, ar_ref, bgate_ref, bin_ref,
                   cq_ref, ck_ref, cv_ref):
    tm = x_ref.shape[0]
    c = GLA_CHUNK
    x = x_ref[...]
    h = (x * _rms_scale(x) * gn_ref[...]).astype(BF16)

    def proj(c0, width):
        return _dot(h, w_ref[:, c0:c0 + width])

    av_ref[...] = proj(_C_AV, GLA_V_PAD).astype(BF16)
    r = proj(_C_AR, GLA_V_PAD)
    ar_ref[...] = (r * jax.nn.sigmoid(r)).astype(BF16)
    bgate_ref[...] = proj(_C_BG, LRU_WIDTH)
    bin_ref[...] = proj(_C_BI, LRU_WIDTH)

    glr = proj(_C_GLR, GLR_PAD).astype(BF16)
    z = _dot(glr, wg_ref[...]) + bg_ref[...]
    g = (jnp.minimum(z, 0.0) - jnp.log(1.0 + jnp.exp(-jnp.abs(z)))) * (1.0 / GLA_GATE_TEMP)
    row_in_chunk = lax.broadcasted_iota(jnp.int32, (tm, 1), 0) & (c - 1)
    b = g
    shift = 1
    while shift < c:
        b = b + jnp.where(row_in_chunk >= shift, pltpu.roll(b, shift, 0), 0.0)
        shift *= 2
    b3 = b.reshape(tm // c, c, GLA_QK_PAD)
    b_mid = b3[:, c // 2 - 1:c // 2, :]
    b_last = b3[:, c - 1:c, :]
    q3 = (proj(_C_AQ, GLA_QK_PAD) * (GLA_DK ** -0.5)).reshape(tm // c, c, GLA_QK_PAD)
    k3 = proj(_C_AK, GLA_QK_PAD).reshape(tm // c, c, GLA_QK_PAD)
    flat = lambda t: t.reshape(tm, GLA_QK_PAD).astype(BF16)
    qe_ref[...] = flat(q3 * jnp.exp(b3 - b_mid))
    ke_ref[...] = flat(k3 * jnp.exp(b_mid - b3))
    qi_ref[...] = flat(q3 * jnp.exp(b3))
    kd_ref[...] = flat(k3 * jnp.exp(b_last - b3))
    dec_ref[...] = jnp.exp(b_last)

    cos = cos_ref[...]
    sa = sa_ref[...]
    sb = sb_ref[...]

    def rope(t):
        return t * cos + pltpu.roll(t, LANES - ROPE_DIM // 2, 1) * sa + pltpu.roll(t, ROPE_DIM // 2, 1) * sb

    cq = proj(_C_CQ, SWA_WIDTH) * (HEAD_DIM ** -0.5)
    ck = proj(_C_CK, SWA_WIDTH)
    cv = proj(_C_CV, SWA_WIDTH)
    for s in range(SWA_SLABS):
        sl = slice(s * LANES, (s + 1) * LANES)
        cq_ref[s] = rope(cq[:, sl])
        ck_ref[s] = rope(ck[:, sl])
        cv_ref[s] = cv[:, sl]


def _inproj(x, gn, w, wg, bg, cos, sa, sb, seq):
    n = x.shape[0]
    tm = TOKEN_TILE
    tiles_per_seq = seq // tm
    row = lambda width: pl.BlockSpec((tm, width), lambda i: (i, 0))
    slab = pl.BlockSpec((SWA_SLABS, tm, LANES), lambda i: (0, i, 0))
    table = pl.BlockSpec((tm, LANES), lambda i: (i % tiles_per_seq, 0))
    sds = jax.ShapeDtypeStruct
    out_shapes = (
        sds((n, GLA_QK_PAD), BF16), sds((n, GLA_QK_PAD), BF16), sds((n, GLA_QK_PAD), BF16),
        sds((n, GLA_QK_PAD), BF16), sds((n // GLA_CHUNK, 1, GLA_QK_PAD), F32),
        sds((n, GLA_V_PAD), BF16), sds((n, GLA_V_PAD), BF16),
        sds((n, LRU_WIDTH), F32), sds((n, LRU_WIDTH), F32),
        sds((SWA_SLABS, n, LANES), F32), sds((SWA_SLABS, n, LANES), F32), sds((SWA_SLABS, n, LANES), F32),
    )
    return pl.pallas_call(
        _inproj_kernel,
        grid=(n // tm,),
        in_specs=[row(D_MODEL), _resident((1, D_MODEL)), _resident((D_MODEL, IN_COLS_PACKED)),
                  _resident((GLR_PAD, GLA_QK_PAD)), _resident((1, GLA_QK_PAD)), table, table, table],
        out_specs=(row(GLA_QK_PAD), row(GLA_QK_PAD), row(GLA_QK_PAD), row(GLA_QK_PAD),
                   pl.BlockSpec((tm // GLA_CHUNK, 1, GLA_QK_PAD), lambda i: (i, 0, 0)),
                   row(GLA_V_PAD), row(GLA_V_PAD), row(LRU_WIDTH), row(LRU_WIDTH), slab, slab, slab),
        out_shape=out_shapes,
        compiler_params=_params(("parallel",), 48),
        name="inproj",
    )(x, gn, w, wg, bg, cos, sa, sb)


def _gla_kernel(qe_ref, ke_ref, qi_ref, kd_ref, dec_ref, v_ref, r_ref, gn_ref, o_ref, st_ref, *, seq):
    c = GLA_CHUNK
    st_ref[...] = jnp.zeros_like(st_ref)

    lane_q = lax.broadcasted_iota(jnp.int32, (1, GLA_QK_PAD), 1)
    q_masks = [((lane_q >= h * GLA_DK) & (lane_q < (h + 1) * GLA_DK)).astype(BF16) for h in range(GLA_HEADS)]
    ri = lax.broadcasted_iota(jnp.int32, (GLA_HEADS * c, c), 0) & (c - 1)
    ci = lax.broadcasted_iota(jnp.int32, (GLA_HEADS * c, c), 1)
    causal = ci <= ri
    gn = gn_ref[...]

    def chunk(i, carry):
        rows = pl.ds(pl.multiple_of(i * c, c), c)
        qe = qe_ref[rows, :]
        ke = ke_ref[rows, :]
        kd = kd_ref[rows, :]
        vb = v_ref[rows, :]
        gate = r_ref[rows, :]
        st = st_ref[...]
        o_inter = _dot_nt(qi_ref[rows, :], st.astype(BF16))
        s = _dot_nt(jnp.concatenate([qe * m for m in q_masks], axis=0), ke)
        s = jnp.where(causal, s, 0.0).astype(BF16)
        dec = dec_ref[i]
        for h in range(GLA_HEADS):
            sl = slice(h * LANES, (h + 1) * LANES)
            vh = vb[:, sl]
            o = o_inter[:, sl] + _dot(s[h * c:(h + 1) * c, :], vh)
            st_ref[sl, :] = st[sl, :] * dec + _dot_tn(vh, kd * q_masks[h])
            ms = jnp.sum(o * o, axis=-1, keepdims=True) * (1.0 / GLA_DV)
            y = o * lax.rsqrt(ms + EPS) * gn[:, sl] * gate[:, sl].astype(F32)
            o_ref[rows, sl] = y.astype(o_ref.dtype)
        return carry

    lax.fori_loop(0, seq // c, chunk, 0, unroll=2)


def _gla(qe, ke, qi, kd, dec, av, ar, gn, seq):
    n = qe.shape[0]
    blk = lambda width: pl.BlockSpec((seq, width), lambda b: (b, 0))
    return pl.pallas_call(
        functools.partial(_gla_kernel, seq=seq),
        grid=(n // seq,),
        in_specs=[blk(GLA_QK_PAD), blk(GLA_QK_PAD), blk(GLA_QK_PAD), blk(GLA_QK_PAD),
                  pl.BlockSpec((seq // GLA_CHUNK, 1, GLA_QK_PAD), lambda b: (b, 0, 0)),
                  blk(GLA_V_PAD), blk(GLA_V_PAD), _resident((1, GLA_V_PAD))],
        out_specs=blk(GLA_V_PAD),
        out_shape=jax.ShapeDtypeStruct((n, GLA_V_PAD), BF16),
        scratch_shapes=[pltpu.VMEM((GLA_V_PAD, GLA_QK_PAD), F32)],
        compiler_params=_params(("parallel",), 40),
        name="gla",
    )(qe, ke, qi, kd, dec, av, ar, gn)


def _lru_kernel(gate_ref, x_ref, cw_ref, cb_ref, wa_ref, ba_ref, wx_ref, bx_ref, lam_ref, gn_ref,
                o_ref, xp_ref, hc_ref, *, seq):
    t = LRU_TILE
    pad = 8
    xp_ref[0:pad, :] = jnp.zeros((pad, LRU_WIDTH), F32)
    xp_ref[pad:, :] = x_ref[...]
    hc_ref[...] = jnp.zeros_like(hc_ref)
    neg_lam = -lam_ref[...]
    softplus = jnp.maximum(neg_lam, 0.0) + jnp.log(1.0 + jnp.exp(-jnp.abs(neg_lam)))
    cw = cw_ref[...]
    row = lax.broadcasted_iota(jnp.int32, (t, 1), 0)

    def tile(i, carry):
        t0 = pl.multiple_of(i * t, t)
        xt = xp_ref[pl.ds(t0, t + pad), :]
        u = cb_ref[...] + xt[pad:, :] * cw[LRU_CONV - 1:LRU_CONV, :]
        for kk in range(LRU_CONV - 1):
            u = u + pltpu.roll(xt, LRU_CONV - 1 - kk, 0)[pad:, :] * cw[kk:kk + 1, :]
        ub = u.astype(BF16)
        r = jax.nn.sigmoid(_dot(ub, wa_ref[...]) + ba_ref[...])
        ig = jax.nn.sigmoid(_dot(ub, wx_ref[...]) + bx_ref[...])
        log_a = (-LRU_C) * r * softplus
        a = jnp.exp(log_a)
        th = jnp.tanh(log_a)
        h = jnp.sqrt(-2.0 * th / (1.0 - th)) * (ig * u)
        shift = 1
        while shift < t:
            keep = row >= shift
            h_prev = jnp.where(keep, pltpu.roll(h, shift, 0), 0.0)
            a_prev = jnp.where(keep, pltpu.roll(a, shift, 0), 1.0)
            h = h + a * h_prev
            a = a * a_prev
            shift *= 2
        h = h + a * hc_ref[...]
        hc_ref[...] = h[t - 1:t, :]
        y = h * _gelu_tanh(gate_ref[pl.ds(t0, t), :])
        o_ref[pl.ds(t0, t), :] = (y * _rms_scale(y) * gn_ref[...]).astype(o_ref.dtype)
        return carry

    lax.fori_loop(0, seq // t, tile, 0)


def _lru(bgate, bin_, cw, cb, wa, ba, wx, bx, lam, gn, seq):
    n = bgate.shape[0]
    blk = pl.BlockSpec((seq, LRU_WIDTH), lambda b: (b, 0))
    vec = _resident((1, LRU_WIDTH))
    mat = _resident((LRU_WIDTH, LRU_WIDTH))
    return pl.pallas_call(
        functools.partial(_lru_kernel, seq=seq),
        grid=(n // seq,),
        in_specs=[blk, blk, _resident((LRU_CONV, LRU_WIDTH)), vec, mat, vec, mat, vec, vec, vec],
        out_specs=blk,
        out_shape=jax.ShapeDtypeStruct((n, LRU_WIDTH), BF16),
        scratch_shapes=[pltpu.VMEM((seq + 8, LRU_WIDTH), F32), pltpu.VMEM((1, LRU_WIDTH), F32)],
        compiler_params=_params(("parallel",), 32),
        name="lru",
    )(bgate, bin_, cw, cb, wa, ba, wx, bx, lam, gn)


def _swa_kernel(q_ref, k_ref, v_ref, gn_ref, o_ref, qd_ref, kd_ref, vd_ref, acc_ref, m_ref, l_ref, *, seq):
    blk = SWA_BLOCK
    res = SWA_RES
    nj = seq // res
    assert nj == blk
    lane = lax.broadcasted_iota(jnp.int32, (1, LANES), 1)
    head0 = lane < HEAD_DIM

    def permuted_masks(groups):
        q_per = blk // groups
        k_per = 2 * blk // groups
        qi = lax.broadcasted_iota(jnp.int32, (blk, 2 * blk), 0)
        ki = lax.broadcasted_iota(jnp.int32, (blk, 2 * blk), 1)
        tq = (qi % q_per) * groups + qi // q_per
        tk = (ki % k_per) * groups + ki // k_per
        two_blocks = (tk >= tq) & (tk <= tq + blk)
        first_block = tk <= tq
        return two_blocks, first_block

    def attend_all(units):
        both = lambda t: jnp.where(head0, t[:blk], t[blk:])
        scores = []
        for qs, kcat, _, _ in units:
            q2 = jnp.concatenate([jnp.where(head0, qs, 0.0), jnp.where(head0, 0.0, qs)], axis=0)
            scores.append(_dot_nt(q2.astype(BF16), kcat.astype(BF16)))
        soft = []
        for s, (_, _, _, mask) in zip(scores, units):
            s = jnp.where(jnp.concatenate([mask, mask], axis=0), s, NEG_BIG)
            m = jnp.max(s, axis=-1, keepdims=True)
            soft.append((m, jnp.exp(s - m).astype(BF16)))
        outs = []
        for (m, p), (_, _, vcat, _) in zip(soft, units):
            v_ones = jnp.concatenate([vcat.astype(BF16), jnp.ones(vcat.shape, BF16)], axis=1)
            pv = _dot(p, v_ones)
            outs.append((both(m), both(pv[:, LANES:]), both(pv[:, :LANES])))
        return outs

    def merge(old, new):
        (m_old, l_old, acc_old), (m_new, l_new, pv) = old, new
        m_tot = jnp.maximum(m_old, m_new)
        w_old = jnp.exp(m_old - m_tot)
        w_new = jnp.exp(m_new - m_tot)
        return m_tot, w_old * l_old + w_new * l_new, w_old * acc_old + w_new * pv

    def regroup(c, carry):
        rows = pl.ds(c, nj, stride=res)
        for s in range(SWA_SLABS):
            qd_ref[c, s] = q_ref[s, rows, :]
            kd_ref[c, s] = k_ref[s, rows, :]
            vd_ref[c, s] = v_ref[s, rows, :]
        return carry

    lax.fori_loop(0, res, regroup, 0)

    qi1 = lax.broadcasted_iota(jnp.int32, (blk, blk), 0)
    kj1 = lax.broadcasted_iota(jnp.int32, (blk, blk), 1)
    causal = kj1 <= qi1

    def dil16(i, carry):
        units = [(2 * i + u, s) for u in range(2) for s in range(SWA_SLABS)]
        loaded = [(qd_ref[c, s], kd_ref[c, s], vd_ref[c, s]) for c, s in units]
        results = attend_all([(qs, ks, vs, causal) for qs, ks, vs in loaded])
        for (c, s), (m_new, l_new, pv) in zip(units, results):
            m_ref[c, s] = m_new
            l_ref[c, s] = l_new
            acc_ref[c, s] = pv
        return carry

    lax.fori_loop(0, res // 2, dil16, 0)

    def dilated(groups):
        n_blocks = nj * groups // blk
        n_res = res // groups
        q_per = blk // groups
        k_per = 2 * blk // groups
        two_blocks, first_block = permuted_masks(groups)
        nb_bits = n_blocks.bit_length() - 1
        assert n_blocks == 1 << nb_bits

        def step(i, carry):
            loaded = []
            for u in range(2):
                it = 2 * i + u
                r = lax.shift_right_logical(it, nb_bits)
                bi = it & (n_blocks - 1)
                q_rows = pl.ds(pl.multiple_of(bi * q_per, q_per), q_per)
                k_rows = pl.ds(pl.multiple_of(jnp.maximum(bi - 1, 0) * q_per, q_per), k_per)
                mask = (two_blocks & (bi > 0)) | (first_block & (bi == 0))
                gather = lambda ref, s, rows, r=r: jnp.concatenate(
                    [ref[a * n_res + r, s, rows, :] for a in range(groups)], axis=0)
                for s in range(SWA_SLABS):
                    old = (gather(m_ref, s, q_rows), gather(l_ref, s, q_rows), gather(acc_ref, s, q_rows))
                    loaded.append((r, s, q_rows, mask, gather(qd_ref, s, q_rows), gather(kd_ref, s, k_rows),
                                   gather(vd_ref, s, k_rows), old))
            news = attend_all([(qs, kcat, vcat, mask) for _, _, _, mask, qs, kcat, vcat, _ in loaded])
            results = [merge(unit[-1], new) for unit, new in zip(loaded, news)]
            for (r, s, q_rows, *_), (m_new, l_new, acc_new) in zip(loaded, results):
                for a in range(groups):
                    piece = slice(a * q_per, (a + 1) * q_per)
                    m_ref[a * n_res + r, s, q_rows, :] = m_new[piece]
                    l_ref[a * n_res + r, s, q_rows, :] = l_new[piece]
                    acc_ref[a * n_res + r, s, q_rows, :] = acc_new[piece]
            return carry

        lax.fori_loop(0, n_res * n_blocks // 2, step, 0)

    dilated(4)
    dilated(16)

    def finish(c, carry):
        outs = [acc_ref[c, s] / l_ref[c, s] for s in range(SWA_SLABS)]
        ss = sum(jnp.sum(o * o, axis=-1, keepdims=True) for o in outs)
        scale = lax.rsqrt(ss * (1.0 / SWA_WIDTH) + EPS)
        rows = pl.ds(c, nj, stride=res)
        for s in range(SWA_SLABS):
            o_ref[s, rows, :] = outs[s] * scale * gn_ref[:, s * LANES:(s + 1) * LANES]
        return carry

    lax.fori_loop(0, res, finish, 0)


def _swa(cq, ck, cv, gn, seq):
    n = cq.shape[1]
    slab = pl.BlockSpec((SWA_SLABS, seq, LANES), lambda b: (0, b, 0))
    grouped = pltpu.VMEM((SWA_RES, SWA_SLABS, seq // SWA_RES, LANES), F32)
    return pl.pallas_call(
        functools.partial(_swa_kernel, seq=seq),
        grid=(n // seq,),
        in_specs=[slab, slab, slab, _resident((1, SWA_WIDTH))],
        out_specs=slab,
        out_shape=jax.ShapeDtypeStruct((SWA_SLABS, n, LANES), F32),
        scratch_shapes=[grouped] * 6,
        compiler_params=_params(("parallel",), 56),
        name="swa",
    )(cq, ck, cv, gn)


def _ffn_kernel(x_ref, oa_ref, ob_ref, oc_ref, woa_ref, wob_ref, woc_ref, gn_ref, wup_ref, cw_ref, cb_ref,
                wdn_ref, gfin_ref, out_ref, carry_ref, act_ref, *, tiles_per_seq, final):
    tm = x_ref.shape[0]

    @pl.when(pl.program_id(0) % tiles_per_seq == 0)
    def _():
        carry_ref[...] = jnp.zeros_like(carry_ref)

    x1 = x_ref[...] + _dot(oa_ref[...], woa_ref[...]) + _dot(ob_ref[...], wob_ref[...])
    for s in range(SWA_SLABS):
        x1 = x1 + _dot(oc_ref[s].astype(BF16), woc_ref[s * LANES:(s + 1) * LANES, :])
    h = (x1 * _rms_scale(x1) * gn_ref[...]).astype(BF16)

    def conv_up(c0):
        cols = slice(c0, c0 + FFN_CHUNK)
        up = _dot(h, wup_ref[:, cols])
        ext = jnp.concatenate([carry_ref[:, cols], up], axis=0)
        carry_ref[:, cols] = up[tm - 8:tm, :]
        cw = cw_ref[:, cols]
        return (cb_ref[:, cols] + up * cw[2:3, :] + pltpu.roll(ext, 2, 0)[8:, :] * cw[0:1, :]
                + pltpu.roll(ext, 1, 0)[8:, :] * cw[1:2, :])

    for j in range(FFN_DIM // FFN_CHUNK):
        val = conv_up(j * FFN_CHUNK)
        gate = conv_up(FFN_DIM + j * FFN_CHUNK)
        act_ref[:, j * FFN_CHUNK:(j + 1) * FFN_CHUNK] = (_gelu_tanh(gate) * val).astype(BF16)
    x2 = x1 + _dot(act_ref[...], wdn_ref[...])
    if final:
        x2 = x2 * _rms_scale(x2) * gfin_ref[...]
    out_ref[...] = x2


def _ffn(x, oa, ob, oc, woa, wob, woc, gn, wup, cw, cb, wdn, gfin, seq, final):
    n = x.shape[0]
    tm = TOKEN_TILE
    row = lambda width: pl.BlockSpec((tm, width), lambda i: (i, 0))
    return pl.pallas_call(
        functools.partial(_ffn_kernel, tiles_per_seq=seq // tm, final=final),
        grid=(n // tm,),
        in_specs=[row(D_MODEL), row(GLA_V_PAD), row(LRU_WIDTH),
                  pl.BlockSpec((SWA_SLABS, tm, LANES), lambda i: (0, i, 0)),
                  _resident((GLA_V_PAD, D_MODEL)), _resident((LRU_WIDTH, D_MODEL)), _resident((SWA_WIDTH, D_MODEL)),
                  _resident((1, D_MODEL)), _resident((D_MODEL, 2 * FFN_DIM)), _resident((FFN_CONV, 2 * FFN_DIM)),
                  _resident((1, 2 * FFN_DIM)), _resident((FFN_DIM, D_MODEL)), _resident((1, D_MODEL))],
        out_specs=row(D_MODEL),
        out_shape=jax.ShapeDtypeStruct((n, D_MODEL), F32),
        scratch_shapes=[pltpu.VMEM((8, 2 * FFN_DIM), F32), pltpu.VMEM((tm, FFN_DIM), BF16)],
        compiler_params=_params(("arbitrary",), 56),
        name="ffn",
    )(x, oa, ob, oc, woa, wob, woc, gn, wup, cw, cb, wdn, gfin)


def _rope_tables(seq):
    half = ROPE_DIM // 2
    inv = ROPE_THETA ** (-jnp.arange(0, ROPE_DIM, 2, dtype=F32) / ROPE_DIM)
    ang = jnp.arange(seq, dtype=F32)[:, None] * inv[None, :]
    cos, sin = jnp.cos(ang), jnp.sin(ang)
    ones = jnp.ones((seq, HEAD_DIM - ROPE_DIM), F32)
    zeros = jnp.zeros((seq, HEAD_DIM - ROPE_DIM), F32)
    zh = jnp.zeros((seq, half), F32)
    per_head = lambda parts: jnp.tile(jnp.concatenate(parts, axis=1), (1, LANES // HEAD_DIM))
    return per_head([cos, cos, ones]), per_head([-sin, zh, zeros]), per_head([zh, sin, zeros])


def _pad_heads(w, axis):
    shape = list(w.shape)
    shape[axis:axis + 1] = [GLA_HEADS, GLA_DV]
    w = w.reshape(shape)
    pad = [(0, 0)] * w.ndim
    pad[axis + 1] = (0, LANES - GLA_DV)
    shape[axis:axis + 2] = [GLA_V_PAD]
    return jnp.pad(w, pad).reshape(shape)


def _pack_w_in(w):
    z = lambda width: jnp.zeros((w.shape[0], width), w.dtype)
    o_v = 2 * GLA_QK
    o_r = o_v + GLA_WIDTH
    o_glr = o_r + GLA_WIDTH
    return jnp.concatenate([
        w[:, 0:GLA_QK], z(GLA_QK_PAD - GLA_QK), w[:, GLA_QK:o_v], z(GLA_QK_PAD - GLA_QK),
        _pad_heads(w[:, o_v:o_r], 1), _pad_heads(w[:, o_r:o_glr], 1),
        w[:, o_glr + GLA_GATE_RANK:], w[:, o_glr:o_glr + GLA_GATE_RANK], z(GLR_PAD - GLA_GATE_RANK)],
        axis=1).astype(BF16)


def _block_diag(w):
    nb, bs, _ = w.shape
    eye = jnp.eye(nb, dtype=w.dtype)
    return jnp.einsum('nij,nm->nimj', w, eye).reshape(nb * bs, nb * bs)


def kernel(x, norm_mix, w_in, gla_w_gate, gla_b_gate, gla_norm, lru_conv_w, lru_conv_b, lru_w_a, lru_b_a,
           lru_w_x, lru_b_x, lru_lambda, lru_norm, swa_norm, w_out, norm_ffn, ffn_w_up, ffn_conv_w, ffn_conv_b,
           ffn_w_down, norm_final):
    batch, seq, _ = x.shape
    depth = w_in.shape[0]
    assert seq % TOKEN_TILE == 0 and seq == SWA_BLOCK * SWA_RES
    cos, sa, sb = _rope_tables(seq)
    xf = x.reshape(batch * seq, D_MODEL)
    row = lambda v: v.reshape(1, -1)
    for l in range(depth):
        wg = jnp.zeros((GLR_PAD, GLA_QK_PAD), F32).at[:GLA_GATE_RANK, :GLA_QK].set(gla_w_gate[l]).astype(BF16)
        bg = jnp.zeros((1, GLA_QK_PAD), F32).at[0, :GLA_QK].set(gla_b_gate[l])
        qe, ke, qi, kd, dec, av, ar, bgate, bin_, cq, ck, cv = _inproj(
            xf, row(norm_mix[l]), _pack_w_in(w_in[l]), wg, bg, cos, sa, sb, seq)
        o_a = _gla(qe, ke, qi, kd, dec, av, ar, row(_pad_heads(jnp.tile(gla_norm[l], GLA_HEADS), 0)), seq)
        o_b = _lru(bgate, bin_, lru_conv_w[l], row(lru_conv_b[l]), _block_diag(lru_w_a[l]).astype(BF16),
                   row(lru_b_a[l]), _block_diag(lru_w_x[l]).astype(BF16), row(lru_b_x[l]),
                   row(lru_lambda[l]), row(lru_norm[l]), seq)
        o_c = _swa(cq, ck, cv, row(swa_norm[l]), seq)
        wo = w_out[l].astype(BF16)
        xf = _ffn(xf, o_a, o_b, o_c, _pad_heads(wo[:GLA_WIDTH], 0), wo[GLA_WIDTH:GLA_WIDTH + LRU_WIDTH],
                  wo[GLA_WIDTH + LRU_WIDTH:], row(norm_ffn[l]), ffn_w_up[l].astype(BF16), ffn_conv_w[l],
                  row(ffn_conv_b[l]), ffn_w_down[l].astype(BF16), row(norm_final), seq, l == depth - 1)
    return xf.reshape(batch, seq, D_MODEL)
```

```python
import functools

import jax
import jax.numpy as jnp
import numpy as np
from jax import lax
from jax.experimental import pallas as pl
from jax.experimental.pallas import tpu as pltpu

F32 = jnp.float32
BF16 = jnp.bfloat16

D_MODEL = 1024
EPS = 1e-6
HEAD_DIM = 64
LANES = 128
GLA_HEADS = 4
GLA_DK = 48
GLA_DV = 96
GLA_QK = GLA_HEADS * GLA_DK
GLA_QK_PAD = 256
GLA_WIDTH = GLA_HEADS * GLA_DV
GLA_V_PAD = GLA_HEADS * LANES
GLA_GATE_RANK = 16
GLA_GATE_TEMP = 16.0
GLA_CHUNK = 64
LRU_WIDTH = 256
LRU_BLOCKS = 8
LRU_C = 8.0
LRU_CONV = 4
SWA_HEADS = 6
SWA_WIDTH = SWA_HEADS * HEAD_DIM
SWA_BLOCK = 128
SWA_SLABS = SWA_WIDTH // LANES
SWA_RES = 16
ROPE_THETA = 500000.0
ROPE_DIM = HEAD_DIM // 4
FFN_DIM = 2816
FFN_CONV = 3
FFN_CHUNK = 256
GLR_PAD = 128

_C_AQ = 0
_C_AK = _C_AQ + GLA_QK_PAD
_C_AV = _C_AK + GLA_QK_PAD
_C_AR = _C_AV + GLA_V_PAD
_C_BG = _C_AR + GLA_V_PAD
_C_BI = _C_BG + LRU_WIDTH
_C_CQ = _C_BI + LRU_WIDTH
_C_CK = _C_CQ + SWA_WIDTH
_C_CV = _C_CK + SWA_WIDTH
_C_GLR = _C_CV + SWA_WIDTH
IN_COLS_PACKED = _C_GLR + GLR_PAD

TOKEN_TILE = 512
V7X_VMEM_BYTES = 64 * 1024 * 1024
NEG_BIG = -1e30
LOG2E = float(np.log2(np.e))


def _params(semantics, vmem_mib):
    assert vmem_mib * 1024 * 1024 < V7X_VMEM_BYTES
    return pltpu.CompilerParams(dimension_semantics=semantics,
                                vmem_limit_bytes=vmem_mib * 1024 * 1024)


def _resident(shape):
    nd = len(shape)
    return pl.BlockSpec(shape, lambda *_: (0,) * nd, pipeline_mode=pl.Buffered(1))


def _rms_scale(x):
    return lax.rsqrt(jnp.mean(x * x, axis=-1, keepdims=True) + EPS)


def _gelu_tanh(x):
    c = float(np.sqrt(2.0 / np.pi))
    return x * (0.5 + 0.5 * jnp.tanh(x * (c + (0.044715 * c) * (x * x))))


def _dot(a, b):
    return jnp.dot(a, b, preferred_element_type=F32)


def _dot_nt(a, b):
    return lax.dot_general(a, b, (((1,), (1,)), ((), ())), preferred_element_type=F32)


def _dot_tn(a, b):
    return lax.dot_general(a, b, (((0,), (0,)), ((), ())), preferred_element_type=F32)


def _inproj_kernel(x_ref, gn_ref, w_ref, wg_ref, bg_ref, cos_ref, sa_ref, sb_ref,
                   cw_ref, cb_ref, wa_ref, ba_ref, wx_ref, bx_ref, lam_ref, gnb_ref,
                   qe_ref, ke_ref, qi_ref, kd_ref, dec_ref, av_ref, ar_ref, ob_ref, cq_ref, ck_ref, cv_ref,
                   xtail_ref, hc_ref, rg_ref, *, tiles_per_seq):
    tm = x_ref.shape[0]
    c = GLA_CHUNK

    @pl.when(pl.program_id(0) % tiles_per_seq == 0)
    def _():
        xtail_ref[...] = jnp.zeros_like(xtail_ref)
        hc_ref[...] = jnp.zeros_like(hc_ref)

    x = x_ref[...]
    h = (x * _rms_scale(x) * gn_ref[...]).astype(BF16)

    def proj(c0, width):
        return _dot(h, w_ref[:, c0:c0 + width])

    p_glr = proj(_C_GLR, GLR_PAD)
    p_bi = proj(_C_BI, LRU_WIDTH)
    z = _dot(p_glr.astype(BF16), wg_ref[...]) + bg_ref[...]
    p_aq = proj(_C_AQ, GLA_QK_PAD)

    ext = jnp.concatenate([xtail_ref[...], p_bi], axis=0)
    xtail_ref[...] = p_bi[tm - 8:tm, :]
    cw = cw_ref[...]
    u = cb_ref[...] + p_bi * cw[LRU_CONV - 1:LRU_CONV, :]
    for kk in range(LRU_CONV - 1):
        u = u + pltpu.roll(ext, LRU_CONV - 1 - kk, 0)[8:, :] * cw[kk:kk + 1, :]
    ub = u.astype(BF16)
    rg_pre = _dot(ub, wa_ref[...])
    ig_pre = _dot(ub, wx_ref[...])
    p_ak = proj(_C_AK, GLA_QK_PAD)

    g = (jnp.minimum(z, 0.0) - jnp.log(1.0 + jnp.exp(-jnp.abs(z)))) * (1.0 / GLA_GATE_TEMP)
    row_in_chunk = lax.broadcasted_iota(jnp.int32, (tm, 1), 0) & (c - 1)
    b = g
    shift = 1
    while shift < c:
        b = b + jnp.where(row_in_chunk >= shift, pltpu.roll(b, shift, 0), 0.0)
        shift *= 2
    p_bg = proj(_C_BG, LRU_WIDTH)

    rg = jax.nn.sigmoid(rg_pre + ba_ref[...])
    ig = jax.nn.sigmoid(ig_pre + bx_ref[...])
    neg_lam = -lam_ref[...]
    softplus = jnp.maximum(neg_lam, 0.0) + jnp.log(1.0 + jnp.exp(-jnp.abs(neg_lam)))
    log_a = (-LRU_C) * rg * softplus
    a = jnp.exp(log_a)
    th = jnp.tanh(log_a)
    hs = jnp.sqrt(-2.0 * th / (1.0 - th)) * (ig * u)
    p_av = proj(_C_AV, GLA_V_PAD)

    row = lax.broadcasted_iota(jnp.int32, (tm, 1), 0)
    shift = 1
    while shift < tm:
        if shift < 8:
            keep = row >= shift
            h_prev = jnp.where(keep, pltpu.roll(hs, shift, 0), 0.0)
            a_prev = jnp.where(keep, pltpu.roll(a, shift, 0), 1.0)
        else:
            h_prev = jnp.concatenate([jnp.zeros((shift, LRU_WIDTH), F32), hs[:tm - shift]], axis=0)
            a_prev = jnp.concatenate([jnp.ones((shift, LRU_WIDTH), F32), a[:tm - shift]], axis=0)
        hs = hs + a * h_prev
        a = a * a_prev
        shift *= 2
    hs = hs + a * hc_ref[...]
    hc_ref[...] = hs[tm - 1:tm, :]
    p_ar = proj(_C_AR, GLA_V_PAD)

    b3 = b.reshape(tm // c, c, GLA_QK_PAD)
    b_mid = b3[:, c // 2 - 1:c // 2, :]
    b_last = b3[:, c - 1:c, :]
    q3 = (p_aq * (GLA_DK ** -0.5)).reshape(tm // c, c, GLA_QK_PAD)
    k3 = p_ak.reshape(tm // c, c, GLA_QK_PAD)
    flat = lambda t: t.reshape(tm, GLA_QK_PAD).astype(BF16)
    qe_ref[...] = flat(q3 * jnp.exp(b3 - b_mid))
    ke_ref[...] = flat(k3 * jnp.exp(b_mid - b3))
    qi_ref[...] = flat(q3 * jnp.exp(b3))
    kd_ref[...] = flat(k3 * jnp.exp(b_last - b3))
    dec_ref[...] = jnp.exp(b_last)
    p_cq = proj(_C_CQ, SWA_WIDTH)

    y = hs * _gelu_tanh(p_bg)
    ob_ref[...] = (y * _rms_scale(y) * gnb_ref[...]).astype(ob_ref.dtype)
    av_ref[...] = p_av.astype(BF16)
    ar_ref[...] = (p_ar * jax.nn.sigmoid(p_ar)).astype(BF16)
    p_ck = proj(_C_CK, SWA_WIDTH)

    cos = cos_ref[...]
    sa = sa_ref[...]
    sb = sb_ref[...]

    def rope(t):
        return t * cos + pltpu.roll(t, LANES - ROPE_DIM // 2, 1) * sa + pltpu.roll(t, ROPE_DIM // 2, 1) * sb

    def regroup(out_ref, slot0, val, rotary):
        for s in range(SWA_SLABS):
            piece = val[:, s * LANES:(s + 1) * LANES]
            rg_ref[slot0 + s] = rope(piece) if rotary else piece
            for cc in range(SWA_RES):
                out_ref[cc, s] = rg_ref[slot0 + s, pl.ds(cc, tm // SWA_RES, stride=SWA_RES), :]

    regroup(cq_ref, 0, p_cq * (HEAD_DIM ** -0.5 * LOG2E), True)
    p_cv = proj(_C_CV, SWA_WIDTH)
    regroup(ck_ref, SWA_SLABS, p_ck, True)
    regroup(cv_ref, 2 * SWA_SLABS, p_cv, False)


def _inproj(x, gn, w, wg, bg, cos, sa, sb, lru, seq):
    n = x.shape[0]
    tm = TOKEN_TILE
    tiles_per_seq = seq // tm
    row = lambda width: pl.BlockSpec((tm, width), lambda i: (i, 0))
    grouped = pl.BlockSpec((SWA_RES, SWA_SLABS, tm // SWA_RES, LANES), lambda i: (0, 0, i, 0))
    table = pl.BlockSpec((tm, LANES), lambda i: (i % tiles_per_seq, 0))
    vec = _resident((1, LRU_WIDTH))
    mat = _resident((LRU_WIDTH, LRU_WIDTH))
    sds = jax.ShapeDtypeStruct
    grouped_shape = sds((SWA_RES, SWA_SLABS, n // SWA_RES, LANES), F32)
    out_shapes = (
        sds((n, GLA_QK_PAD), BF16), sds((n, GLA_QK_PAD), BF16), sds((n, GLA_QK_PAD), BF16),
        sds((n, GLA_QK_PAD), BF16), sds((n // GLA_CHUNK, 1, GLA_QK_PAD), F32),
        sds((n, GLA_V_PAD), BF16), sds((n, GLA_V_PAD), BF16), sds((n, LRU_WIDTH), BF16),
        grouped_shape, grouped_shape, grouped_shape,
    )
    return pl.pallas_call(
        functools.partial(_inproj_kernel, tiles_per_seq=tiles_per_seq),
        grid=(n // tm,),
        in_specs=[row(D_MODEL), _resident((1, D_MODEL)), _resident((D_MODEL, IN_COLS_PACKED)),
                  _resident((GLR_PAD, GLA_QK_PAD)), _resident((1, GLA_QK_PAD)), table, table, table,
                  _resident((LRU_CONV, LRU_WIDTH)), vec, mat, vec, mat, vec, vec, vec],
        out_specs=(row(GLA_QK_PAD), row(GLA_QK_PAD), row(GLA_QK_PAD), row(GLA_QK_PAD),
                   pl.BlockSpec((tm // GLA_CHUNK, 1, GLA_QK_PAD), lambda i: (i, 0, 0)),
                   row(GLA_V_PAD), row(GLA_V_PAD), row(LRU_WIDTH), grouped, grouped, grouped),
        out_shape=out_shapes,
        scratch_shapes=[pltpu.VMEM((8, LRU_WIDTH), F32), pltpu.VMEM((1, LRU_WIDTH), F32),
                        pltpu.VMEM((3 * SWA_SLABS, tm, LANES), F32)],
        compiler_params=_params(("arbitrary",), 48),
        name="inproj",
    )(x, gn, w, wg, bg, cos, sa, sb, *lru)


def _gla_kernel(qe_ref, ke_ref, qi_ref, kd_ref, dec_ref, v_ref, r_ref, gn_ref, o_ref, st_ref, *, seq):
    c = GLA_CHUNK
    st_ref[...] = jnp.zeros_like(st_ref)

    lane_q = lax.broadcasted_iota(jnp.int32, (1, GLA_QK_PAD), 1)
    q_masks = [((lane_q >= h * GLA_DK) & (lane_q < (h + 1) * GLA_DK)).astype(BF16) for h in range(GLA_HEADS)]
    ri = lax.broadcasted_iota(jnp.int32, (GLA_HEADS * c, c), 0) & (c - 1)
    ci = lax.broadcasted_iota(jnp.int32, (GLA_HEADS * c, c), 1)
    causal = ci <= ri
    gn = gn_ref[...]

    def chunk(i, carry):
        rows = pl.ds(pl.multiple_of(i * c, c), c)
        qe = qe_ref[rows, :]
        ke = ke_ref[rows, :]
        kd = kd_ref[rows, :]
        vb = v_ref[rows, :]
        gate = r_ref[rows, :]
        st = st_ref[...]
        o_inter = _dot_nt(qi_ref[rows, :], st.astype(BF16))
        s = _dot_nt(jnp.concatenate([qe * m for m in q_masks], axis=0), ke)
        s = jnp.where(causal, s, 0.0).astype(BF16)
        dec = dec_ref[i]
        for h in range(GLA_HEADS):
            sl = slice(h * LANES, (h + 1) * LANES)
            vh = vb[:, sl]
            o = o_inter[:, sl] + _dot(s[h * c:(h + 1) * c, :], vh)
            st_ref[sl, :] = st[sl, :] * dec + _dot_tn(vh, kd * q_masks[h])
            ms = jnp.sum(o * o, axis=-1, keepdims=True) * (1.0 / GLA_DV)
            y = o * lax.rsqrt(ms + EPS) * gn[:, sl] * gate[:, sl].astype(F32)
            o_ref[rows, sl] = y.astype(o_ref.dtype)
        return carry

    lax.fori_loop(0, seq // c, chunk, 0, unroll=2)


def _gla(qe, ke, qi, kd, dec, av, ar, gn, seq):
    n = qe.shape[0]
    blk = lambda width: pl.BlockSpec((seq, width), lambda b: (b, 0))
    return pl.pallas_call(
        functools.partial(_gla_kernel, seq=seq),
        grid=(n // seq,),
        in_specs=[blk(GLA_QK_PAD), blk(GLA_QK_PAD), blk(GLA_QK_PAD), blk(GLA_QK_PAD),
                  pl.BlockSpec((seq // GLA_CHUNK, 1, GLA_QK_PAD), lambda b: (b, 0, 0)),
                  blk(GLA_V_PAD), blk(GLA_V_PAD), _resident((1, GLA_V_PAD))],
        out_specs=blk(GLA_V_PAD),
        out_shape=jax.ShapeDtypeStruct((n, GLA_V_PAD), BF16),
        scratch_shapes=[pltpu.VMEM((GLA_V_PAD, GLA_QK_PAD), F32)],
        compiler_params=_params(("parallel",), 40),
        name="gla",
    )(qe, ke, qi, kd, dec, av, ar, gn)


def _lru_kernel(gate_ref, x_ref, cw_ref, cb_ref, wa_ref, ba_ref, wx_ref, bx_ref, lam_ref, gn_ref,
                o_ref, xp_ref, hc_ref, *, seq):
    t = LRU_TILE
    pad = 8
    xp_ref[0:pad, :] = jnp.zeros((pad, LRU_WIDTH), F32)
    xp_ref[pad:, :] = x_ref[...]
    hc_ref[...] = jnp.zeros_like(hc_ref)
    neg_lam = -lam_ref[...]
    softplus = jnp.maximum(neg_lam, 0.0) + jnp.log(1.0 + jnp.exp(-jnp.abs(neg_lam)))
    cw = cw_ref[...]
    row = lax.broadcasted_iota(jnp.int32, (t, 1), 0)

    def tile(i, carry):
        t0 = pl.multiple_of(i * t, t)
        xt = xp_ref[pl.ds(t0, t + pad), :]
        u = cb_ref[...] + xt[pad:, :] * cw[LRU_CONV - 1:LRU_CONV, :]
        for kk in range(LRU_CONV - 1):
            u = u + pltpu.roll(xt, LRU_CONV - 1 - kk, 0)[pad:, :] * cw[kk:kk + 1, :]
        ub = u.astype(BF16)
        r = jax.nn.sigmoid(_dot(ub, wa_ref[...]) + ba_ref[...])
        ig = jax.nn.sigmoid(_dot(ub, wx_ref[...]) + bx_ref[...])
        log_a = (-LRU_C) * r * softplus
        a = jnp.exp(log_a)
        th = jnp.tanh(log_a)
        h = jnp.sqrt(-2.0 * th / (1.0 - th)) * (ig * u)
        shift = 1
        while shift < t:
            keep = row >= shift
            h_prev = jnp.where(keep, pltpu.roll(h, shift, 0), 0.0)
            a_prev = jnp.where(keep, pltpu.roll(a, shift, 0), 1.0)
            h = h + a * h_prev
            a = a * a_prev
            shift *= 2
        h = h + a * hc_ref[...]
        hc_ref[...] = h[t - 1:t, :]
        y = h * _gelu_tanh(gate_ref[pl.ds(t0, t), :])
        o_ref[pl.ds(t0, t), :] = (y * _rms_scale(y) * gn_ref[...]).astype(o_ref.dtype)
        return carry

    lax.fori_loop(0, seq // t, tile, 0)


def _lru(bgate, bin_, cw, cb, wa, ba, wx, bx, lam, gn, seq):
    n = bgate.shape[0]
    blk = pl.BlockSpec((seq, LRU_WIDTH), lambda b: (b, 0))
    vec = _resident((1, LRU_WIDTH))
    mat = _resident((LRU_WIDTH, LRU_WIDTH))
    return pl.pallas_call(
        functools.partial(_lru_kernel, seq=seq),
        grid=(n // seq,),
        in_specs=[blk, blk, _resident((LRU_CONV, LRU_WIDTH)), vec, mat, vec, mat, vec, vec, vec],
        out_specs=blk,
        out_shape=jax.ShapeDtypeStruct((n, LRU_WIDTH), BF16),
        scratch_shapes=[pltpu.VMEM((seq + 8, LRU_WIDTH), F32), pltpu.VMEM((1, LRU_WIDTH), F32)],
        compiler_params=_params(("parallel",), 32),
        name="lru",
    )(bgate, bin_, cw, cb, wa, ba, wx, bx, lam, gn)


def _swa_kernel(qd_ref, kd_ref, vd_ref, gn_ref, o_ref, acc_ref, m_ref, l_ref, *, seq):
    blk = SWA_BLOCK
    res = SWA_RES
    nj = seq // res
    assert nj == blk
    lane = lax.broadcasted_iota(jnp.int32, (1, LANES), 1)
    head0 = lane < HEAD_DIM

    def permuted_masks(groups):
        q_per = blk // groups
        k_per = 2 * blk // groups
        qi = lax.broadcasted_iota(jnp.int32, (blk, 2 * blk), 0)
        ki = lax.broadcasted_iota(jnp.int32, (blk, 2 * blk), 1)
        tq = (qi % q_per) * groups + qi // q_per
        tk = (ki % k_per) * groups + ki // k_per
        two_blocks = (tk >= tq) & (tk <= tq + blk)
        first_block = tk <= tq
        return two_blocks, first_block

    def attend_all(units):
        both = lambda t: jnp.where(head0, t[:blk], t[blk:])
        scores = []
        for qs, kcat, _, _ in units:
            q2 = jnp.concatenate([jnp.where(head0, qs, 0.0), jnp.where(head0, 0.0, qs)], axis=0)
            scores.append(_dot_nt(q2.astype(BF16), kcat.astype(BF16)))
        soft = []
        for s, (_, _, _, mask) in zip(scores, units):
            s = jnp.where(jnp.concatenate([mask, mask], axis=0), s, NEG_BIG)
            m = jnp.max(s, axis=-1, keepdims=True)
            soft.append((m, jnp.exp2(s - m).astype(BF16)))
        outs = []
        for (m, p), (_, _, vcat, _) in zip(soft, units):
            v_ones = jnp.concatenate([vcat.astype(BF16), jnp.ones(vcat.shape, BF16)], axis=1)
            pv = _dot(p, v_ones)
            outs.append((both(m), both(pv[:, LANES:]), both(pv[:, :LANES])))
        return outs

    def merge(old, new):
        (m_old, l_old, acc_old), (m_new, l_new, pv) = old, new
        m_tot = jnp.maximum(m_old, m_new)
        w_old = jnp.exp2(m_old - m_tot)
        w_new = jnp.exp2(m_new - m_tot)
        return m_tot, w_old * l_old + w_new * l_new, w_old * acc_old + w_new * pv

    qi1 = lax.broadcasted_iota(jnp.int32, (blk, blk), 0)
    kj1 = lax.broadcasted_iota(jnp.int32, (blk, blk), 1)
    causal = kj1 <= qi1

    def dil16(i, carry):
        units = [(2 * i + u, s) for u in range(2) for s in range(SWA_SLABS)]
        loaded = [(qd_ref[c, s], kd_ref[c, s], vd_ref[c, s]) for c, s in units]
        results = attend_all([(qs, ks, vs, causal) for qs, ks, vs in loaded])
        for (c, s), (m_new, l_new, pv) in zip(units, results):
            m_ref[c, s] = m_new
            l_ref[c, s] = l_new
            acc_ref[c, s] = pv
        return carry

    lax.fori_loop(0, res // 2, dil16, 0)

    def dilated(groups):
        n_blocks = nj * groups // blk
        n_res = res // groups
        q_per = blk // groups
        k_per = 2 * blk // groups
        two_blocks, first_block = permuted_masks(groups)
        nb_bits = n_blocks.bit_length() - 1
        assert n_blocks == 1 << nb_bits

        def step(i, carry):
            loaded = []
            for u in range(2):
                it = 2 * i + u
                r = lax.shift_right_logical(it, nb_bits)
                bi = it & (n_blocks - 1)
                q_rows = pl.ds(pl.multiple_of(bi * q_per, q_per), q_per)
                k_rows = pl.ds(pl.multiple_of(jnp.maximum(bi - 1, 0) * q_per, q_per), k_per)
                mask = (two_blocks & (bi > 0)) | (first_block & (bi == 0))
                gather = lambda ref, s, rows, r=r: jnp.concatenate(
                    [ref[a * n_res + r, s, rows, :] for a in range(groups)], axis=0)
                for s in range(SWA_SLABS):
                    old = (gather(m_ref, s, q_rows), gather(l_ref, s, q_rows), gather(acc_ref, s, q_rows))
                    loaded.append((r, s, q_rows, mask, gather(qd_ref, s, q_rows), gather(kd_ref, s, k_rows),
                                   gather(vd_ref, s, k_rows), old))
            news = attend_all([(qs, kcat, vcat, mask) for _, _, _, mask, qs, kcat, vcat, _ in loaded])
            results = [merge(unit[-1], new) for unit, new in zip(loaded, news)]
            for (r, s, q_rows, *_), (m_new, l_new, acc_new) in zip(loaded, results):
                for a in range(groups):
                    piece = slice(a * q_per, (a + 1) * q_per)
                    m_ref[a * n_res + r, s, q_rows, :] = m_new[piece]
                    l_ref[a * n_res + r, s, q_rows, :] = l_new[piece]
                    acc_ref[a * n_res + r, s, q_rows, :] = acc_new[piece]
            return carry

        lax.fori_loop(0, n_res * n_blocks // 2, step, 0)

    dilated(4)
    dilated(16)

    def finish(c, carry):
        outs = [acc_ref[c, s] / l_ref[c, s] for s in range(SWA_SLABS)]
        ss = sum(jnp.sum(o * o, axis=-1, keepdims=True) for o in outs)
        scale = lax.rsqrt(ss * (1.0 / SWA_WIDTH) + EPS)
        rows = pl.ds(c, nj, stride=res)
        for s in range(SWA_SLABS):
            o_ref[s, rows, :] = outs[s] * scale * gn_ref[:, s * LANES:(s + 1) * LANES]
        return carry

    lax.fori_loop(0, res, finish, 0)


def _swa(cq, ck, cv, gn, seq):
    n = cq.shape[2] * SWA_RES
    nj = seq // SWA_RES
    grouped = pl.BlockSpec((SWA_RES, SWA_SLABS, nj, LANES), lambda b: (0, 0, b, 0))
    state = pltpu.VMEM((SWA_RES, SWA_SLABS, nj, LANES), F32)
    return pl.pallas_call(
        functools.partial(_swa_kernel, seq=seq),
        grid=(n // seq,),
        in_specs=[grouped, grouped, grouped, _resident((1, SWA_WIDTH))],
        out_specs=pl.BlockSpec((SWA_SLABS, seq, LANES), lambda b: (0, b, 0)),
        out_shape=jax.ShapeDtypeStruct((SWA_SLABS, n, LANES), F32),
        scratch_shapes=[state] * 3,
        compiler_params=_params(("parallel",), 48),
        name="swa",
    )(cq, ck, cv, gn)


def _ffn_kernel(x_ref, oa_ref, ob_ref, oc_ref, woa_ref, wob_ref, woc_ref, gn_ref, wup_ref, cw_ref, cb_ref,
                wdn_ref, gfin_ref, out_ref, carry_ref, act_ref, *, tiles_per_seq, final):
    tm = x_ref.shape[0]

    @pl.when(pl.program_id(0) % tiles_per_seq == 0)
    def _():
        carry_ref[...] = jnp.zeros_like(carry_ref)

    x1 = x_ref[...] + _dot(oa_ref[...], woa_ref[...]) + _dot(ob_ref[...], wob_ref[...])
    for s in range(SWA_SLABS):
        x1 = x1 + _dot(oc_ref[s].astype(BF16), woc_ref[s * LANES:(s + 1) * LANES, :])
    h = (x1 * _rms_scale(x1) * gn_ref[...]).astype(BF16)

    def conv_up(c0):
        cols = slice(c0, c0 + FFN_CHUNK)
        up = _dot(h, wup_ref[:, cols])
        ext = jnp.concatenate([carry_ref[:, cols], up], axis=0)
        carry_ref[:, cols] = up[tm - 8:tm, :]
        cw = cw_ref[:, cols]
        return (cb_ref[:, cols] + up * cw[2:3, :] + pltpu.roll(ext, 2, 0)[8:, :] * cw[0:1, :]
                + pltpu.roll(ext, 1, 0)[8:, :] * cw[1:2, :])

    for j in range(FFN_DIM // FFN_CHUNK):
        val = conv_up(j * FFN_CHUNK)
        gate = conv_up(FFN_DIM + j * FFN_CHUNK)
        act_ref[:, j * FFN_CHUNK:(j + 1) * FFN_CHUNK] = (_gelu_tanh(gate) * val).astype(BF16)
    x2 = x1 + _dot(act_ref[...], wdn_ref[...])
    if final:
        x2 = x2 * _rms_scale(x2) * gfin_ref[...]
    out_ref[...] = x2


def _ffn(x, oa, ob, oc, woa, wob, woc, gn, wup, cw, cb, wdn, gfin, seq, final):
    n = x.shape[0]
    tm = TOKEN_TILE
    row = lambda width: pl.BlockSpec((tm, width), lambda i: (i, 0))
    return pl.pallas_call(
        functools.partial(_ffn_kernel, tiles_per_seq=seq // tm, final=final),
        grid=(n // tm,),
        in_specs=[row(D_MODEL), row(GLA_V_PAD), row(LRU_WIDTH),
                  pl.BlockSpec((SWA_SLABS, tm, LANES), lambda i: (0, i, 0)),
                  _resident((GLA_V_PAD, D_MODEL)), _resident((LRU_WIDTH, D_MODEL)), _resident((SWA_WIDTH, D_MODEL)),
                  _resident((1, D_MODEL)), _resident((D_MODEL, 2 * FFN_DIM)), _resident((FFN_CONV, 2 * FFN_DIM)),
                  _resident((1, 2 * FFN_DIM)), _resident((FFN_DIM, D_MODEL)), _resident((1, D_MODEL))],
        out_specs=row(D_MODEL),
        out_shape=jax.ShapeDtypeStruct((n, D_MODEL), F32),
        scratch_shapes=[pltpu.VMEM((8, 2 * FFN_DIM), F32), pltpu.VMEM((tm, FFN_DIM), BF16)],
        compiler_params=_params(("arbitrary",), 56),
        name="ffn",
    )(x, oa, ob, oc, woa, wob, woc, gn, wup, cw, cb, wdn, gfin)


def _rope_tables(seq):
    half = ROPE_DIM // 2
    inv = ROPE_THETA ** (-jnp.arange(0, ROPE_DIM, 2, dtype=F32) / ROPE_DIM)
    ang = jnp.arange(seq, dtype=F32)[:, None] * inv[None, :]
    cos, sin = jnp.cos(ang), jnp.sin(ang)
    ones = jnp.ones((seq, HEAD_DIM - ROPE_DIM), F32)
    zeros = jnp.zeros((seq, HEAD_DIM - ROPE_DIM), F32)
    zh = jnp.zeros((seq, half), F32)
    per_head = lambda parts: jnp.tile(jnp.concatenate(parts, axis=1), (1, LANES // HEAD_DIM))
    return per_head([cos, cos, ones]), per_head([-sin, zh, zeros]), per_head([zh, sin, zeros])


def _pad_heads(w, axis):
    shape = list(w.shape)
    shape[axis:axis + 1] = [GLA_HEADS, GLA_DV]
    w = w.reshape(shape)
    pad = [(0, 0)] * w.ndim
    pad[axis + 1] = (0, LANES - GLA_DV)
    shape[axis:axis + 2] = [GLA_V_PAD]
    return jnp.pad(w, pad).reshape(shape)


def _pack_w_in(w):
    z = lambda width: jnp.zeros((w.shape[0], width), w.dtype)
    o_v = 2 * GLA_QK
    o_r = o_v + GLA_WIDTH
    o_glr = o_r + GLA_WIDTH
    return jnp.concatenate([
        w[:, 0:GLA_QK], z(GLA_QK_PAD - GLA_QK), w[:, GLA_QK:o_v], z(GLA_QK_PAD - GLA_QK),
        _pad_heads(w[:, o_v:o_r], 1), _pad_heads(w[:, o_r:o_glr], 1),
        w[:, o_glr + GLA_GATE_RANK:], w[:, o_glr:o_glr + GLA_GATE_RANK], z(GLR_PAD - GLA_GATE_RANK)],
        axis=1).astype(BF16)


def _block_diag(w):
    nb, bs, _ = w.shape
    eye = jnp.eye(nb, dtype=w.dtype)
    return jnp.einsum('nij,nm->nimj', w, eye).reshape(nb * bs, nb * bs)


def kernel(x, norm_mix, w_in, gla_w_gate, gla_b_gate, gla_norm, lru_conv_w, lru_conv_b, lru_w_a, lru_b_a,
           lru_w_x, lru_b_x, lru_lambda, lru_norm, swa_norm, w_out, norm_ffn, ffn_w_up, ffn_conv_w, ffn_conv_b,
           ffn_w_down, norm_final):
    batch, seq, _ = x.shape
    depth = w_in.shape[0]
    assert seq % TOKEN_TILE == 0 and seq == SWA_BLOCK * SWA_RES
    cos, sa, sb = _rope_tables(seq)
    xf = x.reshape(batch * seq, D_MODEL)
    row = lambda v: v.reshape(1, -1)
    for l in range(depth):
        wg = jnp.zeros((GLR_PAD, GLA_QK_PAD), F32).at[:GLA_GATE_RANK, :GLA_QK].set(gla_w_gate[l]).astype(BF16)
        bg = jnp.zeros((1, GLA_QK_PAD), F32).at[0, :GLA_QK].set(gla_b_gate[l])
        lru = (lru_conv_w[l], row(lru_conv_b[l]), _block_diag(lru_w_a[l]).astype(BF16), row(lru_b_a[l]),
               _block_diag(lru_w_x[l]).astype(BF16), row(lru_b_x[l]), row(lru_lambda[l]), row(lru_norm[l]))
        qe, ke, qi, kd, dec, av, ar, o_b, cq, ck, cv = _inproj(
            xf, row(norm_mix[l]), _pack_w_in(w_in[l]), wg, bg, cos, sa, sb, lru, seq)
        o_a = _gla(qe, ke, qi, kd, dec, av, ar, row(_pad_heads(jnp.tile(gla_norm[l], GLA_HEADS), 0)), seq)
        o_c = _swa(cq, ck, cv, row(swa_norm[l]), seq)
        wo = w_out[l].astype(BF16)
        xf = _ffn(xf, o_a, o_b, o_c, _pad_heads(wo[:GLA_WIDTH], 0), wo[GLA_WIDTH:GLA_WIDTH + LRU_WIDTH],
                  wo[GLA_WIDTH + LRU_WIDTH:], row(norm_ffn[l]), ffn_w_up[l].astype(BF16), ffn_conv_w[l],
                  row(ffn_conv_b[l]), ffn_w_down[l].astype(BF16), row(norm_final), seq, l == depth - 1)
    return xf.reshape(batch, seq, D_MODEL)
```

```python
import functools

import jax
import jax.numpy as jnp
import numpy as np
from jax import lax
from jax.experimental import pallas as pl
from jax.experimental.pallas import tpu as pltpu

F32 = jnp.float32
BF16 = jnp.bfloat16

D_MODEL = 1024
EPS = 1e-6
HEAD_DIM = 64
LANES = 128
GLA_HEADS = 4
GLA_DK = 48
GLA_DV = 96
GLA_QK = GLA_HEADS * GLA_DK
GLA_QK_PAD = 256
GLA_WIDTH = GLA_HEADS * GLA_DV
GLA_V_PAD = GLA_HEADS * LANES
GLA_GATE_RANK = 16
GLA_GATE_TEMP = 16.0
GLA_CHUNK = 64
LRU_WIDTH = 256
LRU_BLOCKS = 8
LRU_C = 8.0
LRU_CONV = 4
SWA_HEADS = 6
SWA_WIDTH = SWA_HEADS * HEAD_DIM
SWA_BLOCK = 128
SWA_SLABS = SWA_WIDTH // LANES
SWA_RES = 16
ROPE_THETA = 500000.0
ROPE_DIM = HEAD_DIM // 4
FFN_DIM = 2816
FFN_CONV = 3
FFN_CHUNK = 256
GLR_PAD = 128

_C_AQ = 0
_C_AK = _C_AQ + GLA_QK_PAD
_C_AV = _C_AK + GLA_QK_PAD
_C_AR = _C_AV + GLA_V_PAD
_C_BG = _C_AR + GLA_V_PAD
_C_BI = _C_BG + LRU_WIDTH
_C_CQ = _C_BI + LRU_WIDTH
_C_CK = _C_CQ + SWA_WIDTH
_C_CV = _C_CK + SWA_WIDTH
_C_GLR = _C_CV + SWA_WIDTH
IN_COLS_PACKED = _C_GLR + GLR_PAD

TOKEN_TILE = 512
V7X_VMEM_BYTES = 64 * 1024 * 1024
NEG_BIG = -1e30
LOG2E = float(np.log2(np.e))


def _params(semantics, vmem_mib):
    assert vmem_mib * 1024 * 1024 < V7X_VMEM_BYTES
    return pltpu.CompilerParams(dimension_semantics=semantics,
                                vmem_limit_bytes=vmem_mib * 1024 * 1024)


def _resident(shape):
    nd = len(shape)
    return pl.BlockSpec(shape, lambda *_: (0,) * nd, pipeline_mode=pl.Buffered(1))


def _rms_scale(x):
    return lax.rsqrt(jnp.mean(x * x, axis=-1, keepdims=True) + EPS)


def _gelu_tanh(x):
    c = float(np.sqrt(2.0 / np.pi))
    return x * (0.5 + 0.5 * jnp.tanh(x * (c + (0.044715 * c) * (x * x))))


def _dot(a, b):
    return jnp.dot(a, b, preferred_element_type=F32)


def _dot_nt(a, b):
    return lax.dot_general(a, b, (((1,), (1,)), ((), ())), preferred_element_type=F32)


def _dot_tn(a, b):
    return lax.dot_general(a, b, (((0,), (0,)), ((), ())), preferred_element_type=F32)


def _inproj_kernel(x_ref, gn_ref, w_ref, wg_ref, bg_ref, cos_ref, sa_ref, sb_ref,
                   cw_ref, cb_ref, wa_ref, ba_ref, wx_ref, bx_ref, lam_ref, gnb_ref,
                   qe_ref, ke_ref, qi_ref, kd_ref, dec_ref, av_ref, ar_ref, ob_ref, cq_ref, ck_ref, cv_ref,
                   xtail_ref, hc_ref, rg_ref, *, tiles_per_seq):
    tm = x_ref.shape[0]
    c = GLA_CHUNK

    @pl.when(pl.program_id(0) % tiles_per_seq == 0)
    def _():
        xtail_ref[...] = jnp.zeros_like(xtail_ref)
        hc_ref[...] = jnp.zeros_like(hc_ref)

    x = x_ref[...]
    h = (x * _rms_scale(x) * gn_ref[...]).astype(BF16)

    def proj(c0, width):
        return _dot(h, w_ref[:, c0:c0 + width])

    tile = 256
    p_glr = proj(_C_GLR, GLR_PAD)
    p_bi = proj(_C_BI, LRU_WIDTH)
    z = _dot(p_glr.astype(BF16), wg_ref[...]) + bg_ref[...]
    p_aq = proj(_C_AQ, GLA_QK_PAD)

    ext = jnp.concatenate([xtail_ref[...], p_bi], axis=0)
    xtail_ref[...] = p_bi[tm - 8:tm, :]
    cw = cw_ref[...]
    u = cb_ref[...] + p_bi * cw[LRU_CONV - 1:LRU_CONV, :]
    for kk in range(LRU_CONV - 1):
        u = u + pltpu.roll(ext, LRU_CONV - 1 - kk, 0)[8:, :] * cw[kk:kk + 1, :]
    ub = u.astype(BF16)
    rg_pre = _dot(ub, wa_ref[...])
    ig_pre = _dot(ub, wx_ref[...])
    p_ak = proj(_C_AK, GLA_QK_PAD)

    g = (jnp.minimum(z, 0.0) - jnp.log(1.0 + jnp.exp(-jnp.abs(z)))) * (1.0 / GLA_GATE_TEMP)
    row_in_chunk = lax.broadcasted_iota(jnp.int32, (tm, 1), 0) & (c - 1)
    b = g
    shift = 1
    while shift < c:
        b = b + jnp.where(row_in_chunk >= shift, pltpu.roll(b, shift, 0), 0.0)
        if shift == 4:
            p_bg = proj(_C_BG, LRU_WIDTH)
        shift *= 2

    rg = jax.nn.sigmoid(rg_pre + ba_ref[...])
    ig = jax.nn.sigmoid(ig_pre + bx_ref[...])
    neg_lam = -lam_ref[...]
    softplus = jnp.maximum(neg_lam, 0.0) + jnp.log(1.0 + jnp.exp(-jnp.abs(neg_lam)))
    log_a = (-LRU_C) * rg * softplus
    a = jnp.exp(log_a)
    th = jnp.tanh(log_a)
    hs = jnp.sqrt(-2.0 * th / (1.0 - th)) * (ig * u)
    av_ref[:, :tile] = proj(_C_AV, tile).astype(BF16)

    row = lax.broadcasted_iota(jnp.int32, (tm, 1), 0)
    silu = lambda t: (t * jax.nn.sigmoid(t)).astype(BF16)
    shift = 1
    while shift < tm:
        if shift < 8:
            keep = row >= shift
            h_prev = jnp.where(keep, pltpu.roll(hs, shift, 0), 0.0)
            a_prev = jnp.where(keep, pltpu.roll(a, shift, 0), 1.0)
        else:
            h_prev = jnp.concatenate([jnp.zeros((shift, LRU_WIDTH), F32), hs[:tm - shift]], axis=0)
            a_prev = jnp.concatenate([jnp.ones((shift, LRU_WIDTH), F32), a[:tm - shift]], axis=0)
        hs = hs + a * h_prev
        a = a * a_prev
        if shift == 2:
            av_ref[:, tile:] = proj(_C_AV + tile, tile).astype(BF16)
        elif shift == 8:
            ar_ref[:, :tile] = silu(proj(_C_AR, tile))
        elif shift == 64:
            ar_ref[:, tile:] = silu(proj(_C_AR + tile, tile))
        shift *= 2
    hs = hs + a * hc_ref[...]
    hc_ref[...] = hs[tm - 1:tm, :]
    p_c0 = proj(_C_CQ, tile)

    b3 = b.reshape(tm // c, c, GLA_QK_PAD)
    b_mid = b3[:, c // 2 - 1:c // 2, :]
    b_last = b3[:, c - 1:c, :]
    q3 = (p_aq * (GLA_DK ** -0.5)).reshape(tm // c, c, GLA_QK_PAD)
    k3 = p_ak.reshape(tm // c, c, GLA_QK_PAD)
    flat = lambda t: t.reshape(tm, GLA_QK_PAD).astype(BF16)
    qe_ref[...] = flat(q3 * jnp.exp(b3 - b_mid))
    ke_ref[...] = flat(k3 * jnp.exp(b_mid - b3))
    p_c1 = proj(_C_CQ + tile, tile)
    qi_ref[...] = flat(q3 * jnp.exp(b3))
    kd_ref[...] = flat(k3 * jnp.exp(b_last - b3))
    dec_ref[...] = jnp.exp(b_last)
    p_c2 = proj(_C_CQ + 2 * tile, tile)

    y = hs * _gelu_tanh(p_bg)
    ob_ref[...] = (y * _rms_scale(y) * gnb_ref[...]).astype(ob_ref.dtype)
    p_c3 = proj(_C_CQ + 3 * tile, tile)

    cos = cos_ref[...]
    sa = sa_ref[...]
    sb = sb_ref[...]

    def rope(t):
        return t * cos + pltpu.roll(t, LANES - ROPE_DIM // 2, 1) * sa + pltpu.roll(t, ROPE_DIM // 2, 1) * sb

    def regroup(out_ref, s, slot, val):
        rg_ref[slot] = val
        for cc in range(SWA_RES):
            out_ref[cc, s] = rg_ref[slot, pl.ds(cc, tm // SWA_RES, stride=SWA_RES), :]

    q_scale = HEAD_DIM ** -0.5 * LOG2E
    regroup(cq_ref, 0, 0, rope(p_c0[:, :LANES] * q_scale))
    regroup(cq_ref, 1, 1, rope(p_c0[:, LANES:] * q_scale))
    p_c4 = proj(_C_CQ + 4 * tile, LANES)
    regroup(cq_ref, 2, 2, rope(p_c1[:, :LANES] * q_scale))
    regroup(ck_ref, 0, 3, rope(p_c1[:, LANES:]))
    regroup(ck_ref, 1, 4, rope(p_c2[:, :LANES]))
    regroup(ck_ref, 2, 5, rope(p_c2[:, LANES:]))
    regroup(cv_ref, 0, 6, p_c3[:, :LANES])
    regroup(cv_ref, 1, 7, p_c3[:, LANES:])
    regroup(cv_ref, 2, 8, p_c4)


def _inproj(x, gn, w, wg, bg, cos, sa, sb, lru, seq):
    n = x.shape[0]
    tm = TOKEN_TILE
    tiles_per_seq = seq // tm
    row = lambda width: pl.BlockSpec((tm, width), lambda i: (i, 0))
    grouped = pl.BlockSpec((SWA_RES, SWA_SLABS, tm // SWA_RES, LANES), lambda i: (0, 0, i, 0))
    table = pl.BlockSpec((tm, LANES), lambda i: (i % tiles_per_seq, 0))
    vec = _resident((1, LRU_WIDTH))
    mat = _resident((LRU_WIDTH, LRU_WIDTH))
    sds = jax.ShapeDtypeStruct
    grouped_shape = sds((SWA_RES, SWA_SLABS, n // SWA_RES, LANES), F32)
    out_shapes = (
        sds((n, GLA_QK_PAD), BF16), sds((n, GLA_QK_PAD), BF16), sds((n, GLA_QK_PAD), BF16),
        sds((n, GLA_QK_PAD), BF16), sds((n // GLA_CHUNK, 1, GLA_QK_PAD), F32),
        sds((n, GLA_V_PAD), BF16), sds((n, GLA_V_PAD), BF16), sds((n, LRU_WIDTH), BF16),
        grouped_shape, grouped_shape, grouped_shape,
    )
    return pl.pallas_call(
        functools.partial(_inproj_kernel, tiles_per_seq=tiles_per_seq),
        grid=(n // tm,),
        in_specs=[row(D_MODEL), _resident((1, D_MODEL)), _resident((D_MODEL, IN_COLS_PACKED)),
                  _resident((GLR_PAD, GLA_QK_PAD)), _resident((1, GLA_QK_PAD)), table, table, table,
                  _resident((LRU_CONV, LRU_WIDTH)), vec, mat, vec, mat, vec, vec, vec],
        out_specs=(row(GLA_QK_PAD), row(GLA_QK_PAD), row(GLA_QK_PAD), row(GLA_QK_PAD),
                   pl.BlockSpec((tm // GLA_CHUNK, 1, GLA_QK_PAD), lambda i: (i, 0, 0)),
                   row(GLA_V_PAD), row(GLA_V_PAD), row(LRU_WIDTH), grouped, grouped, grouped),
        out_shape=out_shapes,
        scratch_shapes=[pltpu.VMEM((8, LRU_WIDTH), F32), pltpu.VMEM((1, LRU_WIDTH), F32),
                        pltpu.VMEM((3 * SWA_SLABS, tm, LANES), F32)],
        compiler_params=_params(("arbitrary",), 48),
        name="inproj",
    )(x, gn, w, wg, bg, cos, sa, sb, *lru)


def _gla_kernel(qe_ref, ke_ref, qi_ref, kd_ref, dec_ref, v_ref, r_ref, gn_ref, o_ref, st_ref, *, seq):
    c = GLA_CHUNK
    st_ref[...] = jnp.zeros_like(st_ref)

    lane_q = lax.broadcasted_iota(jnp.int32, (1, GLA_QK_PAD), 1)
    q_masks = [((lane_q >= h * GLA_DK) & (lane_q < (h + 1) * GLA_DK)).astype(BF16) for h in range(GLA_HEADS)]
    ri = lax.broadcasted_iota(jnp.int32, (GLA_HEADS * c, c), 0) & (c - 1)
    ci = lax.broadcasted_iota(jnp.int32, (GLA_HEADS * c, c), 1)
    causal = ci <= ri
    gn = gn_ref[...]

    per_step = 8
    heads = [slice(h * LANES, (h + 1) * LANES) for h in range(GLA_HEADS)]

    def step(i, carry):
        ids = [per_step * i + u for u in range(per_step)]
        rows = [pl.ds(pl.multiple_of(ci * c, c), c) for ci in ids]
        vbs = [v_ref[r, :] for r in rows]
        scores = [_dot_nt(jnp.concatenate([qe_ref[r, :] * m for m in q_masks], axis=0), ke_ref[r, :])
                  for r in rows]
        incs = []
        for r, vb in zip(rows, vbs):
            kd = kd_ref[r, :]
            incs.append([_dot_tn(vb[:, sl], kd * q_masks[h]) for h, sl in enumerate(heads)])
        st = [st_ref[sl, :] for sl in heads]
        for ci, r, vb, s, inc in zip(ids, rows, vbs, scores, incs):
            o_inter = _dot_nt(qi_ref[r, :], jnp.concatenate(st, axis=0).astype(BF16))
            s = jnp.where(causal, s, 0.0).astype(BF16)
            dec = dec_ref[ci]
            gate = r_ref[r, :]
            for h, sl in enumerate(heads):
                o = o_inter[:, sl] + _dot(s[h * c:(h + 1) * c, :], vb[:, sl])
                st[h] = st[h] * dec + inc[h]
                ms = jnp.sum(o * o, axis=-1, keepdims=True) * (1.0 / GLA_DV)
                y = o * lax.rsqrt(ms + EPS) * gn[:, sl] * gate[:, sl].astype(F32)
                o_ref[r, sl] = y.astype(o_ref.dtype)
        for h, sl in enumerate(heads):
            st_ref[sl, :] = st[h]
        return carry

    lax.fori_loop(0, seq // (c * per_step), step, 0)


def _gla(qe, ke, qi, kd, dec, av, ar, gn, seq):
    n = qe.shape[0]
    blk = lambda width: pl.BlockSpec((seq, width), lambda b: (b, 0))
    return pl.pallas_call(
        functools.partial(_gla_kernel, seq=seq),
        grid=(n // seq,),
        in_specs=[blk(GLA_QK_PAD), blk(GLA_QK_PAD), blk(GLA_QK_PAD), blk(GLA_QK_PAD),
                  pl.BlockSpec((seq // GLA_CHUNK, 1, GLA_QK_PAD), lambda b: (b, 0, 0)),
                  blk(GLA_V_PAD), blk(GLA_V_PAD), _resident((1, GLA_V_PAD))],
        out_specs=blk(GLA_V_PAD),
        out_shape=jax.ShapeDtypeStruct((n, GLA_V_PAD), BF16),
        scratch_shapes=[pltpu.VMEM((GLA_V_PAD, GLA_QK_PAD), F32)],
        compiler_params=_params(("parallel",), 40),
        name="gla",
    )(qe, ke, qi, kd, dec, av, ar, gn)


def _lru_kernel(gate_ref, x_ref, cw_ref, cb_ref, wa_ref, ba_ref, wx_ref, bx_ref, lam_ref, gn_ref,
                o_ref, xp_ref, hc_ref, *, seq):
    t = LRU_TILE
    pad = 8
    xp_ref[0:pad, :] = jnp.zeros((pad, LRU_WIDTH), F32)
    xp_ref[pad:, :] = x_ref[...]
    hc_ref[...] = jnp.zeros_like(hc_ref)
    neg_lam = -lam_ref[...]
    softplus = jnp.maximum(neg_lam, 0.0) + jnp.log(1.0 + jnp.exp(-jnp.abs(neg_lam)))
    cw = cw_ref[...]
    row = lax.broadcasted_iota(jnp.int32, (t, 1), 0)

    def tile(i, carry):
        t0 = pl.multiple_of(i * t, t)
        xt = xp_ref[pl.ds(t0, t + pad), :]
        u = cb_ref[...] + xt[pad:, :] * cw[LRU_CONV - 1:LRU_CONV, :]
        for kk in range(LRU_CONV - 1):
            u = u + pltpu.roll(xt, LRU_CONV - 1 - kk, 0)[pad:, :] * cw[kk:kk + 1, :]
        ub = u.astype(BF16)
        r = jax.nn.sigmoid(_dot(ub, wa_ref[...]) + ba_ref[...])
        ig = jax.nn.sigmoid(_dot(ub, wx_ref[...]) + bx_ref[...])
        log_a = (-LRU_C) * r * softplus
        a = jnp.exp(log_a)
        th = jnp.tanh(log_a)
        h = jnp.sqrt(-2.0 * th / (1.0 - th)) * (ig * u)
        shift = 1
        while shift < t:
            keep = row >= shift
            h_prev = jnp.where(keep, pltpu.roll(h, shift, 0), 0.0)
            a_prev = jnp.where(keep, pltpu.roll(a, shift, 0), 1.0)
            h = h + a * h_prev
            a = a * a_prev
            shift *= 2
        h = h + a * hc_ref[...]
        hc_ref[...] = h[t - 1:t, :]
        y = h * _gelu_tanh(gate_ref[pl.ds(t0, t), :])
        o_ref[pl.ds(t0, t), :] = (y * _rms_scale(y) * gn_ref[...]).astype(o_ref.dtype)
        return carry

    lax.fori_loop(0, seq // t, tile, 0)


def _lru(bgate, bin_, cw, cb, wa, ba, wx, bx, lam, gn, seq):
    n = bgate.shape[0]
    blk = pl.BlockSpec((seq, LRU_WIDTH), lambda b: (b, 0))
    vec = _resident((1, LRU_WIDTH))
    mat = _resident((LRU_WIDTH, LRU_WIDTH))
    return pl.pallas_call(
        functools.partial(_lru_kernel, seq=seq),
        grid=(n // seq,),
        in_specs=[blk, blk, _resident((LRU_CONV, LRU_WIDTH)), vec, mat, vec, mat, vec, vec, vec],
        out_specs=blk,
        out_shape=jax.ShapeDtypeStruct((n, LRU_WIDTH), BF16),
        scratch_shapes=[pltpu.VMEM((seq + 8, LRU_WIDTH), F32), pltpu.VMEM((1, LRU_WIDTH), F32)],
        compiler_params=_params(("parallel",), 32),
        name="lru",
    )(bgate, bin_, cw, cb, wa, ba, wx, bx, lam, gn)


def _swa_kernel(qd_ref, kd_ref, vd_ref, gn_ref, o_ref, acc_ref, m_ref, l_ref, *, seq):
    blk = SWA_BLOCK
    res = SWA_RES
    nj = seq // res
    assert nj == blk
    lane = lax.broadcasted_iota(jnp.int32, (1, LANES), 1)
    head0 = lane < HEAD_DIM

    def permuted_masks(groups):
        q_per = blk // groups
        k_per = 2 * blk // groups
        qi = lax.broadcasted_iota(jnp.int32, (blk, 2 * blk), 0)
        ki = lax.broadcasted_iota(jnp.int32, (blk, 2 * blk), 1)
        tq = (qi % q_per) * groups + qi // q_per
        tk = (ki % k_per) * groups + ki // k_per
        two_blocks = (tk >= tq) & (tk <= tq + blk)
        first_block = tk <= tq
        return two_blocks, first_block

    def attend_all(units):
        both = lambda t: jnp.where(head0, t[:blk], t[blk:])
        scores = []
        for qs, kcat, _, _ in units:
            q2 = jnp.concatenate([jnp.where(head0, qs, 0.0), jnp.where(head0, 0.0, qs)], axis=0)
            scores.append(_dot_nt(q2.astype(BF16), kcat.astype(BF16)))
        soft = []
        for s, (_, _, _, mask) in zip(scores, units):
            s = jnp.where(jnp.concatenate([mask, mask], axis=0), s, NEG_BIG)
            m = jnp.max(s, axis=-1, keepdims=True)
            soft.append((m, jnp.exp2(s - m).astype(BF16)))
        outs = []
        for (m, p), (_, _, vcat, _) in zip(soft, units):
            v_ones = jnp.concatenate([vcat.astype(BF16), jnp.ones(vcat.shape, BF16)], axis=1)
            pv = _dot(p, v_ones)
            outs.append((both(m), both(pv[:, LANES:]), both(pv[:, :LANES])))
        return outs

    def merge(old, new):
        (m_old, l_old, acc_old), (m_new, l_new, pv) = old, new
        m_tot = jnp.maximum(m_old, m_new)
        w_old = jnp.exp2(m_old - m_tot)
        w_new = jnp.exp2(m_new - m_tot)
        return m_tot, w_old * l_old + w_new * l_new, w_old * acc_old + w_new * pv

    qi1 = lax.broadcasted_iota(jnp.int32, (blk, blk), 0)
    kj1 = lax.broadcasted_iota(jnp.int32, (blk, blk), 1)
    causal = kj1 <= qi1

    def dil16(i, carry):
        units = [(2 * i + u, s) for u in range(2) for s in range(SWA_SLABS)]
        loaded = [(qd_ref[c, s], kd_ref[c, s], vd_ref[c, s]) for c, s in units]
        results = attend_all([(qs, ks, vs, causal) for qs, ks, vs in loaded])
        for (c, s), (m_new, l_new, pv) in zip(units, results):
            m_ref[c, s] = m_new
            l_ref[c, s] = l_new
            acc_ref[c, s] = pv
        return carry

    lax.fori_loop(0, res // 2, dil16, 0)

    def dilated(groups):
        n_blocks = nj * groups // blk
        n_res = res // groups
        q_per = blk // groups
        k_per = 2 * blk // groups
        two_blocks, first_block = permuted_masks(groups)
        nb_bits = n_blocks.bit_length() - 1
        assert n_blocks == 1 << nb_bits

        def step(i, carry):
            loaded = []
            for u in range(2):
                it = 2 * i + u
                r = lax.shift_right_logical(it, nb_bits)
                bi = it & (n_blocks - 1)
                q_rows = pl.ds(pl.multiple_of(bi * q_per, q_per), q_per)
                k_rows = pl.ds(pl.multiple_of(jnp.maximum(bi - 1, 0) * q_per, q_per), k_per)
                mask = (two_blocks & (bi > 0)) | (first_block & (bi == 0))
                gather = lambda ref, s, rows, r=r: jnp.concatenate(
                    [ref[a * n_res + r, s, rows, :] for a in range(groups)], axis=0)
                for s in range(SWA_SLABS):
                    old = (gather(m_ref, s, q_rows), gather(l_ref, s, q_rows), gather(acc_ref, s, q_rows))
                    loaded.append((r, s, q_rows, mask, gather(qd_ref, s, q_rows), gather(kd_ref, s, k_rows),
                                   gather(vd_ref, s, k_rows), old))
            news = attend_all([(qs, kcat, vcat, mask) for _, _, _, mask, qs, kcat, vcat, _ in loaded])
            results = [merge(unit[-1], new) for unit, new in zip(loaded, news)]
            for (r, s, q_rows, *_), (m_new, l_new, acc_new) in zip(loaded, results):
                for a in range(groups):
                    piece = slice(a * q_per, (a + 1) * q_per)
                    m_ref[a * n_res + r, s, q_rows, :] = m_new[piece]
                    l_ref[a * n_res + r, s, q_rows, :] = l_new[piece]
                    acc_ref[a * n_res + r, s, q_rows, :] = acc_new[piece]
            return carry

        lax.fori_loop(0, n_res * n_blocks // 2, step, 0)

    dilated(4)
    dilated(16)

    def finish(c, carry):
        outs = [acc_ref[c, s] / l_ref[c, s] for s in range(SWA_SLABS)]
        ss = sum(jnp.sum(o * o, axis=-1, keepdims=True) for o in outs)
        scale = lax.rsqrt(ss * (1.0 / SWA_WIDTH) + EPS)
        rows = pl.ds(c, nj, stride=res)
        for s in range(SWA_SLABS):
            o_ref[s, rows, :] = outs[s] * scale * gn_ref[:, s * LANES:(s + 1) * LANES]
        return carry

    lax.fori_loop(0, res, finish, 0)


def _swa(cq, ck, cv, gn, seq):
    n = cq.shape[2] * SWA_RES
    nj = seq // SWA_RES
    grouped = pl.BlockSpec((SWA_RES, SWA_SLABS, nj, LANES), lambda b: (0, 0, b, 0))
    state = pltpu.VMEM((SWA_RES, SWA_SLABS, nj, LANES), F32)
    return pl.pallas_call(
        functools.partial(_swa_kernel, seq=seq),
        grid=(n // seq,),
        in_specs=[grouped, grouped, grouped, _resident((1, SWA_WIDTH))],
        out_specs=pl.BlockSpec((SWA_SLABS, seq, LANES), lambda b: (0, b, 0)),
        out_shape=jax.ShapeDtypeStruct((SWA_SLABS, n, LANES), F32),
        scratch_shapes=[state] * 3,
        compiler_params=_params(("parallel",), 48),
        name="swa",
    )(cq, ck, cv, gn)


def _ffn_kernel(x_ref, oa_ref, ob_ref, oc_ref, woa_ref, wob_ref, woc_ref, gn_ref, wup_ref, cw_ref, cb_ref,
                wdn_ref, gfin_ref, out_ref, carry_ref, act_ref, *, tiles_per_seq, final):
    tm = x_ref.shape[0]

    @pl.when(pl.program_id(0) % tiles_per_seq == 0)
    def _():
        carry_ref[...] = jnp.zeros_like(carry_ref)

    x1 = x_ref[...] + _dot(oa_ref[...], woa_ref[...]) + _dot(ob_ref[...], wob_ref[...])
    for s in range(SWA_SLABS):
        x1 = x1 + _dot(oc_ref[s].astype(BF16), woc_ref[s * LANES:(s + 1) * LANES, :])
    h = (x1 * _rms_scale(x1) * gn_ref[...]).astype(BF16)

    def conv_up(c0):
        cols = slice(c0, c0 + FFN_CHUNK)
        up = _dot(h, wup_ref[:, cols])
        ext = jnp.concatenate([carry_ref[:, cols], up], axis=0)
        carry_ref[:, cols] = up[tm - 8:tm, :]
        cw = cw_ref[:, cols]
        return (cb_ref[:, cols] + up * cw[2:3, :] + pltpu.roll(ext, 2, 0)[8:, :] * cw[0:1, :]
                + pltpu.roll(ext, 1, 0)[8:, :] * cw[1:2, :])

    for j in range(FFN_DIM // FFN_CHUNK):
        val = conv_up(j * FFN_CHUNK)
        gate = conv_up(FFN_DIM + j * FFN_CHUNK)
        act_ref[:, j * FFN_CHUNK:(j + 1) * FFN_CHUNK] = (_gelu_tanh(gate) * val).astype(BF16)
    x2 = x1 + _dot(act_ref[...], wdn_ref[...])
    if final:
        x2 = x2 * _rms_scale(x2) * gfin_ref[...]
    out_ref[...] = x2


def _ffn(x, oa, ob, oc, woa, wob, woc, gn, wup, cw, cb, wdn, gfin, seq, final):
    n = x.shape[0]
    tm = TOKEN_TILE
    row = lambda width: pl.BlockSpec((tm, width), lambda i: (i, 0))
    return pl.pallas_call(
        functools.partial(_ffn_kernel, tiles_per_seq=seq // tm, final=final),
        grid=(n // tm,),
        in_specs=[row(D_MODEL), row(GLA_V_PAD), row(LRU_WIDTH),
                  pl.BlockSpec((SWA_SLABS, tm, LANES), lambda i: (0, i, 0)),
                  _resident((GLA_V_PAD, D_MODEL)), _resident((LRU_WIDTH, D_MODEL)), _resident((SWA_WIDTH, D_MODEL)),
                  _resident((1, D_MODEL)), _resident((D_MODEL, 2 * FFN_DIM)), _resident((FFN_CONV, 2 * FFN_DIM)),
                  _resident((1, 2 * FFN_DIM)), _resident((FFN_DIM, D_MODEL)), _resident((1, D_MODEL))],
        out_specs=row(D_MODEL),
        out_shape=jax.ShapeDtypeStruct((n, D_MODEL), F32),
        scratch_shapes=[pltpu.VMEM((8, 2 * FFN_DIM), F32), pltpu.VMEM((tm, FFN_DIM), BF16)],
        compiler_params=_params(("arbitrary",), 56),
        name="ffn",
    )(x, oa, ob, oc, woa, wob, woc, gn, wup, cw, cb, wdn, gfin)


def _rope_tables(seq):
    half = ROPE_DIM // 2
    inv = ROPE_THETA ** (-jnp.arange(0, ROPE_DIM, 2, dtype=F32) / ROPE_DIM)
    ang = jnp.arange(seq, dtype=F32)[:, None] * inv[None, :]
    cos, sin = jnp.cos(ang), jnp.sin(ang)
    ones = jnp.ones((seq, HEAD_DIM - ROPE_DIM), F32)
    zeros = jnp.zeros((seq, HEAD_DIM - ROPE_DIM), F32)
    zh = jnp.zeros((seq, half), F32)
    per_head = lambda parts: jnp.tile(jnp.concatenate(parts, axis=1), (1, LANES // HEAD_DIM))
    return per_head([cos, cos, ones]), per_head([-sin, zh, zeros]), per_head([zh, sin, zeros])


def _pad_heads(w, axis):
    shape = list(w.shape)
    shape[axis:axis + 1] = [GLA_HEADS, GLA_DV]
    w = w.reshape(shape)
    pad = [(0, 0)] * w.ndim
    pad[axis + 1] = (0, LANES - GLA_DV)
    shape[axis:axis + 2] = [GLA_V_PAD]
    return jnp.pad(w, pad).reshape(shape)


def _pack_w_in(w):
    z = lambda width: jnp.zeros((w.shape[0], width), w.dtype)
    o_v = 2 * GLA_QK
    o_r = o_v + GLA_WIDTH
    o_glr = o_r + GLA_WIDTH
    return jnp.concatenate([
        w[:, 0:GLA_QK], z(GLA_QK_PAD - GLA_QK), w[:, GLA_QK:o_v], z(GLA_QK_PAD - GLA_QK),
        _pad_heads(w[:, o_v:o_r], 1), _pad_heads(w[:, o_r:o_glr], 1),
        w[:, o_glr + GLA_GATE_RANK:], w[:, o_glr:o_glr + GLA_GATE_RANK], z(GLR_PAD - GLA_GATE_RANK)],
        axis=1).astype(BF16)


def _block_diag(w):
    nb, bs, _ = w.shape
    eye = jnp.eye(nb, dtype=w.dtype)
    return jnp.einsum('nij,nm->nimj', w, eye).reshape(nb * bs, nb * bs)


def kernel(x, norm_mix, w_in, gla_w_gate, gla_b_gate, gla_norm, lru_conv_w, lru_conv_b, lru_w_a, lru_b_a,
           lru_w_x, lru_b_x, lru_lambda, lru_norm, swa_norm, w_out, norm_ffn, ffn_w_up, ffn_conv_w, ffn_conv_b,
           ffn_w_down, norm_final):
    batch, seq, _ = x.shape
    depth = w_in.shape[0]
    assert seq % TOKEN_TILE == 0 and seq == SWA_BLOCK * SWA_RES
    cos, sa, sb = _rope_tables(seq)
    xf = x.reshape(batch * seq, D_MODEL)
    row = lambda v: v.reshape(1, -1)
    for l in range(depth):
        wg = jnp.zeros((GLR_PAD, GLA_QK_PAD), F32).at[:GLA_GATE_RANK, :GLA_QK].set(gla_w_gate[l]).astype(BF16)
        bg = jnp.zeros((1, GLA_QK_PAD), F32).at[0, :GLA_QK].set(gla_b_gate[l])
        lru = (lru_conv_w[l], row(lru_conv_b[l]), _block_diag(lru_w_a[l]).astype(BF16), row(lru_b_a[l]),
               _block_diag(lru_w_x[l]).astype(BF16), row(lru_b_x[l]), row(lru_lambda[l]), row(lru_norm[l]))
        qe, ke, qi, kd, dec, av, ar, o_b, cq, ck, cv = _inproj(
            xf, row(norm_mix[l]), _pack_w_in(w_in[l]), wg, bg, cos, sa, sb, lru, seq)
        o_a = _gla(qe, ke, qi, kd, dec, av, ar, row(_pad_heads(jnp.tile(gla_norm[l], GLA_HEADS), 0)), seq)
        o_c = _swa(cq, ck, cv, row(swa_norm[l]), seq)
        wo = w_out[l].astype(BF16)
        xf = _ffn(xf, o_a, o_b, o_c, _pad_heads(wo[:GLA_WIDTH], 0), wo[GLA_WIDTH:GLA_WIDTH + LRU_WIDTH],
                  wo[GLA_WIDTH + LRU_WIDTH:], row(norm_ffn[l]), ffn_w_up[l].astype(BF16), ffn_conv_w[l],
                  row(ffn_conv_b[l]), ffn_w_down[l].astype(BF16), row(norm_final), seq, l == depth - 1)
    return xf.reshape(batch, seq, D_MODEL)
```

```python
import functools

import jax
import jax.numpy as jnp
import numpy as np
from jax import lax
from jax.experimental import pallas as pl
from jax.experimental.pallas import tpu as pltpu

F32 = jnp.float32
BF16 = jnp.bfloat16

D_MODEL = 1024
EPS = 1e-6
HEAD_DIM = 64
LANES = 128
GLA_HEADS = 4
GLA_DK = 48
GLA_DV = 96
GLA_QK = GLA_HEADS * GLA_DK
GLA_QK_PAD = 256
GLA_WIDTH = GLA_HEADS * GLA_DV
GLA_V_PAD = GLA_HEADS * LANES
GLA_GATE_RANK = 16
GLA_GATE_TEMP = 16.0
GLA_CHUNK = 64
LRU_WIDTH = 256
LRU_BLOCKS = 8
LRU_C = 8.0
LRU_CONV = 4
SWA_HEADS = 6
SWA_WIDTH = SWA_HEADS * HEAD_DIM
SWA_BLOCK = 128
SWA_SLABS = SWA_WIDTH // LANES
SWA_RES = 16
ROPE_THETA = 500000.0
ROPE_DIM = HEAD_DIM // 4
FFN_DIM = 2816
FFN_CONV = 3
FFN_CHUNK = 256
GLR_PAD = 128

_C_AQ = 0
_C_AK = _C_AQ + GLA_QK_PAD
_C_AV = _C_AK + GLA_QK_PAD
_C_AR = _C_AV + GLA_V_PAD
_C_BG = _C_AR + GLA_V_PAD
_C_BI = _C_BG + LRU_WIDTH
_C_CQ = _C_BI + LRU_WIDTH
_C_CK = _C_CQ + SWA_WIDTH
_C_CV = _C_CK + SWA_WIDTH
_C_GLR = _C_CV + SWA_WIDTH
IN_COLS_PACKED = _C_GLR + GLR_PAD

TOKEN_TILE = 512
V7X_VMEM_BYTES = 64 * 1024 * 1024
NEG_BIG = -1e30
LOG2E = float(np.log2(np.e))


def _params(semantics, vmem_mib):
    assert vmem_mib * 1024 * 1024 < V7X_VMEM_BYTES
    return pltpu.CompilerParams(dimension_semantics=semantics,
                                vmem_limit_bytes=vmem_mib * 1024 * 1024)


def _resident(shape):
    nd = len(shape)
    return pl.BlockSpec(shape, lambda *_: (0,) * nd, pipeline_mode=pl.Buffered(1))


def _rms_scale(x):
    return lax.rsqrt(jnp.mean(x * x, axis=-1, keepdims=True) + EPS)


def _gelu_tanh(x):
    c = float(np.sqrt(2.0 / np.pi))
    return x * (0.5 + 0.5 * jnp.tanh(x * (c + (0.044715 * c) * (x * x))))


def _dot(a, b):
    return jnp.dot(a, b, preferred_element_type=F32)


def _dot_nt(a, b):
    return lax.dot_general(a, b, (((1,), (1,)), ((), ())), preferred_element_type=F32)


def _dot_tn(a, b):
    return lax.dot_general(a, b, (((0,), (0,)), ((), ())), preferred_element_type=F32)


def _inproj_kernel(x_ref, gn_ref, w_ref, wg_ref, bg_ref, cos_ref, sa_ref, sb_ref,
                   cw_ref, cb_ref, wa_ref, ba_ref, wx_ref, bx_ref, lam_ref, gnb_ref,
                   qe_ref, ke_ref, qi_ref, kd_ref, dec_ref, av_ref, ar_ref, ob_ref, cq_ref, ck_ref, cv_ref,
                   xtail_ref, hc_ref, rg_ref, *, tiles_per_seq):
    tm = x_ref.shape[0]
    c = GLA_CHUNK

    @pl.when(pl.program_id(0) % tiles_per_seq == 0)
    def _():
        xtail_ref[...] = jnp.zeros_like(xtail_ref)
        hc_ref[...] = jnp.zeros_like(hc_ref)

    x = x_ref[...]
    h = (x * _rms_scale(x) * gn_ref[...]).astype(BF16)

    def proj(c0, width):
        return _dot(h, w_ref[:, c0:c0 + width])

    tile = 256
    p_glr = proj(_C_GLR, GLR_PAD)
    p_bi = proj(_C_BI, LRU_WIDTH)
    z = _dot(p_glr.astype(BF16), wg_ref[...]) + bg_ref[...]
    p_aq = proj(_C_AQ, GLA_QK_PAD)

    ext = jnp.concatenate([xtail_ref[...], p_bi], axis=0)
    xtail_ref[...] = p_bi[tm - 8:tm, :]
    cw = cw_ref[...]
    u = cb_ref[...] + p_bi * cw[LRU_CONV - 1:LRU_CONV, :]
    for kk in range(LRU_CONV - 1):
        u = u + pltpu.roll(ext, LRU_CONV - 1 - kk, 0)[8:, :] * cw[kk:kk + 1, :]
    ub = u.astype(BF16)
    rg_pre = _dot(ub, wa_ref[...])
    ig_pre = _dot(ub, wx_ref[...])
    p_ak = proj(_C_AK, GLA_QK_PAD)

    g = (jnp.minimum(z, 0.0) - jnp.log(1.0 + jnp.exp(-jnp.abs(z)))) * (1.0 / GLA_GATE_TEMP)
    row_in_chunk = lax.broadcasted_iota(jnp.int32, (tm, 1), 0) & (c - 1)
    b = g
    shift = 1
    while shift < c:
        b = b + jnp.where(row_in_chunk >= shift, pltpu.roll(b, shift, 0), 0.0)
        if shift == 4:
            p_bg = proj(_C_BG, LRU_WIDTH)
        shift *= 2

    rg = jax.nn.sigmoid(rg_pre + ba_ref[...])
    ig = jax.nn.sigmoid(ig_pre + bx_ref[...])
    neg_lam = -lam_ref[...]
    softplus = jnp.maximum(neg_lam, 0.0) + jnp.log(1.0 + jnp.exp(-jnp.abs(neg_lam)))
    log_a = (-LRU_C) * rg * softplus
    a = jnp.exp(log_a)
    th = jnp.tanh(log_a)
    hs = jnp.sqrt(-2.0 * th / (1.0 - th)) * (ig * u)
    av_ref[:, :tile] = proj(_C_AV, tile).astype(BF16)

    row = lax.broadcasted_iota(jnp.int32, (tm, 1), 0)
    silu = lambda t: (t * jax.nn.sigmoid(t)).astype(BF16)
    shift = 1
    while shift < tm:
        if shift < 8:
            keep = row >= shift
            h_prev = jnp.where(keep, pltpu.roll(hs, shift, 0), 0.0)
            a_prev = jnp.where(keep, pltpu.roll(a, shift, 0), 1.0)
        else:
            h_prev = jnp.concatenate([jnp.zeros((shift, LRU_WIDTH), F32), hs[:tm - shift]], axis=0)
            a_prev = jnp.concatenate([jnp.ones((shift, LRU_WIDTH), F32), a[:tm - shift]], axis=0)
        hs = hs + a * h_prev
        a = a * a_prev
        if shift == 2:
            av_ref[:, tile:] = proj(_C_AV + tile, tile).astype(BF16)
        elif shift == 8:
            ar_ref[:, :tile] = silu(proj(_C_AR, tile))
        elif shift == 64:
            ar_ref[:, tile:] = silu(proj(_C_AR + tile, tile))
        shift *= 2
    hs = hs + a * hc_ref[...]
    hc_ref[...] = hs[tm - 1:tm, :]
    p_c0 = proj(_C_CQ, tile)

    b3 = b.reshape(tm // c, c, GLA_QK_PAD)
    b_mid = b3[:, c // 2 - 1:c // 2, :]
    b_last = b3[:, c - 1:c, :]
    q3 = (p_aq * (GLA_DK ** -0.5)).reshape(tm // c, c, GLA_QK_PAD)
    k3 = p_ak.reshape(tm // c, c, GLA_QK_PAD)
    flat = lambda t: t.reshape(tm, GLA_QK_PAD).astype(BF16)
    qe_ref[...] = flat(q3 * jnp.exp(b3 - b_mid))
    ke_ref[...] = flat(k3 * jnp.exp(b_mid - b3))
    p_c1 = proj(_C_CQ + tile, tile)
    qi_ref[...] = flat(q3 * jnp.exp(b3))
    kd_ref[...] = flat(k3 * jnp.exp(b_last - b3))
    dec_ref[...] = jnp.exp(b_last)
    p_c2 = proj(_C_CQ + 2 * tile, tile)

    y = hs * _gelu_tanh(p_bg)
    ob_ref[...] = (y * _rms_scale(y) * gnb_ref[...]).astype(ob_ref.dtype)
    p_c3 = proj(_C_CQ + 3 * tile, tile)

    cos = cos_ref[...]
    sa = sa_ref[...]
    sb = sb_ref[...]

    def rope(t):
        return t * cos + pltpu.roll(t, LANES - ROPE_DIM // 2, 1) * sa + pltpu.roll(t, ROPE_DIM // 2, 1) * sb

    def regroup(out_ref, s, slot, val):
        rg_ref[slot] = val
        for cc in range(SWA_RES):
            out_ref[cc, s] = rg_ref[slot, pl.ds(cc, tm // SWA_RES, stride=SWA_RES), :]

    q_scale = HEAD_DIM ** -0.5 * LOG2E
    regroup(cq_ref, 0, 0, rope(p_c0[:, :LANES] * q_scale))
    regroup(cq_ref, 1, 1, rope(p_c0[:, LANES:] * q_scale))
    p_c4 = proj(_C_CQ + 4 * tile, LANES)
    regroup(cq_ref, 2, 2, rope(p_c1[:, :LANES] * q_scale))
    regroup(ck_ref, 0, 3, rope(p_c1[:, LANES:]))
    regroup(ck_ref, 1, 4, rope(p_c2[:, :LANES]))
    regroup(ck_ref, 2, 5, rope(p_c2[:, LANES:]))
    regroup(cv_ref, 0, 6, p_c3[:, :LANES])
    regroup(cv_ref, 1, 7, p_c3[:, LANES:])
    regroup(cv_ref, 2, 8, p_c4)


def _inproj(x, gn, w, wg, bg, cos, sa, sb, lru, seq):
    n = x.shape[0]
    tm = TOKEN_TILE
    tiles_per_seq = seq // tm
    row = lambda width: pl.BlockSpec((tm, width), lambda i: (i, 0))
    grouped = pl.BlockSpec((SWA_RES, SWA_SLABS, tm // SWA_RES, LANES), lambda i: (0, 0, i, 0))
    table = pl.BlockSpec((tm, LANES), lambda i: (i % tiles_per_seq, 0))
    vec = _resident((1, LRU_WIDTH))
    mat = _resident((LRU_WIDTH, LRU_WIDTH))
    sds = jax.ShapeDtypeStruct
    grouped_shape = sds((SWA_RES, SWA_SLABS, n // SWA_RES, LANES), F32)
    out_shapes = (
        sds((n, GLA_QK_PAD), BF16), sds((n, GLA_QK_PAD), BF16), sds((n, GLA_QK_PAD), BF16),
        sds((n, GLA_QK_PAD), BF16), sds((n // GLA_CHUNK, 1, GLA_QK_PAD), F32),
        sds((n, GLA_V_PAD), BF16), sds((n, GLA_V_PAD), BF16), sds((n, LRU_WIDTH), BF16),
        grouped_shape, grouped_shape, grouped_shape,
    )
    return pl.pallas_call(
        functools.partial(_inproj_kernel, tiles_per_seq=tiles_per_seq),
        grid=(n // tm,),
        in_specs=[row(D_MODEL), _resident((1, D_MODEL)), _resident((D_MODEL, IN_COLS_PACKED)),
                  _resident((GLR_PAD, GLA_QK_PAD)), _resident((1, GLA_QK_PAD)), table, table, table,
                  _resident((LRU_CONV, LRU_WIDTH)), vec, mat, vec, mat, vec, vec, vec],
        out_specs=(row(GLA_QK_PAD), row(GLA_QK_PAD), row(GLA_QK_PAD), row(GLA_QK_PAD),
                   pl.BlockSpec((tm // GLA_CHUNK, 1, GLA_QK_PAD), lambda i: (i, 0, 0)),
                   row(GLA_V_PAD), row(GLA_V_PAD), row(LRU_WIDTH), grouped, grouped, grouped),
        out_shape=out_shapes,
        scratch_shapes=[pltpu.VMEM((8, LRU_WIDTH), F32), pltpu.VMEM((1, LRU_WIDTH), F32),
                        pltpu.VMEM((3 * SWA_SLABS, tm, LANES), F32)],
        compiler_params=_params(("arbitrary",), 48),
        name="inproj",
    )(x, gn, w, wg, bg, cos, sa, sb, *lru)


def _gla_kernel(qe_ref, ke_ref, qi_ref, kd_ref, dec_ref, v_ref, r_ref, gn_ref, o_ref, st_ref, *, seq):
    c = GLA_CHUNK
    st_ref[...] = jnp.zeros_like(st_ref)

    lane_q = lax.broadcasted_iota(jnp.int32, (1, GLA_QK_PAD), 1)
    q_masks = [((lane_q >= h * GLA_DK) & (lane_q < (h + 1) * GLA_DK)).astype(BF16) for h in range(GLA_HEADS)]
    ri = lax.broadcasted_iota(jnp.int32, (GLA_HEADS * c, c), 0) & (c - 1)
    ci = lax.broadcasted_iota(jnp.int32, (GLA_HEADS * c, c), 1)
    causal = ci <= ri
    gn = gn_ref[...]

    per_step = 8
    heads = [slice(h * LANES, (h + 1) * LANES) for h in range(GLA_HEADS)]

    def step(i, carry):
        ids = [per_step * i + u for u in range(per_step)]
        rows = [pl.ds(pl.multiple_of(ci * c, c), c) for ci in ids]
        vbs = [v_ref[r, :] for r in rows]
        scores = [_dot_nt(jnp.concatenate([qe_ref[r, :] * m for m in q_masks], axis=0), ke_ref[r, :])
                  for r in rows]
        incs = []
        for r, vb in zip(rows, vbs):
            kd = kd_ref[r, :]
            incs.append([_dot_tn(vb[:, sl], kd * q_masks[h]) for h, sl in enumerate(heads)])
        st = [st_ref[sl, :] for sl in heads]
        for ci, r, vb, s, inc in zip(ids, rows, vbs, scores, incs):
            o_inter = _dot_nt(qi_ref[r, :], jnp.concatenate(st, axis=0).astype(BF16))
            s = jnp.where(causal, s, 0.0).astype(BF16)
            dec = dec_ref[ci]
            gate = r_ref[r, :]
            for h, sl in enumerate(heads):
                o = o_inter[:, sl] + _dot(s[h * c:(h + 1) * c, :], vb[:, sl])
                st[h] = st[h] * dec + inc[h]
                ms = jnp.sum(o * o, axis=-1, keepdims=True) * (1.0 / GLA_DV)
                y = o * lax.rsqrt(ms + EPS) * gn[:, sl] * gate[:, sl].astype(F32)
                o_ref[r, sl] = y.astype(o_ref.dtype)
        for h, sl in enumerate(heads):
            st_ref[sl, :] = st[h]
        return carry

    lax.fori_loop(0, seq // (c * per_step), step, 0)


def _gla(qe, ke, qi, kd, dec, av, ar, gn, seq):
    n = qe.shape[0]
    blk = lambda width: pl.BlockSpec((seq, width), lambda b: (b, 0))
    return pl.pallas_call(
        functools.partial(_gla_kernel, seq=seq),
        grid=(n // seq,),
        in_specs=[blk(GLA_QK_PAD), blk(GLA_QK_PAD), blk(GLA_QK_PAD), blk(GLA_QK_PAD),
                  pl.BlockSpec((seq // GLA_CHUNK, 1, GLA_QK_PAD), lambda b: (b, 0, 0)),
                  blk(GLA_V_PAD), blk(GLA_V_PAD), _resident((1, GLA_V_PAD))],
        out_specs=blk(GLA_V_PAD),
        out_shape=jax.ShapeDtypeStruct((n, GLA_V_PAD), BF16),
        scratch_shapes=[pltpu.VMEM((GLA_V_PAD, GLA_QK_PAD), F32)],
        compiler_params=_params(("parallel",), 40),
        name="gla",
    )(qe, ke, qi, kd, dec, av, ar, gn)


def _swa_kernel(qd_ref, kd_ref, vd_ref, gn_ref, o_ref, acc_ref, m_ref, l_ref, *, seq):
    blk = SWA_BLOCK
    res = SWA_RES
    nj = seq // res
    assert nj == blk
    lane = lax.broadcasted_iota(jnp.int32, (1, LANES), 1)
    head0 = lane < HEAD_DIM

    def permuted_masks(groups):
        q_per = blk // groups
        k_per = 2 * blk // groups
        qi = lax.broadcasted_iota(jnp.int32, (blk, 2 * blk), 0)
        ki = lax.broadcasted_iota(jnp.int32, (blk, 2 * blk), 1)
        tq = (qi % q_per) * groups + qi // q_per
        tk = (ki % k_per) * groups + ki // k_per
        two_blocks = (tk >= tq) & (tk <= tq + blk)
        first_block = tk <= tq
        return two_blocks, first_block

    def attend_all(units):
        both = lambda t: jnp.where(head0, t[:blk], t[blk:])
        scores = []
        for qs, kcat, _, _ in units:
            q2 = jnp.concatenate([jnp.where(head0, qs, 0.0), jnp.where(head0, 0.0, qs)], axis=0)
            scores.append(_dot_nt(q2.astype(BF16), kcat.astype(BF16)))
        soft = []
        for s, (_, _, _, mask) in zip(scores, units):
            s = jnp.where(jnp.concatenate([mask, mask], axis=0), s, NEG_BIG)
            m = jnp.max(s, axis=-1, keepdims=True)
            soft.append((m, jnp.exp2(s - m).astype(BF16)))
        outs = []
        for (m, p), (_, _, vcat, _) in zip(soft, units):
            v_ones = jnp.concatenate([vcat.astype(BF16), jnp.ones(vcat.shape, BF16)], axis=1)
            pv = _dot(p, v_ones)
            outs.append((both(m), both(pv[:, LANES:]), both(pv[:, :LANES])))
        return outs

    def merge(old, new):
        (m_old, l_old, acc_old), (m_new, l_new, pv) = old, new
        m_tot = jnp.maximum(m_old, m_new)
        w_old = jnp.exp2(m_old - m_tot)
        w_new = jnp.exp2(m_new - m_tot)
        return m_tot, w_old * l_old + w_new * l_new, w_old * acc_old + w_new * pv

    qi1 = lax.broadcasted_iota(jnp.int32, (blk, blk), 0)
    kj1 = lax.broadcasted_iota(jnp.int32, (blk, blk), 1)
    causal = kj1 <= qi1

    def dil16(i, carry):
        units = [(2 * i + u, s) for u in range(2) for s in range(SWA_SLABS)]
        loaded = [(qd_ref[c, s], kd_ref[c, s], vd_ref[c, s]) for c, s in units]
        results = attend_all([(qs, ks, vs, causal) for qs, ks, vs in loaded])
        for (c, s), (m_new, l_new, pv) in zip(units, results):
            m_ref[c, s] = m_new
            l_ref[c, s] = l_new
            acc_ref[c, s] = pv
        return carry

    lax.fori_loop(0, res // 2, dil16, 0)

    def dilated(groups):
        n_blocks = nj * groups // blk
        n_res = res // groups
        q_per = blk // groups
        k_per = 2 * blk // groups
        two_blocks, first_block = permuted_masks(groups)
        nb_bits = n_blocks.bit_length() - 1
        assert n_blocks == 1 << nb_bits

        def step(i, carry):
            loaded = []
            for u in range(2):
                it = 2 * i + u
                r = lax.shift_right_logical(it, nb_bits)
                bi = it & (n_blocks - 1)
                q_rows = pl.ds(pl.multiple_of(bi * q_per, q_per), q_per)
                k_rows = pl.ds(pl.multiple_of(jnp.maximum(bi - 1, 0) * q_per, q_per), k_per)
                mask = (two_blocks & (bi > 0)) | (first_block & (bi == 0))
                gather = lambda ref, s, rows, r=r: jnp.concatenate(
                    [ref[a * n_res + r, s, rows, :] for a in range(groups)], axis=0)
                for s in range(SWA_SLABS):
                    old = (gather(m_ref, s, q_rows), gather(l_ref, s, q_rows), gather(acc_ref, s, q_rows))
                    loaded.append((r, s, q_rows, mask, gather(qd_ref, s, q_rows), gather(kd_ref, s, k_rows),
                                   gather(vd_ref, s, k_rows), old))
            news = attend_all([(qs, kcat, vcat, mask) for _, _, _, mask, qs, kcat, vcat, _ in loaded])
            results = [merge(unit[-1], new) for unit, new in zip(loaded, news)]
            for (r, s, q_rows, *_), (m_new, l_new, acc_new) in zip(loaded, results):
                for a in range(groups):
                    piece = slice(a * q_per, (a + 1) * q_per)
                    m_ref[a * n_res + r, s, q_rows, :] = m_new[piece]
                    l_ref[a * n_res + r, s, q_rows, :] = l_new[piece]
                    acc_ref[a * n_res + r, s, q_rows, :] = acc_new[piece]
            return carry

        lax.fori_loop(0, n_res * n_blocks // 2, step, 0)

    dilated(4)
    dilated(16)

    def finish(c, carry):
        outs = [acc_ref[c, s] / l_ref[c, s] for s in range(SWA_SLABS)]
        ss = sum(jnp.sum(o * o, axis=-1, keepdims=True) for o in outs)
        scale = lax.rsqrt(ss * (1.0 / SWA_WIDTH) + EPS)
        rows = pl.ds(c, nj, stride=res)
        for s in range(SWA_SLABS):
            o_ref[s, rows, :] = outs[s] * scale * gn_ref[:, s * LANES:(s + 1) * LANES]
        return carry

    lax.fori_loop(0, res, finish, 0)


def _swa(cq, ck, cv, gn, seq):
    n = cq.shape[2] * SWA_RES
    nj = seq // SWA_RES
    grouped = pl.BlockSpec((SWA_RES, SWA_SLABS, nj, LANES), lambda b: (0, 0, b, 0))
    state = pltpu.VMEM((SWA_RES, SWA_SLABS, nj, LANES), F32)
    return pl.pallas_call(
        functools.partial(_swa_kernel, seq=seq),
        grid=(n // seq,),
        in_specs=[grouped, grouped, grouped, _resident((1, SWA_WIDTH))],
        out_specs=pl.BlockSpec((SWA_SLABS, seq, LANES), lambda b: (0, b, 0)),
        out_shape=jax.ShapeDtypeStruct((SWA_SLABS, n, LANES), F32),
        scratch_shapes=[state] * 3,
        compiler_params=_params(("parallel",), 48),
        name="swa",
    )(cq, ck, cv, gn)


def _ffn_kernel(x_ref, oa_ref, ob_ref, oc_ref, woa_ref, wob_ref, woc_ref, gn_ref, wup_ref, cw_ref, cb_ref,
                wdn_ref, gfin_ref, out_ref, carry_ref, act_ref, *, tiles_per_seq, final):
    tm = x_ref.shape[0]

    @pl.when(pl.program_id(0) % tiles_per_seq == 0)
    def _():
        carry_ref[...] = jnp.zeros_like(carry_ref)

    x1 = x_ref[...] + _dot(oa_ref[...], woa_ref[...]) + _dot(ob_ref[...], wob_ref[...])
    for s in range(SWA_SLABS):
        x1 = x1 + _dot(oc_ref[s].astype(BF16), woc_ref[s * LANES:(s + 1) * LANES, :])
    h = (x1 * _rms_scale(x1) * gn_ref[...]).astype(BF16)

    def conv_up(c0):
        cols = slice(c0, c0 + FFN_CHUNK)
        up = _dot(h, wup_ref[:, cols])
        ext = jnp.concatenate([carry_ref[:, cols], up], axis=0)
        carry_ref[:, cols] = up[tm - 8:tm, :]
        cw = cw_ref[:, cols]
        return (cb_ref[:, cols] + up * cw[2:3, :] + pltpu.roll(ext, 2, 0)[8:, :] * cw[0:1, :]
                + pltpu.roll(ext, 1, 0)[8:, :] * cw[1:2, :])

    for j in range(FFN_DIM // FFN_CHUNK):
        val = conv_up(j * FFN_CHUNK)
        gate = conv_up(FFN_DIM + j * FFN_CHUNK)
        act_ref[:, j * FFN_CHUNK:(j + 1) * FFN_CHUNK] = (_gelu_tanh(gate) * val).astype(BF16)
    x2 = x1 + _dot(act_ref[...], wdn_ref[...])
    if final:
        x2 = x2 * _rms_scale(x2) * gfin_ref[...]
    out_ref[...] = x2


def _ffn(x, oa, ob, oc, woa, wob, woc, gn, wup, cw, cb, wdn, gfin, seq, final):
    n = x.shape[0]
    tm = TOKEN_TILE
    row = lambda width: pl.BlockSpec((tm, width), lambda i: (i, 0))
    return pl.pallas_call(
        functools.partial(_ffn_kernel, tiles_per_seq=seq // tm, final=final),
        grid=(n // tm,),
        in_specs=[row(D_MODEL), row(GLA_V_PAD), row(LRU_WIDTH),
                  pl.BlockSpec((SWA_SLABS, tm, LANES), lambda i: (0, i, 0)),
                  _resident((GLA_V_PAD, D_MODEL)), _resident((LRU_WIDTH, D_MODEL)), _resident((SWA_WIDTH, D_MODEL)),
                  _resident((1, D_MODEL)), _resident((D_MODEL, 2 * FFN_DIM)), _resident((FFN_CONV, 2 * FFN_DIM)),
                  _resident((1, 2 * FFN_DIM)), _resident((FFN_DIM, D_MODEL)), _resident((1, D_MODEL))],
        out_specs=row(D_MODEL),
        out_shape=jax.ShapeDtypeStruct((n, D_MODEL), F32),
        scratch_shapes=[pltpu.VMEM((8, 2 * FFN_DIM), F32), pltpu.VMEM((tm, FFN_DIM), BF16)],
        compiler_params=_params(("arbitrary",), 56),
        name="ffn",
    )(x, oa, ob, oc, woa, wob, woc, gn, wup, cw, cb, wdn, gfin)


def _rope_tables(seq):
    half = ROPE_DIM // 2
    inv = ROPE_THETA ** (-jnp.arange(0, ROPE_DIM, 2, dtype=F32) / ROPE_DIM)
    ang = jnp.arange(seq, dtype=F32)[:, None] * inv[None, :]
    cos, sin = jnp.cos(ang), jnp.sin(ang)
    ones = jnp.ones((seq, HEAD_DIM - ROPE_DIM), F32)
    zeros = jnp.zeros((seq, HEAD_DIM - ROPE_DIM), F32)
    zh = jnp.zeros((seq, half), F32)
    per_head = lambda parts: jnp.tile(jnp.concatenate(parts, axis=1), (1, LANES // HEAD_DIM))
    return per_head([cos, cos, ones]), per_head([-sin, zh, zeros]), per_head([zh, sin, zeros])


def _pad_heads(w, axis):
    shape = list(w.shape)
    shape[axis:axis + 1] = [GLA_HEADS, GLA_DV]
    w = w.reshape(shape)
    pad = [(0, 0)] * w.ndim
    pad[axis + 1] = (0, LANES - GLA_DV)
    shape[axis:axis + 2] = [GLA_V_PAD]
    return jnp.pad(w, pad).reshape(shape)


def _pack_w_in(w):
    z = lambda width: jnp.zeros(w.shape[:-1] + (width,), w.dtype)
    o_v = 2 * GLA_QK
    o_r = o_v + GLA_WIDTH
    o_glr = o_r + GLA_WIDTH
    return jnp.concatenate([
        w[..., 0:GLA_QK], z(GLA_QK_PAD - GLA_QK), w[..., GLA_QK:o_v], z(GLA_QK_PAD - GLA_QK),
        _pad_heads(w[..., o_v:o_r], 2), _pad_heads(w[..., o_r:o_glr], 2),
        w[..., o_glr + GLA_GATE_RANK:], w[..., o_glr:o_glr + GLA_GATE_RANK], z(GLR_PAD - GLA_GATE_RANK)],
        axis=-1).astype(BF16)


def _block_diag(w):
    layers, nb, bs, _ = w.shape
    eye = jnp.eye(nb, dtype=w.dtype)
    return jnp.einsum('lnij,nm->lnimj', w, eye).reshape(layers, nb * bs, nb * bs)


def kernel(x, norm_mix, w_in, gla_w_gate, gla_b_gate, gla_norm, lru_conv_w, lru_conv_b, lru_w_a, lru_b_a,
           lru_w_x, lru_b_x, lru_lambda, lru_norm, swa_norm, w_out, norm_ffn, ffn_w_up, ffn_conv_w, ffn_conv_b,
           ffn_w_down, norm_final):
    batch, seq, _ = x.shape
    depth = w_in.shape[0]
    assert seq % TOKEN_TILE == 0 and seq == SWA_BLOCK * SWA_RES
    cos, sa, sb = _rope_tables(seq)
    xf = x.reshape(batch * seq, D_MODEL)
    rows = lambda v: v.reshape(depth, 1, -1)
    wg = jnp.pad(gla_w_gate, ((0, 0), (0, GLR_PAD - GLA_GATE_RANK), (0, GLA_QK_PAD - GLA_QK))).astype(BF16)
    bg = rows(jnp.pad(gla_b_gate, ((0, 0), (0, GLA_QK_PAD - GLA_QK))))
    w_in_p = _pack_w_in(w_in)
    gla_gn = rows(_pad_heads(jnp.tile(gla_norm, (1, GLA_HEADS)), 1))
    wa = _block_diag(lru_w_a).astype(BF16)
    wx = _block_diag(lru_w_x).astype(BF16)
    wo = w_out.astype(BF16)
    wo_a = _pad_heads(wo[:, :GLA_WIDTH], 1)
    wo_b = wo[:, GLA_WIDTH:GLA_WIDTH + LRU_WIDTH]
    wo_c = wo[:, GLA_WIDTH + LRU_WIDTH:]
    w_up = ffn_w_up.astype(BF16)
    w_dn = ffn_w_down.astype(BF16)
    vecs = [rows(v) for v in (norm_mix, lru_conv_b, lru_b_a, lru_b_x, lru_lambda, lru_norm, swa_norm, norm_ffn,
                              ffn_conv_b)]
    g_mix, cb_lru, ba, bx, lam, g_lru, g_swa, g_ffn, cb_ffn = vecs
    g_fin = norm_final.reshape(1, -1)
    for l in range(depth):
        lru = (lru_conv_w[l], cb_lru[l], wa[l], ba[l], wx[l], bx[l], lam[l], g_lru[l])
        qe, ke, qi, kd, dec, av, ar, o_b, cq, ck, cv = _inproj(
            xf, g_mix[l], w_in_p[l], wg[l], bg[l], cos, sa, sb, lru, seq)
        o_a = _gla(qe, ke, qi, kd, dec, av, ar, gla_gn[l], seq)
        o_c = _swa(cq, ck, cv, g_swa[l], seq)
        xf = _ffn(xf, o_a, o_b, o_c, wo_a[l], wo_b[l], wo_c[l], g_ffn[l], w_up[l], ffn_conv_w[l], cb_ffn[l],
                  w_dn[l], g_fin, seq, l == depth - 1)
    return xf.reshape(batch, seq, D_MODEL)
```

```python
import functools

import jax
import jax.numpy as jnp
import numpy as np
from jax import lax
from jax.experimental import pallas as pl
from jax.experimental.pallas import tpu as pltpu

F32 = jnp.float32
BF16 = jnp.bfloat16

D_MODEL = 1024
EPS = 1e-6
HEAD_DIM = 64
LANES = 128
GLA_HEADS = 4
GLA_DK = 48
GLA_DV = 96
GLA_QK = GLA_HEADS * GLA_DK
GLA_QK_PAD = 256
GLA_WIDTH = GLA_HEADS * GLA_DV
GLA_V_PAD = GLA_HEADS * LANES
GLA_GATE_RANK = 16
GLA_GATE_TEMP = 16.0
GLA_CHUNK = 64
LRU_WIDTH = 256
LRU_BLOCKS = 8
LRU_C = 8.0
LRU_CONV = 4
SWA_HEADS = 6
SWA_WIDTH = SWA_HEADS * HEAD_DIM
SWA_BLOCK = 128
SWA_SLABS = SWA_WIDTH // LANES
SWA_RES = 16
ROPE_THETA = 500000.0
ROPE_DIM = HEAD_DIM // 4
FFN_DIM = 2816
FFN_CONV = 3
FFN_CHUNK = 256
GLR_PAD = 128

_C_AQ = 0
_C_AK = _C_AQ + GLA_QK_PAD
_C_AV = _C_AK + GLA_QK_PAD
_C_AR = _C_AV + GLA_V_PAD
_C_BG = _C_AR + GLA_V_PAD
_C_BI = _C_BG + LRU_WIDTH
_C_CQ = _C_BI + LRU_WIDTH
_C_CK = _C_CQ + SWA_WIDTH
_C_CV = _C_CK + SWA_WIDTH
_C_GLR = _C_CV + SWA_WIDTH
IN_COLS_PACKED = _C_GLR + GLR_PAD

TOKEN_TILE = 512
V7X_VMEM_BYTES = 64 * 1024 * 1024
NEG_BIG = -1e30
LOG2E = float(np.log2(np.e))


def _params(semantics, vmem_mib):
    assert vmem_mib * 1024 * 1024 < V7X_VMEM_BYTES
    return pltpu.CompilerParams(dimension_semantics=semantics,
                                vmem_limit_bytes=vmem_mib * 1024 * 1024)


def _resident(shape, layer=None):
    nd = len(shape)
    if layer is None:
        return pl.BlockSpec(shape, lambda *_: (0,) * nd, pipeline_mode=pl.Buffered(1))
    return pl.BlockSpec((None,) + tuple(shape), lambda *_: (layer,) + (0,) * nd, pipeline_mode=pl.Buffered(1))


def _rms_scale(x):
    return lax.rsqrt(jnp.mean(x * x, axis=-1, keepdims=True) + EPS)


def _gelu_tanh(x):
    c = float(np.sqrt(2.0 / np.pi))
    return x * (0.5 + 0.5 * jnp.tanh(x * (c + (0.044715 * c) * (x * x))))


def _dot(a, b):
    return jnp.dot(a, b, preferred_element_type=F32)


def _dot_nt(a, b):
    return lax.dot_general(a, b, (((1,), (1,)), ((), ())), preferred_element_type=F32)


def _dot_tn(a, b):
    return lax.dot_general(a, b, (((0,), (0,)), ((), ())), preferred_element_type=F32)


def _inproj_kernel(x_ref, gn_ref, w_ref, wg_ref, bg_ref, cos_ref, sa_ref, sb_ref,
                   cw_ref, cb_ref, wa_ref, ba_ref, wx_ref, bx_ref, lam_ref, gnb_ref,
                   qe_ref, ke_ref, qi_ref, kd_ref, dec_ref, av_ref, ar_ref, ob_ref, cq_ref, ck_ref, cv_ref,
                   xtail_ref, hc_ref, rg_ref, *, tiles_per_seq):
    tm = x_ref.shape[0]
    c = GLA_CHUNK

    @pl.when(pl.program_id(0) % tiles_per_seq == 0)
    def _():
        xtail_ref[...] = jnp.zeros_like(xtail_ref)
        hc_ref[...] = jnp.zeros_like(hc_ref)

    x = x_ref[...]
    h = (x * _rms_scale(x) * gn_ref[...]).astype(BF16)

    def proj(c0, width):
        return _dot(h, w_ref[:, c0:c0 + width])

    tile = 256
    p_glr = proj(_C_GLR, GLR_PAD)
    p_bi = proj(_C_BI, LRU_WIDTH)
    z = _dot(p_glr.astype(BF16), wg_ref[...]) + bg_ref[...]
    p_aq = proj(_C_AQ, GLA_QK_PAD)

    ext = jnp.concatenate([xtail_ref[...], p_bi], axis=0)
    xtail_ref[...] = p_bi[tm - 8:tm, :]
    cw = cw_ref[...]
    u = cb_ref[...] + p_bi * cw[LRU_CONV - 1:LRU_CONV, :]
    for kk in range(LRU_CONV - 1):
        u = u + pltpu.roll(ext, LRU_CONV - 1 - kk, 0)[8:, :] * cw[kk:kk + 1, :]
    ub = u.astype(BF16)
    rg_pre = _dot(ub, wa_ref[...])
    ig_pre = _dot(ub, wx_ref[...])
    p_ak = proj(_C_AK, GLA_QK_PAD)

    g = (jnp.minimum(z, 0.0) - jnp.log(1.0 + jnp.exp(-jnp.abs(z)))) * (1.0 / GLA_GATE_TEMP)
    row_in_chunk = lax.broadcasted_iota(jnp.int32, (tm, 1), 0) & (c - 1)
    b = g
    shift = 1
    while shift < c:
        b = b + jnp.where(row_in_chunk >= shift, pltpu.roll(b, shift, 0), 0.0)
        if shift == 4:
            p_bg = proj(_C_BG, LRU_WIDTH)
        shift *= 2

    rg = jax.nn.sigmoid(rg_pre + ba_ref[...])
    ig = jax.nn.sigmoid(ig_pre + bx_ref[...])
    neg_lam = -lam_ref[...]
    softplus = jnp.maximum(neg_lam, 0.0) + jnp.log(1.0 + jnp.exp(-jnp.abs(neg_lam)))
    log_a = (-LRU_C) * rg * softplus
    a = jnp.exp(log_a)
    th = jnp.tanh(log_a)
    hs = jnp.sqrt(-2.0 * th / (1.0 - th)) * (ig * u)
    av_ref[:, :tile] = proj(_C_AV, tile).astype(BF16)

    row = lax.broadcasted_iota(jnp.int32, (tm, 1), 0)
    silu = lambda t: (t * jax.nn.sigmoid(t)).astype(BF16)
    shift = 1
    while shift < tm:
        if shift < 8:
            keep = row >= shift
            h_prev = jnp.where(keep, pltpu.roll(hs, shift, 0), 0.0)
            a_prev = jnp.where(keep, pltpu.roll(a, shift, 0), 1.0)
        else:
            h_prev = jnp.concatenate([jnp.zeros((shift, LRU_WIDTH), F32), hs[:tm - shift]], axis=0)
            a_prev = jnp.concatenate([jnp.ones((shift, LRU_WIDTH), F32), a[:tm - shift]], axis=0)
        hs = hs + a * h_prev
        a = a * a_prev
        if shift == 2:
            av_ref[:, tile:] = proj(_C_AV + tile, tile).astype(BF16)
        elif shift == 8:
            ar_ref[:, :tile] = silu(proj(_C_AR, tile))
        elif shift == 64:
            ar_ref[:, tile:] = silu(proj(_C_AR + tile, tile))
        shift *= 2
    hs = hs + a * hc_ref[...]
    hc_ref[...] = hs[tm - 1:tm, :]
    p_c0 = proj(_C_CQ, tile)

    b3 = b.reshape(tm // c, c, GLA_QK_PAD)
    b_mid = b3[:, c // 2 - 1:c // 2, :]
    b_last = b3[:, c - 1:c, :]
    q3 = (p_aq * (GLA_DK ** -0.5)).reshape(tm // c, c, GLA_QK_PAD)
    k3 = p_ak.reshape(tm // c, c, GLA_QK_PAD)
    flat = lambda t: t.reshape(tm, GLA_QK_PAD).astype(BF16)
    qe_ref[...] = flat(q3 * jnp.exp(b3 - b_mid))
    ke_ref[...] = flat(k3 * jnp.exp(b_mid - b3))
    p_c1 = proj(_C_CQ + tile, tile)
    qi_ref[...] = flat(q3 * jnp.exp(b3))
    kd_ref[...] = flat(k3 * jnp.exp(b_last - b3))
    dec_ref[...] = jnp.exp(b_last)
    p_c2 = proj(_C_CQ + 2 * tile, tile)

    y = hs * _gelu_tanh(p_bg)
    ob_ref[...] = (y * _rms_scale(y) * gnb_ref[...]).astype(ob_ref.dtype)
    p_c3 = proj(_C_CQ + 3 * tile, tile)

    cos = cos_ref[...]
    sa = sa_ref[...]
    sb = sb_ref[...]

    def rope(t):
        return t * cos + pltpu.roll(t, LANES - ROPE_DIM // 2, 1) * sa + pltpu.roll(t, ROPE_DIM // 2, 1) * sb

    def regroup(out_ref, s, slot, val):
        rg_ref[slot] = val
        for cc in range(SWA_RES):
            out_ref[cc, s] = rg_ref[slot, pl.ds(cc, tm // SWA_RES, stride=SWA_RES), :]

    q_scale = HEAD_DIM ** -0.5 * LOG2E
    regroup(cq_ref, 0, 0, rope(p_c0[:, :LANES] * q_scale))
    regroup(cq_ref, 1, 1, rope(p_c0[:, LANES:] * q_scale))
    p_c4 = proj(_C_CQ + 4 * tile, LANES)
    regroup(cq_ref, 2, 2, rope(p_c1[:, :LANES] * q_scale))
    regroup(ck_ref, 0, 3, rope(p_c1[:, LANES:]))
    regroup(ck_ref, 1, 4, rope(p_c2[:, :LANES]))
    regroup(ck_ref, 2, 5, rope(p_c2[:, LANES:]))
    regroup(cv_ref, 0, 6, p_c3[:, :LANES])
    regroup(cv_ref, 1, 7, p_c3[:, LANES:])
    regroup(cv_ref, 2, 8, p_c4)


def _inproj(x, gn, w, wg, bg, cos, sa, sb, lru, seq, layer):
    n = x.shape[0]
    tm = TOKEN_TILE
    tiles_per_seq = seq // tm
    row = lambda width: pl.BlockSpec((tm, width), lambda i: (i, 0))
    grouped = pl.BlockSpec((SWA_RES, SWA_SLABS, tm // SWA_RES, LANES), lambda i: (0, 0, i, 0))
    table = pl.BlockSpec((tm, LANES), lambda i: (i % tiles_per_seq, 0))
    vec = _resident((1, LRU_WIDTH), layer)
    mat = _resident((LRU_WIDTH, LRU_WIDTH), layer)
    sds = jax.ShapeDtypeStruct
    grouped_shape = sds((SWA_RES, SWA_SLABS, n // SWA_RES, LANES), F32)
    out_shapes = (
        sds((n, GLA_QK_PAD), BF16), sds((n, GLA_QK_PAD), BF16), sds((n, GLA_QK_PAD), BF16),
        sds((n, GLA_QK_PAD), BF16), sds((n // GLA_CHUNK, 1, GLA_QK_PAD), F32),
        sds((n, GLA_V_PAD), BF16), sds((n, GLA_V_PAD), BF16), sds((n, LRU_WIDTH), BF16),
        grouped_shape, grouped_shape, grouped_shape,
    )
    return pl.pallas_call(
        functools.partial(_inproj_kernel, tiles_per_seq=tiles_per_seq),
        grid=(n // tm,),
        in_specs=[row(D_MODEL), _resident((1, D_MODEL), layer), _resident((D_MODEL, IN_COLS_PACKED), layer),
                  _resident((GLR_PAD, GLA_QK_PAD), layer), _resident((1, GLA_QK_PAD), layer), table, table, table,
                  _resident((LRU_CONV, LRU_WIDTH), layer), vec, mat, vec, mat, vec, vec, vec],
        out_specs=(row(GLA_QK_PAD), row(GLA_QK_PAD), row(GLA_QK_PAD), row(GLA_QK_PAD),
                   pl.BlockSpec((tm // GLA_CHUNK, 1, GLA_QK_PAD), lambda i: (i, 0, 0)),
                   row(GLA_V_PAD), row(GLA_V_PAD), row(LRU_WIDTH), grouped, grouped, grouped),
        out_shape=out_shapes,
        scratch_shapes=[pltpu.VMEM((8, LRU_WIDTH), F32), pltpu.VMEM((1, LRU_WIDTH), F32),
                        pltpu.VMEM((3 * SWA_SLABS, tm, LANES), F32)],
        compiler_params=_params(("arbitrary",), 48),
        name="inproj",
    )(x, gn, w, wg, bg, cos, sa, sb, *lru)


def _gla_kernel(qe_ref, ke_ref, qi_ref, kd_ref, dec_ref, v_ref, r_ref, gn_ref, o_ref, st_ref, *, seq):
    c = GLA_CHUNK
    st_ref[...] = jnp.zeros_like(st_ref)

    lane_q = lax.broadcasted_iota(jnp.int32, (1, GLA_QK_PAD), 1)
    q_masks = [((lane_q >= h * GLA_DK) & (lane_q < (h + 1) * GLA_DK)).astype(BF16) for h in range(GLA_HEADS)]
    ri = lax.broadcasted_iota(jnp.int32, (GLA_HEADS * c, c), 0) & (c - 1)
    ci = lax.broadcasted_iota(jnp.int32, (GLA_HEADS * c, c), 1)
    causal = ci <= ri
    gn = gn_ref[...]

    per_step = 8
    heads = [slice(h * LANES, (h + 1) * LANES) for h in range(GLA_HEADS)]

    def step(i, carry):
        ids = [per_step * i + u for u in range(per_step)]
        rows = [pl.ds(pl.multiple_of(ci * c, c), c) for ci in ids]
        vbs = [v_ref[r, :] for r in rows]
        scores = [_dot_nt(jnp.concatenate([qe_ref[r, :] * m for m in q_masks], axis=0), ke_ref[r, :])
                  for r in rows]
        incs = []
        for r, vb in zip(rows, vbs):
            kd = kd_ref[r, :]
            incs.append([_dot_tn(vb[:, sl], kd * q_masks[h]) for h, sl in enumerate(heads)])
        st = [st_ref[sl, :] for sl in heads]
        for ci, r, vb, s, inc in zip(ids, rows, vbs, scores, incs):
            o_inter = _dot_nt(qi_ref[r, :], jnp.concatenate(st, axis=0).astype(BF16))
            s = jnp.where(causal, s, 0.0).astype(BF16)
            dec = dec_ref[ci]
            gate = r_ref[r, :]
            for h, sl in enumerate(heads):
                o = o_inter[:, sl] + _dot(s[h * c:(h + 1) * c, :], vb[:, sl])
                st[h] = st[h] * dec + inc[h]
                ms = jnp.sum(o * o, axis=-1, keepdims=True) * (1.0 / GLA_DV)
                y = o * lax.rsqrt(ms + EPS) * gn[:, sl] * gate[:, sl].astype(F32)
                o_ref[r, sl] = y.astype(o_ref.dtype)
        for h, sl in enumerate(heads):
            st_ref[sl, :] = st[h]
        return carry

    lax.fori_loop(0, seq // (c * per_step), step, 0)


def _gla(qe, ke, qi, kd, dec, av, ar, gn, seq, layer):
    n = qe.shape[0]
    blk = lambda width: pl.BlockSpec((seq, width), lambda b: (b, 0))
    return pl.pallas_call(
        functools.partial(_gla_kernel, seq=seq),
        grid=(n // seq,),
        in_specs=[blk(GLA_QK_PAD), blk(GLA_QK_PAD), blk(GLA_QK_PAD), blk(GLA_QK_PAD),
                  pl.BlockSpec((seq // GLA_CHUNK, 1, GLA_QK_PAD), lambda b: (b, 0, 0)),
                  blk(GLA_V_PAD), blk(GLA_V_PAD), _resident((1, GLA_V_PAD), layer)],
        out_specs=blk(GLA_V_PAD),
        out_shape=jax.ShapeDtypeStruct((n, GLA_V_PAD), BF16),
        scratch_shapes=[pltpu.VMEM((GLA_V_PAD, GLA_QK_PAD), F32)],
        compiler_params=_params(("parallel",), 40),
        name="gla",
    )(qe, ke, qi, kd, dec, av, ar, gn)


def _swa_kernel(qd_ref, kd_ref, vd_ref, gn_ref, o_ref, acc_ref, m_ref, l_ref, *, seq):
    blk = SWA_BLOCK
    res = SWA_RES
    nj = seq // res
    assert nj == blk
    lane = lax.broadcasted_iota(jnp.int32, (1, LANES), 1)
    head0 = lane < HEAD_DIM

    def permuted_masks(groups):
        q_per = blk // groups
        k_per = 2 * blk // groups
        qi = lax.broadcasted_iota(jnp.int32, (blk, 2 * blk), 0)
        ki = lax.broadcasted_iota(jnp.int32, (blk, 2 * blk), 1)
        tq = (qi % q_per) * groups + qi // q_per
        tk = (ki % k_per) * groups + ki // k_per
        two_blocks = (tk >= tq) & (tk <= tq + blk)
        first_block = tk <= tq
        return two_blocks, first_block

    def attend_all(units):
        both = lambda t: jnp.where(head0, t[:blk], t[blk:])
        scores = []
        for qs, kcat, _, _ in units:
            q2 = jnp.concatenate([jnp.where(head0, qs, 0.0), jnp.where(head0, 0.0, qs)], axis=0)
            scores.append(_dot_nt(q2.astype(BF16), kcat.astype(BF16)))
        soft = []
        for s, (_, _, _, mask) in zip(scores, units):
            s = jnp.where(jnp.concatenate([mask, mask], axis=0), s, NEG_BIG)
            m = jnp.max(s, axis=-1, keepdims=True)
            soft.append((m, jnp.exp2(s - m).astype(BF16)))
        outs = []
        for (m, p), (_, _, vcat, _) in zip(soft, units):
            v_ones = jnp.concatenate([vcat.astype(BF16), jnp.ones(vcat.shape, BF16)], axis=1)
            pv = _dot(p, v_ones)
            outs.append((both(m), both(pv[:, LANES:]), both(pv[:, :LANES])))
        return outs

    def merge(old, new):
        (m_old, l_old, acc_old), (m_new, l_new, pv) = old, new
        m_tot = jnp.maximum(m_old, m_new)
        w_old = jnp.exp2(m_old - m_tot)
        w_new = jnp.exp2(m_new - m_tot)
        return m_tot, w_old * l_old + w_new * l_new, w_old * acc_old + w_new * pv

    qi1 = lax.broadcasted_iota(jnp.int32, (blk, blk), 0)
    kj1 = lax.broadcasted_iota(jnp.int32, (blk, blk), 1)
    causal = kj1 <= qi1

    def dil16(i, carry):
        units = [(2 * i + u, s) for u in range(2) for s in range(SWA_SLABS)]
        loaded = [(qd_ref[c, s], kd_ref[c, s], vd_ref[c, s]) for c, s in units]
        results = attend_all([(qs, ks, vs, causal) for qs, ks, vs in loaded])
        for (c, s), (m_new, l_new, pv) in zip(units, results):
            m_ref[c, s] = m_new
            l_ref[c, s] = l_new
            acc_ref[c, s] = pv
        return carry

    lax.fori_loop(0, res // 2, dil16, 0)

    def dilated(groups):
        n_blocks = nj * groups // blk
        n_res = res // groups
        q_per = blk // groups
        k_per = 2 * blk // groups
        two_blocks, first_block = permuted_masks(groups)
        nb_bits = n_blocks.bit_length() - 1
        assert n_blocks == 1 << nb_bits

        def step(i, carry):
            loaded = []
            for u in range(2):
                it = 2 * i + u
                r = lax.shift_right_logical(it, nb_bits)
                bi = it & (n_blocks - 1)
                q_rows = pl.ds(pl.multiple_of(bi * q_per, q_per), q_per)
                k_rows = pl.ds(pl.multiple_of(jnp.maximum(bi - 1, 0) * q_per, q_per), k_per)
                mask = (two_blocks & (bi > 0)) | (first_block & (bi == 0))
                gather = lambda ref, s, rows, r=r: jnp.concatenate(
                    [ref[a * n_res + r, s, rows, :] for a in range(groups)], axis=0)
                for s in range(SWA_SLABS):
                    old = (gather(m_ref, s, q_rows), gather(l_ref, s, q_rows), gather(acc_ref, s, q_rows))
                    loaded.append((r, s, q_rows, mask, gather(qd_ref, s, q_rows), gather(kd_ref, s, k_rows),
                                   gather(vd_ref, s, k_rows), old))
            news = attend_all([(qs, kcat, vcat, mask) for _, _, _, mask, qs, kcat, vcat, _ in loaded])
            results = [merge(unit[-1], new) for unit, new in zip(loaded, news)]
            for (r, s, q_rows, *_), (m_new, l_new, acc_new) in zip(loaded, results):
                for a in range(groups):
                    piece = slice(a * q_per, (a + 1) * q_per)
                    m_ref[a * n_res + r, s, q_rows, :] = m_new[piece]
                    l_ref[a * n_res + r, s, q_rows, :] = l_new[piece]
                    acc_ref[a * n_res + r, s, q_rows, :] = acc_new[piece]
            return carry

        lax.fori_loop(0, n_res * n_blocks // 2, step, 0)

    dilated(4)
    dilated(16)

    def finish(c, carry):
        outs = [acc_ref[c, s] / l_ref[c, s] for s in range(SWA_SLABS)]
        ss = sum(jnp.sum(o * o, axis=-1, keepdims=True) for o in outs)
        scale = lax.rsqrt(ss * (1.0 / SWA_WIDTH) + EPS)
        rows = pl.ds(c, nj, stride=res)
        for s in range(SWA_SLABS):
            o_ref[s, rows, :] = outs[s] * scale * gn_ref[:, s * LANES:(s + 1) * LANES]
        return carry

    lax.fori_loop(0, res, finish, 0)


def _swa(cq, ck, cv, gn, seq, layer):
    n = cq.shape[2] * SWA_RES
    nj = seq // SWA_RES
    grouped = pl.BlockSpec((SWA_RES, SWA_SLABS, nj, LANES), lambda b: (0, 0, b, 0))
    state = pltpu.VMEM((SWA_RES, SWA_SLABS, nj, LANES), F32)
    return pl.pallas_call(
        functools.partial(_swa_kernel, seq=seq),
        grid=(n // seq,),
        in_specs=[grouped, grouped, grouped, _resident((1, SWA_WIDTH), layer)],
        out_specs=pl.BlockSpec((SWA_SLABS, seq, LANES), lambda b: (0, b, 0)),
        out_shape=jax.ShapeDtypeStruct((SWA_SLABS, n, LANES), F32),
        scratch_shapes=[state] * 3,
        compiler_params=_params(("parallel",), 48),
        name="swa",
    )(cq, ck, cv, gn)


def _ffn_kernel(x_ref, oa_ref, ob_ref, oc_ref, woa_ref, wob_ref, woc_ref, gn_ref, wup_ref, cw_ref, cb_ref,
                wdn_ref, gfin_ref, out_ref, carry_ref, act_ref, *, tiles_per_seq, final):
    tm = x_ref.shape[0]

    @pl.when(pl.program_id(0) % tiles_per_seq == 0)
    def _():
        carry_ref[...] = jnp.zeros_like(carry_ref)

    oc = jnp.concatenate([oc_ref[s].astype(BF16) for s in range(SWA_SLABS)], axis=1)
    x1 = x_ref[...] + _dot(oa_ref[...], woa_ref[...]) + _dot(ob_ref[...], wob_ref[...]) + _dot(oc, woc_ref[...])
    h = (x1 * _rms_scale(x1) * gn_ref[...]).astype(BF16)

    def conv_up(c0):
        cols = slice(c0, c0 + FFN_CHUNK)
        up = _dot(h, wup_ref[:, cols])
        ext = jnp.concatenate([carry_ref[:, cols], up], axis=0)
        carry_ref[:, cols] = up[tm - 8:tm, :]
        cw = cw_ref[:, cols]
        return (cb_ref[:, cols] + up * cw[2:3, :] + pltpu.roll(ext, 2, 0)[8:, :] * cw[0:1, :]
                + pltpu.roll(ext, 1, 0)[8:, :] * cw[1:2, :])

    for j in range(FFN_DIM // FFN_CHUNK):
        val = conv_up(j * FFN_CHUNK)
        gate = conv_up(FFN_DIM + j * FFN_CHUNK)
        act_ref[:, j * FFN_CHUNK:(j + 1) * FFN_CHUNK] = (_gelu_tanh(gate) * val).astype(BF16)
    x2 = x1 + _dot(act_ref[...], wdn_ref[...])
    if final:
        x2 = x2 * _rms_scale(x2) * gfin_ref[...]
    out_ref[...] = x2


def _ffn(x, oa, ob, oc, woa, wob, woc, gn, wup, cw, cb, wdn, gfin, seq, layer, final):
    n = x.shape[0]
    tm = TOKEN_TILE
    row = lambda width: pl.BlockSpec((tm, width), lambda i: (i, 0))
    return pl.pallas_call(
        functools.partial(_ffn_kernel, tiles_per_seq=seq // tm, final=final),
        grid=(n // tm,),
        in_specs=[row(D_MODEL), row(GLA_V_PAD), row(LRU_WIDTH),
                  pl.BlockSpec((SWA_SLABS, tm, LANES), lambda i: (0, i, 0)),
                  _resident((GLA_V_PAD, D_MODEL), layer), _resident((LRU_WIDTH, D_MODEL), layer),
                  _resident((SWA_WIDTH, D_MODEL), layer), _resident((1, D_MODEL), layer),
                  _resident((D_MODEL, 2 * FFN_DIM), layer), _resident((FFN_CONV, 2 * FFN_DIM), layer),
                  _resident((1, 2 * FFN_DIM), layer), _resident((FFN_DIM, D_MODEL), layer), _resident((1, D_MODEL))],
        out_specs=row(D_MODEL),
        out_shape=jax.ShapeDtypeStruct((n, D_MODEL), F32),
        scratch_shapes=[pltpu.VMEM((8, 2 * FFN_DIM), F32), pltpu.VMEM((tm, FFN_DIM), BF16)],
        compiler_params=_params(("arbitrary",), 56),
        name="ffn",
    )(x, oa, ob, oc, woa, wob, woc, gn, wup, cw, cb, wdn, gfin)


def _rope_tables(seq):
    half = ROPE_DIM // 2
    inv = ROPE_THETA ** (-jnp.arange(0, ROPE_DIM, 2, dtype=F32) / ROPE_DIM)
    ang = jnp.arange(seq, dtype=F32)[:, None] * inv[None, :]
    cos, sin = jnp.cos(ang), jnp.sin(ang)
    ones = jnp.ones((seq, HEAD_DIM - ROPE_DIM), F32)
    zeros = jnp.zeros((seq, HEAD_DIM - ROPE_DIM), F32)
    zh = jnp.zeros((seq, half), F32)
    per_head = lambda parts: jnp.tile(jnp.concatenate(parts, axis=1), (1, LANES // HEAD_DIM))
    return per_head([cos, cos, ones]), per_head([-sin, zh, zeros]), per_head([zh, sin, zeros])


def _pad_heads(w, axis):
    shape = list(w.shape)
    shape[axis:axis + 1] = [GLA_HEADS, GLA_DV]
    w = w.reshape(shape)
    pad = [(0, 0)] * w.ndim
    pad[axis + 1] = (0, LANES - GLA_DV)
    shape[axis:axis + 2] = [GLA_V_PAD]
    return jnp.pad(w, pad).reshape(shape)


def _pack_w_in(w):
    z = lambda width: jnp.zeros(w.shape[:-1] + (width,), w.dtype)
    o_v = 2 * GLA_QK
    o_r = o_v + GLA_WIDTH
    o_glr = o_r + GLA_WIDTH
    return jnp.concatenate([
        w[..., 0:GLA_QK], z(GLA_QK_PAD - GLA_QK), w[..., GLA_QK:o_v], z(GLA_QK_PAD - GLA_QK),
        _pad_heads(w[..., o_v:o_r], 2), _pad_heads(w[..., o_r:o_glr], 2),
        w[..., o_glr + GLA_GATE_RANK:], w[..., o_glr:o_glr + GLA_GATE_RANK], z(GLR_PAD - GLA_GATE_RANK)],
        axis=-1).astype(BF16)


def _block_diag(w):
    layers, nb, bs, _ = w.shape
    eye = jnp.eye(nb, dtype=w.dtype)
    return jnp.einsum('lnij,nm->lnimj', w, eye).reshape(layers, nb * bs, nb * bs)


def kernel(x, norm_mix, w_in, gla_w_gate, gla_b_gate, gla_norm, lru_conv_w, lru_conv_b, lru_w_a, lru_b_a,
           lru_w_x, lru_b_x, lru_lambda, lru_norm, swa_norm, w_out, norm_ffn, ffn_w_up, ffn_conv_w, ffn_conv_b,
           ffn_w_down, norm_final):
    batch, seq, _ = x.shape
    depth = w_in.shape[0]
    assert seq % TOKEN_TILE == 0 and seq == SWA_BLOCK * SWA_RES
    cos, sa, sb = _rope_tables(seq)
    xf = x.reshape(batch * seq, D_MODEL)
    rows = lambda v: v.reshape(depth, 1, -1)
    wg = jnp.pad(gla_w_gate, ((0, 0), (0, GLR_PAD - GLA_GATE_RANK), (0, GLA_QK_PAD - GLA_QK))).astype(BF16)
    bg = rows(jnp.pad(gla_b_gate, ((0, 0), (0, GLA_QK_PAD - GLA_QK))))
    w_in_p = _pack_w_in(w_in)
    gla_gn = rows(_pad_heads(jnp.tile(gla_norm, (1, GLA_HEADS)), 1))
    wa = _block_diag(lru_w_a).astype(BF16)
    wx = _block_diag(lru_w_x).astype(BF16)
    wo = w_out.astype(BF16)
    wo_a = _pad_heads(wo[:, :GLA_WIDTH], 1)
    wo_b = wo[:, GLA_WIDTH:GLA_WIDTH + LRU_WIDTH]
    wo_c = wo[:, GLA_WIDTH + LRU_WIDTH:]
    w_up = ffn_w_up.astype(BF16)
    w_dn = ffn_w_down.astype(BF16)
    vecs = [rows(v) for v in (norm_mix, lru_conv_b, lru_b_a, lru_b_x, lru_lambda, lru_norm, swa_norm, norm_ffn,
                              ffn_conv_b)]
    g_mix, cb_lru, ba, bx, lam, g_lru, g_swa, g_ffn, cb_ffn = vecs
    g_fin = norm_final.reshape(1, -1)
    lru = (lru_conv_w, cb_lru, wa, ba, wx, bx, lam, g_lru)
    for l in range(depth):
        qe, ke, qi, kd, dec, av, ar, o_b, cq, ck, cv = _inproj(xf, g_mix, w_in_p, wg, bg, cos, sa, sb, lru, seq, l)
        o_a = _gla(qe, ke, qi, kd, dec, av, ar, gla_gn, seq, l)
        o_c = _swa(cq, ck, cv, g_swa, seq, l)
        xf = _ffn(xf, o_a, o_b, o_c, wo_a, wo_b, wo_c, g_ffn, w_up, ffn_conv_w, cb_ffn, w_dn, g_fin, seq, l,
                  l == depth - 1)
    return xf.reshape(batch, seq, D_MODEL)
```

```python
import functools

import jax
import jax.numpy as jnp
import numpy as np
from jax import lax
from jax.experimental import pallas as pl
from jax.experimental.pallas import tpu as pltpu

F32 = jnp.float32
BF16 = jnp.bfloat16

D_MODEL = 1024
EPS = 1e-6
HEAD_DIM = 64
LANES = 128
GLA_HEADS = 4
GLA_DK = 48
GLA_DV = 96
GLA_QK = GLA_HEADS * GLA_DK
GLA_QK_PAD = 256
GLA_WIDTH = GLA_HEADS * GLA_DV
GLA_V_PAD = GLA_HEADS * LANES
GLA_GATE_RANK = 16
GLA_GATE_TEMP = 16.0
GLA_CHUNK = 64
LRU_WIDTH = 256
LRU_BLOCKS = 8
LRU_C = 8.0
LRU_CONV = 4
SWA_HEADS = 6
SWA_WIDTH = SWA_HEADS * HEAD_DIM
SWA_BLOCK = 128
SWA_SLABS = SWA_WIDTH // LANES
SWA_RES = 16
ROPE_THETA = 500000.0
ROPE_DIM = HEAD_DIM // 4
FFN_DIM = 2816
FFN_CONV = 3
FFN_CHUNK = 256
GLR_PAD = 128

_C_AQ = 0
_C_AK = _C_AQ + GLA_QK_PAD
_C_AV = _C_AK + GLA_QK_PAD
_C_AR = _C_AV + GLA_V_PAD
_C_BG = _C_AR + GLA_V_PAD
_C_BI = _C_BG + LRU_WIDTH
_C_CQ = _C_BI + LRU_WIDTH
_C_CK = _C_CQ + SWA_WIDTH
_C_CV = _C_CK + SWA_WIDTH
_C_GLR = _C_CV + SWA_WIDTH
IN_COLS_PACKED = _C_GLR + GLR_PAD

TOKEN_TILE = 512
V7X_VMEM_BYTES = 64 * 1024 * 1024
NEG_BIG = -1e30
LOG2E = float(np.log2(np.e))


def _params(semantics, vmem_mib):
    assert vmem_mib * 1024 * 1024 < V7X_VMEM_BYTES
    return pltpu.CompilerParams(dimension_semantics=semantics,
                                vmem_limit_bytes=vmem_mib * 1024 * 1024)


def _resident(shape, layer=None):
    nd = len(shape)
    if layer is None:
        return pl.BlockSpec(shape, lambda *_: (0,) * nd, pipeline_mode=pl.Buffered(1))
    return pl.BlockSpec((None,) + tuple(shape), lambda *_: (layer,) + (0,) * nd, pipeline_mode=pl.Buffered(1))


def _rms_scale(x):
    return lax.rsqrt(jnp.mean(x * x, axis=-1, keepdims=True) + EPS)


def _gelu_tanh(x):
    c = float(np.sqrt(2.0 / np.pi))
    return x * (0.5 + 0.5 * jnp.tanh(x * (c + (0.044715 * c) * (x * x))))


def _dot(a, b):
    return jnp.dot(a, b, preferred_element_type=F32)


def _dot_nt(a, b):
    return lax.dot_general(a, b, (((1,), (1,)), ((), ())), preferred_element_type=F32)


def _dot_tn(a, b):
    return lax.dot_general(a, b, (((0,), (0,)), ((), ())), preferred_element_type=F32)


def _inproj_kernel(x_ref, gn_ref, w_ref, wg_ref, bg_ref, cos_ref, sa_ref, sb_ref,
                   cw_ref, cb_ref, wa_ref, ba_ref, wx_ref, bx_ref, lam_ref, gnb_ref,
                   qe_ref, ke_ref, qi_ref, kd_ref, dec_ref, av_ref, ar_ref, ob_ref, cq_ref, ck_ref, cv_ref,
                   xtail_ref, hc_ref, rg_ref, *, tiles_per_seq):
    tm = x_ref.shape[0]
    c = GLA_CHUNK

    @pl.when(pl.program_id(0) % tiles_per_seq == 0)
    def _():
        xtail_ref[...] = jnp.zeros_like(xtail_ref)
        hc_ref[...] = jnp.zeros_like(hc_ref)

    x = x_ref[...]
    h = (x * _rms_scale(x) * gn_ref[...]).astype(BF16)

    def proj(c0, width):
        return _dot(h, w_ref[:, c0:c0 + width])

    tile = 256
    sub = c
    row_sub = lax.broadcasted_iota(jnp.int32, (sub, 1), 0)
    silu = lambda t: (t * jax.nn.sigmoid(t)).astype(BF16)

    def scan_steps(update):
        shift = 1
        while shift < sub:
            if shift < 8:
                shifted = lambda v, fill, s=shift: jnp.where(row_sub >= s, pltpu.roll(v, s, 0), fill)
            else:
                shifted = lambda v, fill, s=shift: jnp.concatenate(
                    [jnp.full((s, v.shape[1]), fill, F32), v[:sub - s]], axis=0)
            update(shifted)
            shift *= 2

    p_glr = proj(_C_GLR, GLR_PAD)
    p_bi = proj(_C_BI, LRU_WIDTH)
    z = _dot(p_glr.astype(BF16), wg_ref[...]) + bg_ref[...]
    p_aq = proj(_C_AQ, GLA_QK_PAD)

    ext = jnp.concatenate([xtail_ref[...], p_bi], axis=0)
    xtail_ref[...] = p_bi[tm - 8:tm, :]
    cw = cw_ref[...]
    u = cb_ref[...] + p_bi * cw[LRU_CONV - 1:LRU_CONV, :]
    for kk in range(LRU_CONV - 1):
        u = u + pltpu.roll(ext, LRU_CONV - 1 - kk, 0)[8:, :] * cw[kk:kk + 1, :]
    ub = u.astype(BF16)
    rg_pre = _dot(ub, wa_ref[...]) + ba_ref[...]
    ig_pre = _dot(ub, wx_ref[...]) + bx_ref[...]
    p_ak = proj(_C_AK, GLA_QK_PAD)
    p_bg = proj(_C_BG, LRU_WIDTH)

    def value_tile(k):
        if k == 1:
            av_ref[:, :tile] = proj(_C_AV, tile).astype(BF16)
        elif k == 3:
            av_ref[:, tile:] = proj(_C_AV + tile, tile).astype(BF16)
        elif k == 5:
            ar_ref[:, :tile] = silu(proj(_C_AR, tile))
        elif k == 7:
            ar_ref[:, tile:] = silu(proj(_C_AR + tile, tile))

    for k in range(tm // sub):
        rows = slice(k * sub, (k + 1) * sub)
        zk = z[rows]
        bsum = [(jnp.minimum(zk, 0.0) - jnp.log(1.0 + jnp.exp(-jnp.abs(zk)))) * (1.0 / GLA_GATE_TEMP)]

        def add_back(shifted, bsum=bsum):
            bsum[0] = bsum[0] + shifted(bsum[0], 0.0)

        scan_steps(add_back)
        b = bsum[0]
        b_mid = b[sub // 2 - 1:sub // 2, :]
        b_last = b[sub - 1:sub, :]
        q = p_aq[rows] * (GLA_DK ** -0.5)
        kx = p_ak[rows]
        qe_ref[rows, :] = (q * jnp.exp(b - b_mid)).astype(BF16)
        ke_ref[rows, :] = (kx * jnp.exp(b_mid - b)).astype(BF16)
        qi_ref[rows, :] = (q * jnp.exp(b)).astype(BF16)
        kd_ref[rows, :] = (kx * jnp.exp(b_last - b)).astype(BF16)
        dec_ref[k] = jnp.exp(b_last)
        value_tile(k)

    neg_lam = -lam_ref[...]
    softplus = jnp.maximum(neg_lam, 0.0) + jnp.log(1.0 + jnp.exp(-jnp.abs(neg_lam)))
    carry = hc_ref[...]
    swa_proj = {}
    for k in range(tm // sub):
        rows = slice(k * sub, (k + 1) * sub)
        log_a = (-LRU_C) * jax.nn.sigmoid(rg_pre[rows]) * softplus
        th = jnp.tanh(log_a)
        state = [jnp.exp(log_a), jnp.sqrt(-2.0 * th / (1.0 - th)) * (jax.nn.sigmoid(ig_pre[rows]) * u[rows])]

        def combine(shifted, state=state):
            a, hs = state
            state[1] = hs + a * shifted(hs, 0.0)
            state[0] = a * shifted(a, 1.0)

        scan_steps(combine)
        a, hs = state
        hs = hs + a * carry
        carry = hs[sub - 1:sub, :]
        y = hs * _gelu_tanh(p_bg[rows])
        ob_ref[rows, :] = (y * _rms_scale(y) * gnb_ref[...]).astype(ob_ref.dtype)
        if k % 2 == 1:
            swa_proj[k // 2] = proj(_C_CQ + (k // 2) * tile, tile)
    hc_ref[...] = carry
    p_c0, p_c1, p_c2, p_c3 = (swa_proj[i] for i in range(4))

    cos = cos_ref[...]
    sa = sa_ref[...]
    sb = sb_ref[...]

    def rope(t):
        return t * cos + pltpu.roll(t, LANES - ROPE_DIM // 2, 1) * sa + pltpu.roll(t, ROPE_DIM // 2, 1) * sb

    def regroup(out_ref, s, slot, val):
        rg_ref[slot] = val
        for cc in range(SWA_RES):
            out_ref[cc, s] = rg_ref[slot, pl.ds(cc, tm // SWA_RES, stride=SWA_RES), :]

    q_scale = HEAD_DIM ** -0.5 * LOG2E
    regroup(cq_ref, 0, 0, rope(p_c0[:, :LANES] * q_scale))
    regroup(cq_ref, 1, 1, rope(p_c0[:, LANES:] * q_scale))
    p_c4 = proj(_C_CQ + 4 * tile, LANES)
    regroup(cq_ref, 2, 2, rope(p_c1[:, :LANES] * q_scale))
    regroup(ck_ref, 0, 3, rope(p_c1[:, LANES:]))
    regroup(ck_ref, 1, 4, rope(p_c2[:, :LANES]))
    regroup(ck_ref, 2, 5, rope(p_c2[:, LANES:]))
    regroup(cv_ref, 0, 6, p_c3[:, :LANES])
    regroup(cv_ref, 1, 7, p_c3[:, LANES:])
    regroup(cv_ref, 2, 8, p_c4)


def _inproj(x, gn, w, wg, bg, cos, sa, sb, lru, seq, layer):
    n = x.shape[0]
    tm = TOKEN_TILE
    tiles_per_seq = seq // tm
    row = lambda width: pl.BlockSpec((tm, width), lambda i: (i, 0))
    grouped = pl.BlockSpec((SWA_RES, SWA_SLABS, tm // SWA_RES, LANES), lambda i: (0, 0, i, 0))
    table = pl.BlockSpec((tm, LANES), lambda i: (i % tiles_per_seq, 0))
    vec = _resident((1, LRU_WIDTH), layer)
    mat = _resident((LRU_WIDTH, LRU_WIDTH), layer)
    sds = jax.ShapeDtypeStruct
    grouped_shape = sds((SWA_RES, SWA_SLABS, n // SWA_RES, LANES), F32)
    out_shapes = (
        sds((n, GLA_QK_PAD), BF16), sds((n, GLA_QK_PAD), BF16), sds((n, GLA_QK_PAD), BF16),
        sds((n, GLA_QK_PAD), BF16), sds((n // GLA_CHUNK, 1, GLA_QK_PAD), F32),
        sds((n, GLA_V_PAD), BF16), sds((n, GLA_V_PAD), BF16), sds((n, LRU_WIDTH), BF16),
        grouped_shape, grouped_shape, grouped_shape,
    )
    return pl.pallas_call(
        functools.partial(_inproj_kernel, tiles_per_seq=tiles_per_seq),
        grid=(n // tm,),
        in_specs=[row(D_MODEL), _resident((1, D_MODEL), layer), _resident((D_MODEL, IN_COLS_PACKED), layer),
                  _resident((GLR_PAD, GLA_QK_PAD), layer), _resident((1, GLA_QK_PAD), layer), table, table, table,
                  _resident((LRU_CONV, LRU_WIDTH), layer), vec, mat, vec, mat, vec, vec, vec],
        out_specs=(row(GLA_QK_PAD), row(GLA_QK_PAD), row(GLA_QK_PAD), row(GLA_QK_PAD),
                   pl.BlockSpec((tm // GLA_CHUNK, 1, GLA_QK_PAD), lambda i: (i, 0, 0)),
                   row(GLA_V_PAD), row(GLA_V_PAD), row(LRU_WIDTH), grouped, grouped, grouped),
        out_shape=out_shapes,
        scratch_shapes=[pltpu.VMEM((8, LRU_WIDTH), F32), pltpu.VMEM((1, LRU_WIDTH), F32),
                        pltpu.VMEM((3 * SWA_SLABS, tm, LANES), F32)],
        compiler_params=_params(("arbitrary",), 48),
        name="inproj",
    )(x, gn, w, wg, bg, cos, sa, sb, *lru)


def _gla_kernel(qe_ref, ke_ref, qi_ref, kd_ref, dec_ref, v_ref, r_ref, gn_ref, o_ref, st_ref, *, seq):
    c = GLA_CHUNK
    st_ref[...] = jnp.zeros_like(st_ref)

    lane_q = lax.broadcasted_iota(jnp.int32, (1, GLA_QK_PAD), 1)
    q_masks = [((lane_q >= h * GLA_DK) & (lane_q < (h + 1) * GLA_DK)).astype(BF16) for h in range(GLA_HEADS)]
    ri = lax.broadcasted_iota(jnp.int32, (GLA_HEADS * c, c), 0) & (c - 1)
    ci = lax.broadcasted_iota(jnp.int32, (GLA_HEADS * c, c), 1)
    causal = ci <= ri
    gn = gn_ref[...]

    per_step = 8
    heads = [slice(h * LANES, (h + 1) * LANES) for h in range(GLA_HEADS)]

    def step(i, carry):
        ids = [per_step * i + u for u in range(per_step)]
        rows = [pl.ds(pl.multiple_of(ci * c, c), c) for ci in ids]
        vbs = [v_ref[r, :] for r in rows]
        scores = [_dot_nt(jnp.concatenate([qe_ref[r, :] * m for m in q_masks], axis=0), ke_ref[r, :])
                  for r in rows]
        incs = []
        for r, vb in zip(rows, vbs):
            kd = kd_ref[r, :]
            incs.append([_dot_tn(vb[:, sl], kd * q_masks[h]) for h, sl in enumerate(heads)])
        st = [st_ref[sl, :] for sl in heads]
        for ci, r, vb, s, inc in zip(ids, rows, vbs, scores, incs):
            o_inter = _dot_nt(qi_ref[r, :], jnp.concatenate(st, axis=0).astype(BF16))
            s = jnp.where(causal, s, 0.0).astype(BF16)
            dec = dec_ref[ci]
            gate = r_ref[r, :]
            for h, sl in enumerate(heads):
                o = o_inter[:, sl] + _dot(s[h * c:(h + 1) * c, :], vb[:, sl])
                st[h] = st[h] * dec + inc[h]
                ms = jnp.sum(o * o, axis=-1, keepdims=True) * (1.0 / GLA_DV)
                y = o * lax.rsqrt(ms + EPS) * gn[:, sl] * gate[:, sl].astype(F32)
                o_ref[r, sl] = y.astype(o_ref.dtype)
        for h, sl in enumerate(heads):
            st_ref[sl, :] = st[h]
        return carry

    lax.fori_loop(0, seq // (c * per_step), step, 0)


def _gla(qe, ke, qi, kd, dec, av, ar, gn, seq, layer):
    n = qe.shape[0]
    blk = lambda width: pl.BlockSpec((seq, width), lambda b: (b, 0))
    return pl.pallas_call(
        functools.partial(_gla_kernel, seq=seq),
        grid=(n // seq,),
        in_specs=[blk(GLA_QK_PAD), blk(GLA_QK_PAD), blk(GLA_QK_PAD), blk(GLA_QK_PAD),
                  pl.BlockSpec((seq // GLA_CHUNK, 1, GLA_QK_PAD), lambda b: (b, 0, 0)),
                  blk(GLA_V_PAD), blk(GLA_V_PAD), _resident((1, GLA_V_PAD), layer)],
        out_specs=blk(GLA_V_PAD),
        out_shape=jax.ShapeDtypeStruct((n, GLA_V_PAD), BF16),
        scratch_shapes=[pltpu.VMEM((GLA_V_PAD, GLA_QK_PAD), F32)],
        compiler_params=_params(("parallel",), 40),
        name="gla",
    )(qe, ke, qi, kd, dec, av, ar, gn)


def _swa_kernel(qd_ref, kd_ref, vd_ref, gn_ref, o_ref, acc_ref, m_ref, l_ref, *, seq):
    blk = SWA_BLOCK
    res = SWA_RES
    nj = seq // res
    assert nj == blk
    lane = lax.broadcasted_iota(jnp.int32, (1, LANES), 1)
    head0 = lane < HEAD_DIM

    def permuted_masks(groups):
        q_per = blk // groups
        k_per = 2 * blk // groups
        qi = lax.broadcasted_iota(jnp.int32, (blk, 2 * blk), 0)
        ki = lax.broadcasted_iota(jnp.int32, (blk, 2 * blk), 1)
        tq = (qi % q_per) * groups + qi // q_per
        tk = (ki % k_per) * groups + ki // k_per
        two_blocks = (tk >= tq) & (tk <= tq + blk)
        first_block = tk <= tq
        return two_blocks, first_block

    def attend_all(units):
        both = lambda t: jnp.where(head0, t[:blk], t[blk:])
        scores = []
        for qs, kcat, _, _ in units:
            q2 = jnp.concatenate([jnp.where(head0, qs, 0.0), jnp.where(head0, 0.0, qs)], axis=0)
            scores.append(_dot_nt(q2.astype(BF16), kcat.astype(BF16)))
        soft = []
        for s, (_, _, _, mask) in zip(scores, units):
            s = jnp.where(jnp.concatenate([mask, mask], axis=0), s, NEG_BIG)
            m = jnp.max(s, axis=-1, keepdims=True)
            soft.append((m, jnp.exp2(s - m).astype(BF16)))
        outs = []
        for (m, p), (_, _, vcat, _) in zip(soft, units):
            v_ones = jnp.concatenate([vcat.astype(BF16), jnp.ones(vcat.shape, BF16)], axis=1)
            pv = _dot(p, v_ones)
            outs.append((both(m), both(pv[:, LANES:]), both(pv[:, :LANES])))
        return outs

    def merge(old, new):
        (m_old, l_old, acc_old), (m_new, l_new, pv) = old, new
        m_tot = jnp.maximum(m_old, m_new)
        w_old = jnp.exp2(m_old - m_tot)
        w_new = jnp.exp2(m_new - m_tot)
        return m_tot, w_old * l_old + w_new * l_new, w_old * acc_old + w_new * pv

    qi1 = lax.broadcasted_iota(jnp.int32, (blk, blk), 0)
    kj1 = lax.broadcasted_iota(jnp.int32, (blk, blk), 1)
    causal = kj1 <= qi1

    def dil16(i, carry):
        units = [(2 * i + u, s) for u in range(2) for s in range(SWA_SLABS)]
        loaded = [(qd_ref[c, s], kd_ref[c, s], vd_ref[c, s]) for c, s in units]
        results = attend_all([(qs, ks, vs, causal) for qs, ks, vs in loaded])
        for (c, s), (m_new, l_new, pv) in zip(units, results):
            m_ref[c, s] = m_new
            l_ref[c, s] = l_new
            acc_ref[c, s] = pv
        return carry

    lax.fori_loop(0, res // 2, dil16, 0)

    def dilated(groups):
        n_blocks = nj * groups // blk
        n_res = res // groups
        q_per = blk // groups
        k_per = 2 * blk // groups
        two_blocks, first_block = permuted_masks(groups)
        nb_bits = n_blocks.bit_length() - 1
        assert n_blocks == 1 << nb_bits

        def step(i, carry):
            loaded = []
            for u in range(2):
                it = 2 * i + u
                r = lax.shift_right_logical(it, nb_bits)
                bi = it & (n_blocks - 1)
                q_rows = pl.ds(pl.multiple_of(bi * q_per, q_per), q_per)
                k_rows = pl.ds(pl.multiple_of(jnp.maximum(bi - 1, 0) * q_per, q_per), k_per)
                mask = (two_blocks & (bi > 0)) | (first_block & (bi == 0))
                gather = lambda ref, s, rows, r=r: jnp.concatenate(
                    [ref[a * n_res + r, s, rows, :] for a in range(groups)], axis=0)
                for s in range(SWA_SLABS):
                    old = (gather(m_ref, s, q_rows), gather(l_ref, s, q_rows), gather(acc_ref, s, q_rows))
                    loaded.append((r, s, q_rows, mask, gather(qd_ref, s, q_rows), gather(kd_ref, s, k_rows),
                                   gather(vd_ref, s, k_rows), old))
            news = attend_all([(qs, kcat, vcat, mask) for _, _, _, mask, qs, kcat, vcat, _ in loaded])
            results = [merge(unit[-1], new) for unit, new in zip(loaded, news)]
            for (r, s, q_rows, *_), (m_new, l_new, acc_new) in zip(loaded, results):
                for a in range(groups):
                    piece = slice(a * q_per, (a + 1) * q_per)
                    m_ref[a * n_res + r, s, q_rows, :] = m_new[piece]
                    l_ref[a * n_res + r, s, q_rows, :] = l_new[piece]
                    acc_ref[a * n_res + r, s, q_rows, :] = acc_new[piece]
            return carry

        lax.fori_loop(0, n_res * n_blocks // 2, step, 0)

    dilated(4)
    dilated(16)

    def finish(c, carry):
        outs = [acc_ref[c, s] / l_ref[c, s] for s in range(SWA_SLABS)]
        ss = sum(jnp.sum(o * o, axis=-1, keepdims=True) for o in outs)
        scale = lax.rsqrt(ss * (1.0 / SWA_WIDTH) + EPS)
        rows = pl.ds(c, nj, stride=res)
        for s in range(SWA_SLABS):
            o_ref[s, rows, :] = outs[s] * scale * gn_ref[:, s * LANES:(s + 1) * LANES]
        return carry

    lax.fori_loop(0, res, finish, 0)


def _swa(cq, ck, cv, gn, seq, layer):
    n = cq.shape[2] * SWA_RES
    nj = seq // SWA_RES
    grouped = pl.BlockSpec((SWA_RES, SWA_SLABS, nj, LANES), lambda b: (0, 0, b, 0))
    state = pltpu.VMEM((SWA_RES, SWA_SLABS, nj, LANES), F32)
    return pl.pallas_call(
        functools.partial(_swa_kernel, seq=seq),
        grid=(n // seq,),
        in_specs=[grouped, grouped, grouped, _resident((1, SWA_WIDTH), layer)],
        out_specs=pl.BlockSpec((SWA_SLABS, seq, LANES), lambda b: (0, b, 0)),
        out_shape=jax.ShapeDtypeStruct((SWA_SLABS, n, LANES), F32),
        scratch_shapes=[state] * 3,
        compiler_params=_params(("parallel",), 48),
        name="swa",
    )(cq, ck, cv, gn)


def _ffn_kernel(x_ref, oa_ref, ob_ref, oc_ref, woa_ref, wob_ref, woc_ref, gn_ref, wup_ref, cw_ref, cb_ref,
                wdn_ref, gfin_ref, out_ref, carry_ref, act_ref, *, tiles_per_seq, final):
    tm = x_ref.shape[0]

    @pl.when(pl.program_id(0) % tiles_per_seq == 0)
    def _():
        carry_ref[...] = jnp.zeros_like(carry_ref)

    oc = jnp.concatenate([oc_ref[s].astype(BF16) for s in range(SWA_SLABS)], axis=1)
    x1 = x_ref[...] + _dot(oa_ref[...], woa_ref[...]) + _dot(ob_ref[...], wob_ref[...]) + _dot(oc, woc_ref[...])
    h = (x1 * _rms_scale(x1) * gn_ref[...]).astype(BF16)

    def conv_up(c0):
        cols = slice(c0, c0 + FFN_CHUNK)
        up = _dot(h, wup_ref[:, cols])
        ext = jnp.concatenate([carry_ref[:, cols], up], axis=0)
        carry_ref[:, cols] = up[tm - 8:tm, :]
        cw = cw_ref[:, cols]
        return (cb_ref[:, cols] + up * cw[2:3, :] + pltpu.roll(ext, 2, 0)[8:, :] * cw[0:1, :]
                + pltpu.roll(ext, 1, 0)[8:, :] * cw[1:2, :])

    for j in range(FFN_DIM // FFN_CHUNK):
        val = conv_up(j * FFN_CHUNK)
        gate = conv_up(FFN_DIM + j * FFN_CHUNK)
        act_ref[:, j * FFN_CHUNK:(j + 1) * FFN_CHUNK] = (_gelu_tanh(gate) * val).astype(BF16)
    x2 = x1 + _dot(act_ref[...], wdn_ref[...])
    if final:
        x2 = x2 * _rms_scale(x2) * gfin_ref[...]
    out_ref[...] = x2


def _ffn(x, oa, ob, oc, woa, wob, woc, gn, wup, cw, cb, wdn, gfin, seq, layer, final):
    n = x.shape[0]
    tm = TOKEN_TILE
    row = lambda width: pl.BlockSpec((tm, width), lambda i: (i, 0))
    return pl.pallas_call(
        functools.partial(_ffn_kernel, tiles_per_seq=seq // tm, final=final),
        grid=(n // tm,),
        in_specs=[row(D_MODEL), row(GLA_V_PAD), row(LRU_WIDTH),
                  pl.BlockSpec((SWA_SLABS, tm, LANES), lambda i: (0, i, 0)),
                  _resident((GLA_V_PAD, D_MODEL), layer), _resident((LRU_WIDTH, D_MODEL), layer),
                  _resident((SWA_WIDTH, D_MODEL), layer), _resident((1, D_MODEL), layer),
                  _resident((D_MODEL, 2 * FFN_DIM), layer), _resident((FFN_CONV, 2 * FFN_DIM), layer),
                  _resident((1, 2 * FFN_DIM), layer), _resident((FFN_DIM, D_MODEL), layer), _resident((1, D_MODEL))],
        out_specs=row(D_MODEL),
        out_shape=jax.ShapeDtypeStruct((n, D_MODEL), F32),
        scratch_shapes=[pltpu.VMEM((8, 2 * FFN_DIM), F32), pltpu.VMEM((tm, FFN_DIM), BF16)],
        compiler_params=_params(("arbitrary",), 56),
        name="ffn",
    )(x, oa, ob, oc, woa, wob, woc, gn, wup, cw, cb, wdn, gfin)


def _rope_tables(seq):
    half = ROPE_DIM // 2
    inv = ROPE_THETA ** (-jnp.arange(0, ROPE_DIM, 2, dtype=F32) / ROPE_DIM)
    ang = jnp.arange(seq, dtype=F32)[:, None] * inv[None, :]
    cos, sin = jnp.cos(ang), jnp.sin(ang)
    ones = jnp.ones((seq, HEAD_DIM - ROPE_DIM), F32)
    zeros = jnp.zeros((seq, HEAD_DIM - ROPE_DIM), F32)
    zh = jnp.zeros((seq, half), F32)
    per_head = lambda parts: jnp.tile(jnp.concatenate(parts, axis=1), (1, LANES // HEAD_DIM))
    return per_head([cos, cos, ones]), per_head([-sin, zh, zeros]), per_head([zh, sin, zeros])


def _pad_heads(w, axis):
    shape = list(w.shape)
    shape[axis:axis + 1] = [GLA_HEADS, GLA_DV]
    w = w.reshape(shape)
    pad = [(0, 0)] * w.ndim
    pad[axis + 1] = (0, LANES - GLA_DV)
    shape[axis:axis + 2] = [GLA_V_PAD]
    return jnp.pad(w, pad).reshape(shape)


def _pack_w_in(w):
    z = lambda width: jnp.zeros(w.shape[:-1] + (width,), w.dtype)
    o_v = 2 * GLA_QK
    o_r = o_v + GLA_WIDTH
    o_glr = o_r + GLA_WIDTH
    return jnp.concatenate([
        w[..., 0:GLA_QK], z(GLA_QK_PAD - GLA_QK), w[..., GLA_QK:o_v], z(GLA_QK_PAD - GLA_QK),
        _pad_heads(w[..., o_v:o_r], 2), _pad_heads(w[..., o_r:o_glr], 2),
        w[..., o_glr + GLA_GATE_RANK:], w[..., o_glr:o_glr + GLA_GATE_RANK], z(GLR_PAD - GLA_GATE_RANK)],
        axis=-1).astype(BF16)


def _block_diag(w):
    layers, nb, bs, _ = w.shape
    eye = jnp.eye(nb, dtype=w.dtype)
    return jnp.einsum('lnij,nm->lnimj', w, eye).reshape(layers, nb * bs, nb * bs)


def kernel(x, norm_mix, w_in, gla_w_gate, gla_b_gate, gla_norm, lru_conv_w, lru_conv_b, lru_w_a, lru_b_a,
           lru_w_x, lru_b_x, lru_lambda, lru_norm, swa_norm, w_out, norm_ffn, ffn_w_up, ffn_conv_w, ffn_conv_b,
           ffn_w_down, norm_final):
    batch, seq, _ = x.shape
    depth = w_in.shape[0]
    assert seq % TOKEN_TILE == 0 and seq == SWA_BLOCK * SWA_RES
    cos, sa, sb = _rope_tables(seq)
    xf = x.reshape(batch * seq, D_MODEL)
    rows = lambda v: v.reshape(depth, 1, -1)
    wg = jnp.pad(gla_w_gate, ((0, 0), (0, GLR_PAD - GLA_GATE_RANK), (0, GLA_QK_PAD - GLA_QK))).astype(BF16)
    bg = rows(jnp.pad(gla_b_gate, ((0, 0), (0, GLA_QK_PAD - GLA_QK))))
    w_in_p = _pack_w_in(w_in)
    gla_gn = rows(_pad_heads(jnp.tile(gla_norm, (1, GLA_HEADS)), 1))
    wa = _block_diag(lru_w_a).astype(BF16)
    wx = _block_diag(lru_w_x).astype(BF16)
    wo = w_out.astype(BF16)
    wo_a = _pad_heads(wo[:, :GLA_WIDTH], 1)
    wo_b = wo[:, GLA_WIDTH:GLA_WIDTH + LRU_WIDTH]
    wo_c = wo[:, GLA_WIDTH + LRU_WIDTH:]
    w_up = ffn_w_up.astype(BF16)
    w_dn = ffn_w_down.astype(BF16)
    vecs = [rows(v) for v in (norm_mix, lru_conv_b, lru_b_a, lru_b_x, lru_lambda, lru_norm, swa_norm, norm_ffn,
                              ffn_conv_b)]
    g_mix, cb_lru, ba, bx, lam, g_lru, g_swa, g_ffn, cb_ffn = vecs
    g_fin = norm_final.reshape(1, -1)
    lru = (lru_conv_w, cb_lru, wa, ba, wx, bx, lam, g_lru)
    for l in range(depth):
        qe, ke, qi, kd, dec, av, ar, o_b, cq, ck, cv = _inproj(xf, g_mix, w_in_p, wg, bg, cos, sa, sb, lru, seq, l)
        o_a = _gla(qe, ke, qi, kd, dec, av, ar, gla_gn, seq, l)
        o_c = _swa(cq, ck, cv, g_swa, seq, l)
        xf = _ffn(xf, o_a, o_b, o_c, wo_a, wo_b, wo_c, g_ffn, w_up, ffn_conv_w, cb_ffn, w_dn, g_fin, seq, l,
                  l == depth - 1)
    return xf.reshape(batch, seq, D_MODEL)
```

```python
import functools

import jax
import jax.numpy as jnp
import numpy as np
from jax import lax
from jax.experimental import pallas as pl
from jax.experimental.pallas import tpu as pltpu

F32 = jnp.float32
BF16 = jnp.bfloat16

D_MODEL = 1024
EPS = 1e-6
HEAD_DIM = 64
LANES = 128
GLA_HEADS = 4
GLA_DK = 48
GLA_DV = 96
GLA_QK = GLA_HEADS * GLA_DK
GLA_QK_PAD = 256
GLA_WIDTH = GLA_HEADS * GLA_DV
GLA_V_PAD = GLA_HEADS * LANES
GLA_GATE_RANK = 16
GLA_GATE_TEMP = 16.0
GLA_CHUNK = 64
LRU_WIDTH = 256
LRU_BLOCKS = 8
LRU_C = 8.0
LRU_CONV = 4
SWA_HEADS = 6
SWA_WIDTH = SWA_HEADS * HEAD_DIM
SWA_BLOCK = 128
SWA_SLABS = SWA_WIDTH // LANES
SWA_RES = 16
ROPE_THETA = 500000.0
ROPE_DIM = HEAD_DIM // 4
FFN_DIM = 2816
FFN_CONV = 3
FFN_CHUNK = 256
GLR_PAD = 128

_C_AQ = 0
_C_AK = _C_AQ + GLA_QK_PAD
_C_AV = _C_AK + GLA_QK_PAD
_C_AR = _C_AV + GLA_V_PAD
_C_BG = _C_AR + GLA_V_PAD
_C_BI = _C_BG + LRU_WIDTH
_C_CQ = _C_BI + LRU_WIDTH
_C_CK = _C_CQ + SWA_WIDTH
_C_CV = _C_CK + SWA_WIDTH
_C_GLR = _C_CV + SWA_WIDTH
IN_COLS_PACKED = _C_GLR + GLR_PAD

TOKEN_TILE = 512
V7X_VMEM_BYTES = 64 * 1024 * 1024
NEG_BIG = -1e30
LOG2E = float(np.log2(np.e))


def _params(semantics, vmem_mib):
    assert vmem_mib * 1024 * 1024 < V7X_VMEM_BYTES
    return pltpu.CompilerParams(dimension_semantics=semantics,
                                vmem_limit_bytes=vmem_mib * 1024 * 1024)


def _resident(shape, layer=None):
    nd = len(shape)
    if layer is None:
        return pl.BlockSpec(shape, lambda *_: (0,) * nd, pipeline_mode=pl.Buffered(1))
    return pl.BlockSpec((None,) + tuple(shape), lambda *_: (layer,) + (0,) * nd, pipeline_mode=pl.Buffered(1))


def _rms_scale(x):
    return lax.rsqrt(jnp.mean(x * x, axis=-1, keepdims=True) + EPS)


def _gelu_tanh(x):
    c = float(np.sqrt(2.0 / np.pi))
    return x * (0.5 + 0.5 * jnp.tanh(x * (c + (0.044715 * c) * (x * x))))


def _dot(a, b):
    return jnp.dot(a, b, preferred_element_type=F32)


def _dot_nt(a, b):
    return lax.dot_general(a, b, (((1,), (1,)), ((), ())), preferred_element_type=F32)


def _dot_tn(a, b):
    return lax.dot_general(a, b, (((0,), (0,)), ((), ())), preferred_element_type=F32)


def _inproj_kernel(x_ref, gn_ref, w_ref, wg_ref, bg_ref, cos_ref, sa_ref, sb_ref,
                   cw_ref, cb_ref, wa_ref, ba_ref, wx_ref, bx_ref, lam_ref, gnb_ref,
                   qe_ref, ke_ref, qi_ref, kd_ref, dec_ref, av_ref, ar_ref, ob_ref, cq_ref, ck_ref, cv_ref,
                   xtail_ref, hc_ref, rg_ref, *, tiles_per_seq):
    tm = x_ref.shape[0]
    c = GLA_CHUNK

    @pl.when(pl.program_id(0) % tiles_per_seq == 0)
    def _():
        xtail_ref[...] = jnp.zeros_like(xtail_ref)
        hc_ref[...] = jnp.zeros_like(hc_ref)

    x = x_ref[...]
    h = (x * _rms_scale(x) * gn_ref[...]).astype(BF16)

    def proj(c0, width):
        return _dot(h, w_ref[:, c0:c0 + width])

    tile = 256
    sub = c
    row_sub = lax.broadcasted_iota(jnp.int32, (sub, 1), 0)
    silu = lambda t: (t * jax.nn.sigmoid(t)).astype(BF16)

    def scan_steps(update):
        shift = 1
        while shift < sub:
            if shift < 8:
                shifted = lambda v, fill, s=shift: jnp.where(row_sub >= s, pltpu.roll(v, s, 0), fill)
            else:
                shifted = lambda v, fill, s=shift: jnp.concatenate(
                    [jnp.full((s, v.shape[1]), fill, F32), v[:sub - s]], axis=0)
            update(shifted)
            shift *= 2

    p_glr = proj(_C_GLR, GLR_PAD)
    p_bi = proj(_C_BI, LRU_WIDTH)
    z = _dot(p_glr.astype(BF16), wg_ref[...]) + bg_ref[...]
    p_aq = proj(_C_AQ, GLA_QK_PAD)

    ext = jnp.concatenate([xtail_ref[...], p_bi], axis=0)
    xtail_ref[...] = p_bi[tm - 8:tm, :]
    cw = cw_ref[...]
    u = cb_ref[...] + p_bi * cw[LRU_CONV - 1:LRU_CONV, :]
    for kk in range(LRU_CONV - 1):
        u = u + pltpu.roll(ext, LRU_CONV - 1 - kk, 0)[8:, :] * cw[kk:kk + 1, :]
    ub = u.astype(BF16)
    rg_pre = _dot(ub, wa_ref[...]) + ba_ref[...]
    ig_pre = _dot(ub, wx_ref[...]) + bx_ref[...]
    p_ak = proj(_C_AK, GLA_QK_PAD)
    p_bg = proj(_C_BG, LRU_WIDTH)

    def value_tile(k):
        if k == 1:
            av_ref[:, :tile] = proj(_C_AV, tile).astype(BF16)
        elif k == 3:
            av_ref[:, tile:] = proj(_C_AV + tile, tile).astype(BF16)
        elif k == 5:
            ar_ref[:, :tile] = silu(proj(_C_AR, tile))
        elif k == 7:
            ar_ref[:, tile:] = silu(proj(_C_AR + tile, tile))

    for k in range(tm // sub):
        rows = slice(k * sub, (k + 1) * sub)
        zk = z[rows]
        bsum = [(jnp.minimum(zk, 0.0) - jnp.log(1.0 + jnp.exp(-jnp.abs(zk)))) * (1.0 / GLA_GATE_TEMP)]

        def add_back(shifted, bsum=bsum):
            bsum[0] = bsum[0] + shifted(bsum[0], 0.0)

        scan_steps(add_back)
        b = bsum[0]
        b_mid = b[sub // 2 - 1:sub // 2, :]
        b_last = b[sub - 1:sub, :]
        q = p_aq[rows] * (GLA_DK ** -0.5)
        kx = p_ak[rows]
        qe_ref[rows, :] = (q * jnp.exp(b - b_mid)).astype(BF16)
        ke_ref[rows, :] = (kx * jnp.exp(b_mid - b)).astype(BF16)
        qi_ref[rows, :] = (q * jnp.exp(b)).astype(BF16)
        kd_ref[rows, :] = (kx * jnp.exp(b_last - b)).astype(BF16)
        dec_ref[k] = jnp.exp(b_last)
        value_tile(k)

    neg_lam = -lam_ref[...]
    softplus = jnp.maximum(neg_lam, 0.0) + jnp.log(1.0 + jnp.exp(-jnp.abs(neg_lam)))
    carry = hc_ref[...]
    swa_proj = {}
    for k in range(tm // sub):
        rows = slice(k * sub, (k + 1) * sub)
        log_a = (-LRU_C) * jax.nn.sigmoid(rg_pre[rows]) * softplus
        th = jnp.tanh(log_a)
        state = [jnp.exp(log_a), jnp.sqrt(-2.0 * th / (1.0 - th)) * (jax.nn.sigmoid(ig_pre[rows]) * u[rows])]

        def combine(shifted, state=state):
            a, hs = state
            state[1] = hs + a * shifted(hs, 0.0)
            state[0] = a * shifted(a, 1.0)

        scan_steps(combine)
        a, hs = state
        hs = hs + a * carry
        carry = hs[sub - 1:sub, :]
        y = hs * _gelu_tanh(p_bg[rows])
        ob_ref[rows, :] = (y * _rms_scale(y) * gnb_ref[...]).astype(ob_ref.dtype)
        if k % 2 == 1:
            swa_proj[k // 2] = proj(_C_CQ + (k // 2) * tile, tile)
    hc_ref[...] = carry
    p_c0, p_c1, p_c2, p_c3 = (swa_proj[i] for i in range(4))

    cos = cos_ref[...]
    sa = sa_ref[...]
    sb = sb_ref[...]

    def rope(t):
        return t * cos + pltpu.roll(t, LANES - ROPE_DIM // 2, 1) * sa + pltpu.roll(t, ROPE_DIM // 2, 1) * sb

    def regroup(out_ref, s, slot, val):
        rg_ref[slot] = val
        for cc in range(SWA_RES):
            out_ref[cc, s] = rg_ref[slot, pl.ds(cc, tm // SWA_RES, stride=SWA_RES), :]

    q_scale = HEAD_DIM ** -0.5 * LOG2E
    regroup(cq_ref, 0, 0, rope(p_c0[:, :LANES] * q_scale))
    regroup(cq_ref, 1, 1, rope(p_c0[:, LANES:] * q_scale))
    p_c4 = proj(_C_CQ + 4 * tile, LANES)
    regroup(cq_ref, 2, 2, rope(p_c1[:, :LANES] * q_scale))
    regroup(ck_ref, 0, 3, rope(p_c1[:, LANES:]))
    regroup(ck_ref, 1, 4, rope(p_c2[:, :LANES]))
    regroup(ck_ref, 2, 5, rope(p_c2[:, LANES:]))
    regroup(cv_ref, 0, 6, p_c3[:, :LANES])
    regroup(cv_ref, 1, 7, p_c3[:, LANES:])
    regroup(cv_ref, 2, 8, p_c4)


def _inproj(x, gn, w, wg, bg, cos, sa, sb, lru, seq, layer):
    n = x.shape[0]
    tm = TOKEN_TILE
    tiles_per_seq = seq // tm
    row = lambda width: pl.BlockSpec((tm, width), lambda i: (i, 0))
    grouped = pl.BlockSpec((SWA_RES, SWA_SLABS, tm // SWA_RES, LANES), lambda i: (0, 0, i, 0))
    table = pl.BlockSpec((tm, LANES), lambda i: (i % tiles_per_seq, 0))
    vec = _resident((1, LRU_WIDTH), layer)
    mat = _resident((LRU_WIDTH, LRU_WIDTH), layer)
    sds = jax.ShapeDtypeStruct
    grouped_shape = sds((SWA_RES, SWA_SLABS, n // SWA_RES, LANES), F32)
    out_shapes = (
        sds((n, GLA_QK_PAD), BF16), sds((n, GLA_QK_PAD), BF16), sds((n, GLA_QK_PAD), BF16),
        sds((n, GLA_QK_PAD), BF16), sds((n // GLA_CHUNK, 1, GLA_QK_PAD), F32),
        sds((n, GLA_V_PAD), BF16), sds((n, GLA_V_PAD), BF16), sds((n, LRU_WIDTH), BF16),
        grouped_shape, grouped_shape, grouped_shape,
    )
    return pl.pallas_call(
        functools.partial(_inproj_kernel, tiles_per_seq=tiles_per_seq),
        grid=(n // tm,),
        in_specs=[row(D_MODEL), _resident((1, D_MODEL), layer), _resident((D_MODEL, IN_COLS_PACKED), layer),
                  _resident((GLR_PAD, GLA_QK_PAD), layer), _resident((1, GLA_QK_PAD), layer), table, table, table,
                  _resident((LRU_CONV, LRU_WIDTH), layer), vec, mat, vec, mat, vec, vec, vec],
        out_specs=(row(GLA_QK_PAD), row(GLA_QK_PAD), row(GLA_QK_PAD), row(GLA_QK_PAD),
                   pl.BlockSpec((tm // GLA_CHUNK, 1, GLA_QK_PAD), lambda i: (i, 0, 0)),
                   row(GLA_V_PAD), row(GLA_V_PAD), row(LRU_WIDTH), grouped, grouped, grouped),
        out_shape=out_shapes,
        scratch_shapes=[pltpu.VMEM((8, LRU_WIDTH), F32), pltpu.VMEM((1, LRU_WIDTH), F32),
                        pltpu.VMEM((3 * SWA_SLABS, tm, LANES), F32)],
        compiler_params=_params(("arbitrary",), 48),
        name="inproj",
    )(x, gn, w, wg, bg, cos, sa, sb, *lru)


def _gla_kernel(qe_ref, ke_ref, qi_ref, kd_ref, dec_ref, v_ref, r_ref, gn_ref, o_ref, st_ref, *, seq):
    c = GLA_CHUNK
    st_ref[...] = jnp.zeros_like(st_ref)

    lane_q = lax.broadcasted_iota(jnp.int32, (1, GLA_QK_PAD), 1)
    q_masks = [((lane_q >= h * GLA_DK) & (lane_q < (h + 1) * GLA_DK)).astype(BF16) for h in range(GLA_HEADS)]
    ri = lax.broadcasted_iota(jnp.int32, (GLA_HEADS * c, c), 0) & (c - 1)
    ci = lax.broadcasted_iota(jnp.int32, (GLA_HEADS * c, c), 1)
    causal = ci <= ri
    gn = gn_ref[...]

    per_step = 8
    heads = [slice(h * LANES, (h + 1) * LANES) for h in range(GLA_HEADS)]

    def step(i, carry):
        ids = [per_step * i + u for u in range(per_step)]
        rows = [pl.ds(pl.multiple_of(ci * c, c), c) for ci in ids]
        vbs = [v_ref[r, :] for r in rows]
        scores = [_dot_nt(jnp.concatenate([qe_ref[r, :] * m for m in q_masks], axis=0), ke_ref[r, :])
                  for r in rows]
        incs = []
        for r, vb in zip(rows, vbs):
            kd = kd_ref[r, :]
            incs.append([_dot_tn(vb[:, sl], kd * q_masks[h]) for h, sl in enumerate(heads)])
        st = [st_ref[sl, :] for sl in heads]
        for ci, r, vb, s, inc in zip(ids, rows, vbs, scores, incs):
            o_inter = _dot_nt(qi_ref[r, :], jnp.concatenate(st, axis=0).astype(BF16))
            s = jnp.where(causal, s, 0.0).astype(BF16)
            dec = dec_ref[ci]
            gate = r_ref[r, :]
            for h, sl in enumerate(heads):
                o = o_inter[:, sl] + _dot(s[h * c:(h + 1) * c, :], vb[:, sl])
                st[h] = st[h] * dec + inc[h]
                ms = jnp.sum(o * o, axis=-1, keepdims=True) * (1.0 / GLA_DV)
                y = o * lax.rsqrt(ms + EPS) * gn[:, sl] * gate[:, sl].astype(F32)
                o_ref[r, sl] = y.astype(o_ref.dtype)
        for h, sl in enumerate(heads):
            st_ref[sl, :] = st[h]
        return carry

    lax.fori_loop(0, seq // (c * per_step), step, 0)


def _gla(qe, ke, qi, kd, dec, av, ar, gn, seq, layer):
    n = qe.shape[0]
    blk = lambda width: pl.BlockSpec((seq, width), lambda b: (b, 0))
    return pl.pallas_call(
        functools.partial(_gla_kernel, seq=seq),
        grid=(n // seq,),
        in_specs=[blk(GLA_QK_PAD), blk(GLA_QK_PAD), blk(GLA_QK_PAD), blk(GLA_QK_PAD),
                  pl.BlockSpec((seq // GLA_CHUNK, 1, GLA_QK_PAD), lambda b: (b, 0, 0)),
                  blk(GLA_V_PAD), blk(GLA_V_PAD), _resident((1, GLA_V_PAD), layer)],
        out_specs=blk(GLA_V_PAD),
        out_shape=jax.ShapeDtypeStruct((n, GLA_V_PAD), BF16),
        scratch_shapes=[pltpu.VMEM((GLA_V_PAD, GLA_QK_PAD), F32)],
        compiler_params=_params(("parallel",), 40),
        name="gla",
    )(qe, ke, qi, kd, dec, av, ar, gn)


def _swa_kernel(qd_ref, kd_ref, vd_ref, gn_ref, o_ref, acc_ref, m_ref, l_ref, *, seq):
    blk = SWA_BLOCK
    res = SWA_RES
    nj = seq // res
    assert nj == blk
    lane = lax.broadcasted_iota(jnp.int32, (1, LANES), 1)
    head0 = lane < HEAD_DIM

    def permuted_masks(groups):
        q_per = blk // groups
        k_per = 2 * blk // groups
        qi = lax.broadcasted_iota(jnp.int32, (blk, 2 * blk), 0)
        ki = lax.broadcasted_iota(jnp.int32, (blk, 2 * blk), 1)
        tq = (qi % q_per) * groups + qi // q_per
        tk = (ki % k_per) * groups + ki // k_per
        two_blocks = (tk >= tq) & (tk <= tq + blk)
        first_block = tk <= tq
        return two_blocks, first_block

    def attend_all(units):
        both = lambda t: jnp.where(head0, t[:blk], t[blk:])
        scores = []
        for qs, kcat, _, _ in units:
            q2 = jnp.concatenate([jnp.where(head0, qs, 0.0), jnp.where(head0, 0.0, qs)], axis=0)
            scores.append(_dot_nt(q2.astype(BF16), kcat.astype(BF16)))
        soft = []
        for s, (_, _, _, mask) in zip(scores, units):
            s = jnp.where(jnp.concatenate([mask, mask], axis=0), s, NEG_BIG)
            m = jnp.max(s, axis=-1, keepdims=True)
            soft.append((m, jnp.exp2(s - m).astype(BF16)))
        outs = []
        for (m, p), (_, _, vcat, _) in zip(soft, units):
            v_ones = jnp.concatenate([vcat.astype(BF16), jnp.ones(vcat.shape, BF16)], axis=1)
            pv = _dot(p, v_ones)
            outs.append((both(m), both(pv[:, LANES:]), both(pv[:, :LANES])))
        return outs

    def merge(old, new):
        (m_old, l_old, acc_old), (m_new, l_new, pv) = old, new
        m_tot = jnp.maximum(m_old, m_new)
        w_old = jnp.exp2(m_old - m_tot)
        w_new = jnp.exp2(m_new - m_tot)
        return m_tot, w_old * l_old + w_new * l_new, w_old * acc_old + w_new * pv

    qi1 = lax.broadcasted_iota(jnp.int32, (blk, blk), 0)
    kj1 = lax.broadcasted_iota(jnp.int32, (blk, blk), 1)
    causal = kj1 <= qi1

    per_step = 4

    def dil16(i, carry):
        units = [(per_step * i + u, s) for u in range(per_step) for s in range(SWA_SLABS)]
        loaded = [(qd_ref[c, s], kd_ref[c, s], vd_ref[c, s]) for c, s in units]
        results = attend_all([(qs, ks, vs, causal) for qs, ks, vs in loaded])
        for (c, s), (m_new, l_new, pv) in zip(units, results):
            m_ref[c, s] = m_new
            l_ref[c, s] = l_new
            acc_ref[c, s] = pv
        return carry

    lax.fori_loop(0, res // per_step, dil16, 0)

    def dilated(groups):
        n_blocks = nj * groups // blk
        n_res = res // groups
        q_per = blk // groups
        k_per = 2 * blk // groups
        two_blocks, first_block = permuted_masks(groups)
        per_step = 2
        nb_bits = n_blocks.bit_length() - 1
        assert n_blocks == 1 << nb_bits

        def step(i, carry):
            loaded = []
            for u in range(per_step):
                it = per_step * i + u
                r = lax.shift_right_logical(it, nb_bits)
                bi = it & (n_blocks - 1)
                q_rows = pl.ds(pl.multiple_of(bi * q_per, q_per), q_per)
                k_rows = pl.ds(pl.multiple_of(jnp.maximum(bi - 1, 0) * q_per, q_per), k_per)
                mask = (two_blocks & (bi > 0)) | (first_block & (bi == 0))
                gather = lambda ref, s, rows, r=r: jnp.concatenate(
                    [ref[a * n_res + r, s, rows, :] for a in range(groups)], axis=0)
                for s in range(SWA_SLABS):
                    old = (gather(m_ref, s, q_rows), gather(l_ref, s, q_rows), gather(acc_ref, s, q_rows))
                    loaded.append((r, s, q_rows, mask, gather(qd_ref, s, q_rows), gather(kd_ref, s, k_rows),
                                   gather(vd_ref, s, k_rows), old))
            news = attend_all([(qs, kcat, vcat, mask) for _, _, _, mask, qs, kcat, vcat, _ in loaded])
            results = [merge(unit[-1], new) for unit, new in zip(loaded, news)]
            for (r, s, q_rows, *_), (m_new, l_new, acc_new) in zip(loaded, results):
                for a in range(groups):
                    piece = slice(a * q_per, (a + 1) * q_per)
                    m_ref[a * n_res + r, s, q_rows, :] = m_new[piece]
                    l_ref[a * n_res + r, s, q_rows, :] = l_new[piece]
                    acc_ref[a * n_res + r, s, q_rows, :] = acc_new[piece]
            return carry

        lax.fori_loop(0, n_res * n_blocks // per_step, step, 0)

    dilated(4)
    dilated(16)

    def finish(c, carry):
        outs = [acc_ref[c, s] / l_ref[c, s] for s in range(SWA_SLABS)]
        ss = sum(jnp.sum(o * o, axis=-1, keepdims=True) for o in outs)
        scale = lax.rsqrt(ss * (1.0 / SWA_WIDTH) + EPS)
        rows = pl.ds(c, nj, stride=res)
        for s in range(SWA_SLABS):
            o_ref[s, rows, :] = outs[s] * scale * gn_ref[:, s * LANES:(s + 1) * LANES]
        return carry

    lax.fori_loop(0, res, finish, 0)


def _swa(cq, ck, cv, gn, seq, layer):
    n = cq.shape[2] * SWA_RES
    nj = seq // SWA_RES
    grouped = pl.BlockSpec((SWA_RES, SWA_SLABS, nj, LANES), lambda b: (0, 0, b, 0))
    state = pltpu.VMEM((SWA_RES, SWA_SLABS, nj, LANES), F32)
    return pl.pallas_call(
        functools.partial(_swa_kernel, seq=seq),
        grid=(n // seq,),
        in_specs=[grouped, grouped, grouped, _resident((1, SWA_WIDTH), layer)],
        out_specs=pl.BlockSpec((SWA_SLABS, seq, LANES), lambda b: (0, b, 0)),
        out_shape=jax.ShapeDtypeStruct((SWA_SLABS, n, LANES), F32),
        scratch_shapes=[state] * 3,
        compiler_params=_params(("parallel",), 48),
        name="swa",
    )(cq, ck, cv, gn)


def _ffn_kernel(x_ref, oa_ref, ob_ref, oc_ref, woa_ref, wob_ref, woc_ref, gn_ref, wup_ref, cw_ref, cb_ref,
                wdn_ref, gfin_ref, out_ref, carry_ref, act_ref, *, tiles_per_seq, final):
    tm = x_ref.shape[0]

    @pl.when(pl.program_id(0) % tiles_per_seq == 0)
    def _():
        carry_ref[...] = jnp.zeros_like(carry_ref)

    oc = jnp.concatenate([oc_ref[s].astype(BF16) for s in range(SWA_SLABS)], axis=1)
    x1 = x_ref[...] + _dot(oa_ref[...], woa_ref[...]) + _dot(ob_ref[...], wob_ref[...]) + _dot(oc, woc_ref[...])
    h = (x1 * _rms_scale(x1) * gn_ref[...]).astype(BF16)

    def conv_up(c0):
        cols = slice(c0, c0 + FFN_CHUNK)
        up = _dot(h, wup_ref[:, cols])
        ext = jnp.concatenate([carry_ref[:, cols], up], axis=0)
        carry_ref[:, cols] = up[tm - 8:tm, :]
        cw = cw_ref[:, cols]
        return (cb_ref[:, cols] + up * cw[2:3, :] + pltpu.roll(ext, 2, 0)[8:, :] * cw[0:1, :]
                + pltpu.roll(ext, 1, 0)[8:, :] * cw[1:2, :])

    for j in range(FFN_DIM // FFN_CHUNK):
        val = conv_up(j * FFN_CHUNK)
        gate = conv_up(FFN_DIM + j * FFN_CHUNK)
        act_ref[:, j * FFN_CHUNK:(j + 1) * FFN_CHUNK] = (_gelu_tanh(gate) * val).astype(BF16)
    x2 = x1 + _dot(act_ref[...], wdn_ref[...])
    if final:
        x2 = x2 * _rms_scale(x2) * gfin_ref[...]
    out_ref[...] = x2


def _ffn(x, oa, ob, oc, woa, wob, woc, gn, wup, cw, cb, wdn, gfin, seq, layer, final):
    n = x.shape[0]
    tm = TOKEN_TILE
    row = lambda width: pl.BlockSpec((tm, width), lambda i: (i, 0))
    return pl.pallas_call(
        functools.partial(_ffn_kernel, tiles_per_seq=seq // tm, final=final),
        grid=(n // tm,),
        in_specs=[row(D_MODEL), row(GLA_V_PAD), row(LRU_WIDTH),
                  pl.BlockSpec((SWA_SLABS, tm, LANES), lambda i: (0, i, 0)),
                  _resident((GLA_V_PAD, D_MODEL), layer), _resident((LRU_WIDTH, D_MODEL), layer),
                  _resident((SWA_WIDTH, D_MODEL), layer), _resident((1, D_MODEL), layer),
                  _resident((D_MODEL, 2 * FFN_DIM), layer), _resident((FFN_CONV, 2 * FFN_DIM), layer),
                  _resident((1, 2 * FFN_DIM), layer), _resident((FFN_DIM, D_MODEL), layer), _resident((1, D_MODEL))],
        out_specs=row(D_MODEL),
        out_shape=jax.ShapeDtypeStruct((n, D_MODEL), F32),
        scratch_shapes=[pltpu.VMEM((8, 2 * FFN_DIM), F32), pltpu.VMEM((tm, FFN_DIM), BF16)],
        compiler_params=_params(("arbitrary",), 56),
        name="ffn",
    )(x, oa, ob, oc, woa, wob, woc, gn, wup, cw, cb, wdn, gfin)


def _rope_tables(seq):
    half = ROPE_DIM // 2
    inv = ROPE_THETA ** (-jnp.arange(0, ROPE_DIM, 2, dtype=F32) / ROPE_DIM)
    ang = jnp.arange(seq, dtype=F32)[:, None] * inv[None, :]
    cos, sin = jnp.cos(ang), jnp.sin(ang)
    ones = jnp.ones((seq, HEAD_DIM - ROPE_DIM), F32)
    zeros = jnp.zeros((seq, HEAD_DIM - ROPE_DIM), F32)
    zh = jnp.zeros((seq, half), F32)
    per_head = lambda parts: jnp.tile(jnp.concatenate(parts, axis=1), (1, LANES // HEAD_DIM))
    return per_head([cos, cos, ones]), per_head([-sin, zh, zeros]), per_head([zh, sin, zeros])


def _pad_heads(w, axis):
    shape = list(w.shape)
    shape[axis:axis + 1] = [GLA_HEADS, GLA_DV]
    w = w.reshape(shape)
    pad = [(0, 0)] * w.ndim
    pad[axis + 1] = (0, LANES - GLA_DV)
    shape[axis:axis + 2] = [GLA_V_PAD]
    return jnp.pad(w, pad).reshape(shape)


def _pack_w_in(w):
    z = lambda width: jnp.zeros(w.shape[:-1] + (width,), w.dtype)
    o_v = 2 * GLA_QK
    o_r = o_v + GLA_WIDTH
    o_glr = o_r + GLA_WIDTH
    return jnp.concatenate([
        w[..., 0:GLA_QK], z(GLA_QK_PAD - GLA_QK), w[..., GLA_QK:o_v], z(GLA_QK_PAD - GLA_QK),
        _pad_heads(w[..., o_v:o_r], 2), _pad_heads(w[..., o_r:o_glr], 2),
        w[..., o_glr + GLA_GATE_RANK:], w[..., o_glr:o_glr + GLA_GATE_RANK], z(GLR_PAD - GLA_GATE_RANK)],
        axis=-1).astype(BF16)


def _block_diag(w):
    layers, nb, bs, _ = w.shape
    eye = jnp.eye(nb, dtype=w.dtype)
    return jnp.einsum('lnij,nm->lnimj', w, eye).reshape(layers, nb * bs, nb * bs)


def kernel(x, norm_mix, w_in, gla_w_gate, gla_b_gate, gla_norm, lru_conv_w, lru_conv_b, lru_w_a, lru_b_a,
           lru_w_x, lru_b_x, lru_lambda, lru_norm, swa_norm, w_out, norm_ffn, ffn_w_up, ffn_conv_w, ffn_conv_b,
           ffn_w_down, norm_final):
    batch, seq, _ = x.shape
    depth = w_in.shape[0]
    assert seq % TOKEN_TILE == 0 and seq == SWA_BLOCK * SWA_RES
    cos, sa, sb = _rope_tables(seq)
    xf = x.reshape(batch * seq, D_MODEL)
    rows = lambda v: v.reshape(depth, 1, -1)
    wg = jnp.pad(gla_w_gate, ((0, 0), (0, GLR_PAD - GLA_GATE_RANK), (0, GLA_QK_PAD - GLA_QK))).astype(BF16)
    bg = rows(jnp.pad(gla_b_gate, ((0, 0), (0, GLA_QK_PAD - GLA_QK))))
    w_in_p = _pack_w_in(w_in)
    gla_gn = rows(_pad_heads(jnp.tile(gla_norm, (1, GLA_HEADS)), 1))
    wa = _block_diag(lru_w_a).astype(BF16)
    wx = _block_diag(lru_w_x).astype(BF16)
    wo = w_out.astype(BF16)
    wo_a = _pad_heads(wo[:, :GLA_WIDTH], 1)
    wo_b = wo[:, GLA_WIDTH:GLA_WIDTH + LRU_WIDTH]
    wo_c = wo[:, GLA_WIDTH + LRU_WIDTH:]
    w_up = ffn_w_up.astype(BF16)
    w_dn = ffn_w_down.astype(BF16)
    vecs = [rows(v) for v in (norm_mix, lru_conv_b, lru_b_a, lru_b_x, lru_lambda, lru_norm, swa_norm, norm_ffn,
                              ffn_conv_b)]
    g_mix, cb_lru, ba, bx, lam, g_lru, g_swa, g_ffn, cb_ffn = vecs
    g_fin = norm_final.reshape(1, -1)
    lru = (lru_conv_w, cb_lru, wa, ba, wx, bx, lam, g_lru)
    for l in range(depth):
        qe, ke, qi, kd, dec, av, ar, o_b, cq, ck, cv = _inproj(xf, g_mix, w_in_p, wg, bg, cos, sa, sb, lru, seq, l)
        o_a = _gla(qe, ke, qi, kd, dec, av, ar, gla_gn, seq, l)
        o_c = _swa(cq, ck, cv, g_swa, seq, l)
        xf = _ffn(xf, o_a, o_b, o_c, wo_a, wo_b, wo_c, g_ffn, w_up, ffn_conv_w, cb_ffn, w_dn, g_fin, seq, l,
                  l == depth - 1)
    return xf.reshape(batch, seq, D_MODEL)
```

```python
import functools

import jax
import jax.numpy as jnp
import numpy as np
from jax import lax
from jax.experimental import pallas as pl
from jax.experimental.pallas import tpu as pltpu

F32 = jnp.float32
BF16 = jnp.bfloat16

D_MODEL = 1024
EPS = 1e-6
HEAD_DIM = 64
LANES = 128
GLA_HEADS = 4
GLA_DK = 48
GLA_DV = 96
GLA_QK = GLA_HEADS * GLA_DK
GLA_QK_PAD = 256
GLA_WIDTH = GLA_HEADS * GLA_DV
GLA_V_PAD = GLA_HEADS * LANES
GLA_GATE_RANK = 16
GLA_GATE_TEMP = 16.0
GLA_CHUNK = 64
LRU_WIDTH = 256
LRU_BLOCKS = 8
LRU_C = 8.0
LRU_CONV = 4
SWA_HEADS = 6
SWA_WIDTH = SWA_HEADS * HEAD_DIM
SWA_BLOCK = 128
SWA_SLABS = SWA_WIDTH // LANES
SWA_RES = 16
ROPE_THETA = 500000.0
ROPE_DIM = HEAD_DIM // 4
FFN_DIM = 2816
FFN_CONV = 3
FFN_CHUNK = 256
GLR_PAD = 128

_C_AQ = 0
_C_AK = _C_AQ + GLA_QK_PAD
_C_AV = _C_AK + GLA_QK_PAD
_C_AR = _C_AV + GLA_V_PAD
_C_BG = _C_AR + GLA_V_PAD
_C_BI = _C_BG + LRU_WIDTH
_C_CQ = _C_BI + LRU_WIDTH
_C_CK = _C_CQ + SWA_WIDTH
_C_CV = _C_CK + SWA_WIDTH
_C_GLR = _C_CV + SWA_WIDTH
IN_COLS_PACKED = _C_GLR + GLR_PAD

TOKEN_TILE = 512
V7X_VMEM_BYTES = 64 * 1024 * 1024
NEG_BIG = -1e30
LOG2E = float(np.log2(np.e))


def _params(semantics, vmem_mib):
    assert vmem_mib * 1024 * 1024 < V7X_VMEM_BYTES
    return pltpu.CompilerParams(dimension_semantics=semantics,
                                vmem_limit_bytes=vmem_mib * 1024 * 1024)


def _resident(shape, layer=None):
    nd = len(shape)
    if layer is None:
        return pl.BlockSpec(shape, lambda *_: (0,) * nd, pipeline_mode=pl.Buffered(1))
    return pl.BlockSpec((None,) + tuple(shape), lambda *_: (layer,) + (0,) * nd, pipeline_mode=pl.Buffered(1))


def _rms_scale(x):
    return lax.rsqrt(jnp.mean(x * x, axis=-1, keepdims=True) + EPS)


def _gelu_tanh(x):
    c = float(np.sqrt(2.0 / np.pi))
    return x * (0.5 + 0.5 * jnp.tanh(x * (c + (0.044715 * c) * (x * x))))


def _dot(a, b):
    return jnp.dot(a, b, preferred_element_type=F32)


def _dot_nt(a, b):
    return lax.dot_general(a, b, (((1,), (1,)), ((), ())), preferred_element_type=F32)


def _dot_tn(a, b):
    return lax.dot_general(a, b, (((0,), (0,)), ((), ())), preferred_element_type=F32)


def _inproj_kernel(x_ref, gn_ref, w_ref, wg_ref, bg_ref, cos_ref, sa_ref, sb_ref,
                   cw_ref, cb_ref, wa_ref, ba_ref, wx_ref, bx_ref, lam_ref, gnb_ref,
                   qe_ref, ke_ref, qi_ref, kd_ref, dec_ref, av_ref, ar_ref, ob_ref, cq_ref, ck_ref, cv_ref,
                   xtail_ref, hc_ref, rg_ref, rg4_ref, *, tiles_per_seq):
    tm = x_ref.shape[0]
    c = GLA_CHUNK

    @pl.when(pl.program_id(0) % tiles_per_seq == 0)
    def _():
        xtail_ref[...] = jnp.zeros_like(xtail_ref)
        hc_ref[...] = jnp.zeros_like(hc_ref)

    x = x_ref[...]
    h = (x * _rms_scale(x) * gn_ref[...]).astype(BF16)

    def proj(c0, width):
        return _dot(h, w_ref[:, c0:c0 + width])

    tile = 256
    sub = c
    row_sub = lax.broadcasted_iota(jnp.int32, (sub, 1), 0)
    silu = lambda t: (t * jax.nn.sigmoid(t)).astype(BF16)

    def scan_steps(update):
        shift = 1
        while shift < sub:
            if shift < 8:
                shifted = lambda v, fill, s=shift: jnp.where(row_sub >= s, pltpu.roll(v, s, 0), fill)
            else:
                shifted = lambda v, fill, s=shift: jnp.concatenate(
                    [jnp.full((s, v.shape[1]), fill, F32), v[:sub - s]], axis=0)
            update(shifted)
            shift *= 2

    p_glr = proj(_C_GLR, GLR_PAD)
    p_bi = proj(_C_BI, LRU_WIDTH)
    z = _dot(p_glr.astype(BF16), wg_ref[...]) + bg_ref[...]
    p_aq = proj(_C_AQ, GLA_QK_PAD)

    ext = jnp.concatenate([xtail_ref[...], p_bi], axis=0)
    xtail_ref[...] = p_bi[tm - 8:tm, :]
    cw = cw_ref[...]
    u = cb_ref[...] + p_bi * cw[LRU_CONV - 1:LRU_CONV, :]
    for kk in range(LRU_CONV - 1):
        u = u + pltpu.roll(ext, LRU_CONV - 1 - kk, 0)[8:, :] * cw[kk:kk + 1, :]
    ub = u.astype(BF16)
    rg_pre = _dot(ub, wa_ref[...]) + ba_ref[...]
    ig_pre = _dot(ub, wx_ref[...]) + bx_ref[...]
    p_ak = proj(_C_AK, GLA_QK_PAD)
    p_bg = proj(_C_BG, LRU_WIDTH)

    def value_tile(k):
        if k == 1:
            av_ref[:, :tile] = proj(_C_AV, tile).astype(BF16)
        elif k == 3:
            av_ref[:, tile:] = proj(_C_AV + tile, tile).astype(BF16)
        elif k == 5:
            ar_ref[:, :tile] = silu(proj(_C_AR, tile))
        elif k == 7:
            ar_ref[:, tile:] = silu(proj(_C_AR + tile, tile))

    for k in range(tm // sub):
        rows = slice(k * sub, (k + 1) * sub)
        zk = z[rows]
        bsum = [(jnp.minimum(zk, 0.0) - jnp.log(1.0 + jnp.exp(-jnp.abs(zk)))) * (1.0 / GLA_GATE_TEMP)]

        def add_back(shifted, bsum=bsum):
            bsum[0] = bsum[0] + shifted(bsum[0], 0.0)

        scan_steps(add_back)
        b = bsum[0]
        b_mid = b[sub // 2 - 1:sub // 2, :]
        b_last = b[sub - 1:sub, :]
        q = p_aq[rows] * (GLA_DK ** -0.5)
        kx = p_ak[rows]
        qe_ref[rows, :] = (q * jnp.exp(b - b_mid)).astype(BF16)
        ke_ref[rows, :] = (kx * jnp.exp(b_mid - b)).astype(BF16)
        qi_ref[rows, :] = (q * jnp.exp(b)).astype(BF16)
        kd_ref[rows, :] = (kx * jnp.exp(b_last - b)).astype(BF16)
        dec_ref[k] = jnp.exp(b_last)
        value_tile(k)

    neg_lam = -lam_ref[...]
    softplus = jnp.maximum(neg_lam, 0.0) + jnp.log(1.0 + jnp.exp(-jnp.abs(neg_lam)))
    carry = hc_ref[...]
    swa_proj = {}
    for k in range(tm // sub):
        rows = slice(k * sub, (k + 1) * sub)
        log_a = (-LRU_C) * jax.nn.sigmoid(rg_pre[rows]) * softplus
        th = jnp.tanh(log_a)
        state = [jnp.exp(log_a), jnp.sqrt(-2.0 * th / (1.0 - th)) * (jax.nn.sigmoid(ig_pre[rows]) * u[rows])]

        def combine(shifted, state=state):
            a, hs = state
            state[1] = hs + a * shifted(hs, 0.0)
            state[0] = a * shifted(a, 1.0)

        scan_steps(combine)
        a, hs = state
        hs = hs + a * carry
        carry = hs[sub - 1:sub, :]
        y = hs * _gelu_tanh(p_bg[rows])
        ob_ref[rows, :] = (y * _rms_scale(y) * gnb_ref[...]).astype(ob_ref.dtype)
        if k % 2 == 1:
            swa_proj[k // 2] = proj(_C_CQ + (k // 2) * tile, tile)
    hc_ref[...] = carry
    p_c0, p_c1, p_c2, p_c3 = (swa_proj[i] for i in range(4))

    cos = cos_ref[...]
    sa = sa_ref[...]
    sb = sb_ref[...]

    def rope(t):
        return t * cos + pltpu.roll(t, LANES - ROPE_DIM // 2, 1) * sa + pltpu.roll(t, ROPE_DIM // 2, 1) * sb

    def regroup(out_ref, s, slot, val):
        rg_ref[slot] = val
        for a in range(4):
            rg4_ref[slot, a] = rg_ref[slot, pl.ds(a, tm // 4, stride=4), :]
        for a in range(4):
            for b in range(4):
                out_ref[4 * b + a, s] = rg4_ref[slot, a, pl.ds(b, tm // SWA_RES, stride=4), :]

    q_scale = HEAD_DIM ** -0.5 * LOG2E
    regroup(cq_ref, 0, 0, rope(p_c0[:, :LANES] * q_scale))
    regroup(cq_ref, 1, 1, rope(p_c0[:, LANES:] * q_scale))
    p_c4 = proj(_C_CQ + 4 * tile, LANES)
    regroup(cq_ref, 2, 2, rope(p_c1[:, :LANES] * q_scale))
    regroup(ck_ref, 0, 3, rope(p_c1[:, LANES:]))
    regroup(ck_ref, 1, 4, rope(p_c2[:, :LANES]))
    regroup(ck_ref, 2, 5, rope(p_c2[:, LANES:]))
    regroup(cv_ref, 0, 6, p_c3[:, :LANES])
    regroup(cv_ref, 1, 7, p_c3[:, LANES:])
    regroup(cv_ref, 2, 8, p_c4)


def _inproj(x, gn, w, wg, bg, cos, sa, sb, lru, seq, layer):
    n = x.shape[0]
    tm = TOKEN_TILE
    tiles_per_seq = seq // tm
    row = lambda width: pl.BlockSpec((tm, width), lambda i: (i, 0))
    grouped = pl.BlockSpec((SWA_RES, SWA_SLABS, tm // SWA_RES, LANES), lambda i: (0, 0, i, 0))
    table = pl.BlockSpec((tm, LANES), lambda i: (i % tiles_per_seq, 0))
    vec = _resident((1, LRU_WIDTH), layer)
    mat = _resident((LRU_WIDTH, LRU_WIDTH), layer)
    sds = jax.ShapeDtypeStruct
    grouped_shape = sds((SWA_RES, SWA_SLABS, n // SWA_RES, LANES), F32)
    out_shapes = (
        sds((n, GLA_QK_PAD), BF16), sds((n, GLA_QK_PAD), BF16), sds((n, GLA_QK_PAD), BF16),
        sds((n, GLA_QK_PAD), BF16), sds((n // GLA_CHUNK, 1, GLA_QK_PAD), F32),
        sds((n, GLA_V_PAD), BF16), sds((n, GLA_V_PAD), BF16), sds((n, LRU_WIDTH), BF16),
        grouped_shape, grouped_shape, grouped_shape,
    )
    return pl.pallas_call(
        functools.partial(_inproj_kernel, tiles_per_seq=tiles_per_seq),
        grid=(n // tm,),
        in_specs=[row(D_MODEL), _resident((1, D_MODEL), layer), _resident((D_MODEL, IN_COLS_PACKED), layer),
                  _resident((GLR_PAD, GLA_QK_PAD), layer), _resident((1, GLA_QK_PAD), layer), table, table, table,
                  _resident((LRU_CONV, LRU_WIDTH), layer), vec, mat, vec, mat, vec, vec, vec],
        out_specs=(row(GLA_QK_PAD), row(GLA_QK_PAD), row(GLA_QK_PAD), row(GLA_QK_PAD),
                   pl.BlockSpec((tm // GLA_CHUNK, 1, GLA_QK_PAD), lambda i: (i, 0, 0)),
                   row(GLA_V_PAD), row(GLA_V_PAD), row(LRU_WIDTH), grouped, grouped, grouped),
        out_shape=out_shapes,
        scratch_shapes=[pltpu.VMEM((8, LRU_WIDTH), F32), pltpu.VMEM((1, LRU_WIDTH), F32),
                        pltpu.VMEM((3 * SWA_SLABS, tm, LANES), F32),
                        pltpu.VMEM((3 * SWA_SLABS, 4, tm // 4, LANES), F32)],
        compiler_params=_params(("arbitrary",), 48),
        name="inproj",
    )(x, gn, w, wg, bg, cos, sa, sb, *lru)


def _gla_kernel(qe_ref, ke_ref, qi_ref, kd_ref, dec_ref, v_ref, r_ref, gn_ref, o_ref, st_ref, *, seq):
    c = GLA_CHUNK
    st_ref[...] = jnp.zeros_like(st_ref)

    lane_q = lax.broadcasted_iota(jnp.int32, (1, GLA_QK_PAD), 1)
    q_masks = [((lane_q >= h * GLA_DK) & (lane_q < (h + 1) * GLA_DK)).astype(BF16) for h in range(GLA_HEADS)]
    ri = lax.broadcasted_iota(jnp.int32, (GLA_HEADS * c, c), 0) & (c - 1)
    ci = lax.broadcasted_iota(jnp.int32, (GLA_HEADS * c, c), 1)
    causal = ci <= ri
    gn = gn_ref[...]

    per_step = 8
    heads = [slice(h * LANES, (h + 1) * LANES) for h in range(GLA_HEADS)]

    def step(i, carry):
        ids = [per_step * i + u for u in range(per_step)]
        rows = [pl.ds(pl.multiple_of(ci * c, c), c) for ci in ids]
        vbs = [v_ref[r, :] for r in rows]
        scores = [_dot_nt(jnp.concatenate([qe_ref[r, :] * m for m in q_masks], axis=0), ke_ref[r, :])
                  for r in rows]
        incs = []
        for r, vb in zip(rows, vbs):
            kd = kd_ref[r, :]
            incs.append([_dot_tn(vb[:, sl], kd * q_masks[h]) for h, sl in enumerate(heads)])
        st = [st_ref[sl, :] for sl in heads]
        for ci, r, vb, s, inc in zip(ids, rows, vbs, scores, incs):
            o_inter = _dot_nt(qi_ref[r, :], jnp.concatenate(st, axis=0).astype(BF16))
            s = jnp.where(causal, s, 0.0).astype(BF16)
            dec = dec_ref[ci]
            gate = r_ref[r, :]
            for h, sl in enumerate(heads):
                o = o_inter[:, sl] + _dot(s[h * c:(h + 1) * c, :], vb[:, sl])
                st[h] = st[h] * dec + inc[h]
                ms = jnp.sum(o * o, axis=-1, keepdims=True) * (1.0 / GLA_DV)
                y = o * lax.rsqrt(ms + EPS) * gn[:, sl] * gate[:, sl].astype(F32)
                o_ref[r, sl] = y.astype(o_ref.dtype)
        for h, sl in enumerate(heads):
            st_ref[sl, :] = st[h]
        return carry

    lax.fori_loop(0, seq // (c * per_step), step, 0)


def _gla(qe, ke, qi, kd, dec, av, ar, gn, seq, layer):
    n = qe.shape[0]
    blk = lambda width: pl.BlockSpec((seq, width), lambda b: (b, 0))
    return pl.pallas_call(
        functools.partial(_gla_kernel, seq=seq),
        grid=(n // seq,),
        in_specs=[blk(GLA_QK_PAD), blk(GLA_QK_PAD), blk(GLA_QK_PAD), blk(GLA_QK_PAD),
                  pl.BlockSpec((seq // GLA_CHUNK, 1, GLA_QK_PAD), lambda b: (b, 0, 0)),
                  blk(GLA_V_PAD), blk(GLA_V_PAD), _resident((1, GLA_V_PAD), layer)],
        out_specs=blk(GLA_V_PAD),
        out_shape=jax.ShapeDtypeStruct((n, GLA_V_PAD), BF16),
        scratch_shapes=[pltpu.VMEM((GLA_V_PAD, GLA_QK_PAD), F32)],
        compiler_params=_params(("parallel",), 40),
        name="gla",
    )(qe, ke, qi, kd, dec, av, ar, gn)


def _swa_kernel(qd_ref, kd_ref, vd_ref, gn_ref, o_ref, acc_ref, m_ref, l_ref, out4_ref, *, seq):
    blk = SWA_BLOCK
    res = SWA_RES
    nj = seq // res
    assert nj == blk
    lane = lax.broadcasted_iota(jnp.int32, (1, LANES), 1)
    head0 = lane < HEAD_DIM

    def permuted_masks(groups):
        q_per = blk // groups
        k_per = 2 * blk // groups
        qi = lax.broadcasted_iota(jnp.int32, (blk, 2 * blk), 0)
        ki = lax.broadcasted_iota(jnp.int32, (blk, 2 * blk), 1)
        tq = (qi % q_per) * groups + qi // q_per
        tk = (ki % k_per) * groups + ki // k_per
        two_blocks = (tk >= tq) & (tk <= tq + blk)
        first_block = tk <= tq
        return two_blocks, first_block

    def attend_all(units):
        both = lambda t: jnp.where(head0, t[:blk], t[blk:])
        scores = []
        for qs, kcat, _, _ in units:
            q2 = jnp.concatenate([jnp.where(head0, qs, 0.0), jnp.where(head0, 0.0, qs)], axis=0)
            scores.append(_dot_nt(q2.astype(BF16), kcat.astype(BF16)))
        soft = []
        for s, (_, _, _, mask) in zip(scores, units):
            s = jnp.where(jnp.concatenate([mask, mask], axis=0), s, NEG_BIG)
            m = jnp.max(s, axis=-1, keepdims=True)
            soft.append((m, jnp.exp2(s - m).astype(BF16)))
        outs = []
        for (m, p), (_, _, vcat, _) in zip(soft, units):
            v_ones = jnp.concatenate([vcat.astype(BF16), jnp.ones(vcat.shape, BF16)], axis=1)
            pv = _dot(p, v_ones)
            outs.append((both(m), both(pv[:, LANES:]), both(pv[:, :LANES])))
        return outs

    def merge(old, new):
        (m_old, l_old, acc_old), (m_new, l_new, pv) = old, new
        m_tot = jnp.maximum(m_old, m_new)
        w_old = jnp.exp2(m_old - m_tot)
        w_new = jnp.exp2(m_new - m_tot)
        return m_tot, w_old * l_old + w_new * l_new, w_old * acc_old + w_new * pv

    qi1 = lax.broadcasted_iota(jnp.int32, (blk, blk), 0)
    kj1 = lax.broadcasted_iota(jnp.int32, (blk, blk), 1)
    causal = kj1 <= qi1

    per_step = 4

    def dil16(i, carry):
        units = [(per_step * i + u, s) for u in range(per_step) for s in range(SWA_SLABS)]
        loaded = [(qd_ref[c, s], kd_ref[c, s], vd_ref[c, s]) for c, s in units]
        results = attend_all([(qs, ks, vs, causal) for qs, ks, vs in loaded])
        for (c, s), (m_new, l_new, pv) in zip(units, results):
            m_ref[c, s] = m_new
            l_ref[c, s] = l_new
            acc_ref[c, s] = pv
        return carry

    lax.fori_loop(0, res // per_step, dil16, 0)

    def dilated(groups):
        n_blocks = nj * groups // blk
        n_res = res // groups
        q_per = blk // groups
        k_per = 2 * blk // groups
        two_blocks, first_block = permuted_masks(groups)
        per_step = 2
        nb_bits = n_blocks.bit_length() - 1
        assert n_blocks == 1 << nb_bits

        def step(i, carry):
            loaded = []
            for u in range(per_step):
                it = per_step * i + u
                r = lax.shift_right_logical(it, nb_bits)
                bi = it & (n_blocks - 1)
                q_rows = pl.ds(pl.multiple_of(bi * q_per, q_per), q_per)
                k_rows = pl.ds(pl.multiple_of(jnp.maximum(bi - 1, 0) * q_per, q_per), k_per)
                mask = (two_blocks & (bi > 0)) | (first_block & (bi == 0))
                gather = lambda ref, s, rows, r=r: jnp.concatenate(
                    [ref[a * n_res + r, s, rows, :] for a in range(groups)], axis=0)
                for s in range(SWA_SLABS):
                    old = (gather(m_ref, s, q_rows), gather(l_ref, s, q_rows), gather(acc_ref, s, q_rows))
                    loaded.append((r, s, q_rows, mask, gather(qd_ref, s, q_rows), gather(kd_ref, s, k_rows),
                                   gather(vd_ref, s, k_rows), old))
            news = attend_all([(qs, kcat, vcat, mask) for _, _, _, mask, qs, kcat, vcat, _ in loaded])
            results = [merge(unit[-1], new) for unit, new in zip(loaded, news)]
            for (r, s, q_rows, *_), (m_new, l_new, acc_new) in zip(loaded, results):
                for a in range(groups):
                    piece = slice(a * q_per, (a + 1) * q_per)
                    m_ref[a * n_res + r, s, q_rows, :] = m_new[piece]
                    l_ref[a * n_res + r, s, q_rows, :] = l_new[piece]
                    acc_ref[a * n_res + r, s, q_rows, :] = acc_new[piece]
            return carry

        lax.fori_loop(0, n_res * n_blocks // per_step, step, 0)

    dilated(4)
    dilated(16)

    def finish(a, carry):
        for b in range(4):
            c = 4 * b + a
            outs = [acc_ref[c, s] / l_ref[c, s] for s in range(SWA_SLABS)]
            ss = sum(jnp.sum(o * o, axis=-1, keepdims=True) for o in outs)
            scale = lax.rsqrt(ss * (1.0 / SWA_WIDTH) + EPS)
            for s in range(SWA_SLABS):
                y = outs[s] * scale * gn_ref[:, s * LANES:(s + 1) * LANES]
                out4_ref[s, pl.ds(b, nj, stride=4), :] = y
        for s in range(SWA_SLABS):
            o_ref[s, pl.ds(a, 4 * nj, stride=4), :] = out4_ref[s]
        return carry

    lax.fori_loop(0, 4, finish, 0)


def _swa(cq, ck, cv, gn, seq, layer):
    n = cq.shape[2] * SWA_RES
    nj = seq // SWA_RES
    grouped = pl.BlockSpec((SWA_RES, SWA_SLABS, nj, LANES), lambda b: (0, 0, b, 0))
    state = pltpu.VMEM((SWA_RES, SWA_SLABS, nj, LANES), F32)
    return pl.pallas_call(
        functools.partial(_swa_kernel, seq=seq),
        grid=(n // seq,),
        in_specs=[grouped, grouped, grouped, _resident((1, SWA_WIDTH), layer)],
        out_specs=pl.BlockSpec((SWA_SLABS, seq, LANES), lambda b: (0, b, 0)),
        out_shape=jax.ShapeDtypeStruct((SWA_SLABS, n, LANES), F32),
        scratch_shapes=[state] * 3 + [pltpu.VMEM((SWA_SLABS, 4 * nj, LANES), F32)],
        compiler_params=_params(("parallel",), 48),
        name="swa",
    )(cq, ck, cv, gn)


def _ffn_kernel(x_ref, oa_ref, ob_ref, oc_ref, woa_ref, wob_ref, woc_ref, gn_ref, wup_ref, cw_ref, cb_ref,
                wdn_ref, gfin_ref, out_ref, carry_ref, act_ref, *, tiles_per_seq, final):
    tm = x_ref.shape[0]

    @pl.when(pl.program_id(0) % tiles_per_seq == 0)
    def _():
        carry_ref[...] = jnp.zeros_like(carry_ref)

    oc = jnp.concatenate([oc_ref[s].astype(BF16) for s in range(SWA_SLABS)], axis=1)
    x1 = x_ref[...] + _dot(oa_ref[...], woa_ref[...]) + _dot(ob_ref[...], wob_ref[...]) + _dot(oc, woc_ref[...])
    h = (x1 * _rms_scale(x1) * gn_ref[...]).astype(BF16)

    def conv_up(c0):
        cols = slice(c0, c0 + FFN_CHUNK)
        up = _dot(h, wup_ref[:, cols])
        ext = jnp.concatenate([carry_ref[:, cols], up], axis=0)
        carry_ref[:, cols] = up[tm - 8:tm, :]
        cw = cw_ref[:, cols]
        return (cb_ref[:, cols] + up * cw[2:3, :] + pltpu.roll(ext, 2, 0)[8:, :] * cw[0:1, :]
                + pltpu.roll(ext, 1, 0)[8:, :] * cw[1:2, :])

    for j in range(FFN_DIM // FFN_CHUNK):
        val = conv_up(j * FFN_CHUNK)
        gate = conv_up(FFN_DIM + j * FFN_CHUNK)
        act_ref[:, j * FFN_CHUNK:(j + 1) * FFN_CHUNK] = (_gelu_tanh(gate) * val).astype(BF16)
    x2 = x1 + _dot(act_ref[...], wdn_ref[...])
    if final:
        x2 = x2 * _rms_scale(x2) * gfin_ref[...]
    out_ref[...] = x2


def _ffn(x, oa, ob, oc, woa, wob, woc, gn, wup, cw, cb, wdn, gfin, seq, layer, final):
    n = x.shape[0]
    tm = TOKEN_TILE
    row = lambda width: pl.BlockSpec((tm, width), lambda i: (i, 0))
    return pl.pallas_call(
        functools.partial(_ffn_kernel, tiles_per_seq=seq // tm, final=final),
        grid=(n // tm,),
        in_specs=[row(D_MODEL), row(GLA_V_PAD), row(LRU_WIDTH),
                  pl.BlockSpec((SWA_SLABS, tm, LANES), lambda i: (0, i, 0)),
                  _resident((GLA_V_PAD, D_MODEL), layer), _resident((LRU_WIDTH, D_MODEL), layer),
                  _resident((SWA_WIDTH, D_MODEL), layer), _resident((1, D_MODEL), layer),
                  _resident((D_MODEL, 2 * FFN_DIM), layer), _resident((FFN_CONV, 2 * FFN_DIM), layer),
                  _resident((1, 2 * FFN_DIM), layer), _resident((FFN_DIM, D_MODEL), layer), _resident((1, D_MODEL))],
        out_specs=row(D_MODEL),
        out_shape=jax.ShapeDtypeStruct((n, D_MODEL), F32),
        scratch_shapes=[pltpu.VMEM((8, 2 * FFN_DIM), F32), pltpu.VMEM((tm, FFN_DIM), BF16)],
        compiler_params=_params(("arbitrary",), 56),
        name="ffn",
    )(x, oa, ob, oc, woa, wob, woc, gn, wup, cw, cb, wdn, gfin)


def _rope_tables(seq):
    half = ROPE_DIM // 2
    inv = ROPE_THETA ** (-jnp.arange(0, ROPE_DIM, 2, dtype=F32) / ROPE_DIM)
    ang = jnp.arange(seq, dtype=F32)[:, None] * inv[None, :]
    cos, sin = jnp.cos(ang), jnp.sin(ang)
    ones = jnp.ones((seq, HEAD_DIM - ROPE_DIM), F32)
    zeros = jnp.zeros((seq, HEAD_DIM - ROPE_DIM), F32)
    zh = jnp.zeros((seq, half), F32)
    per_head = lambda parts: jnp.tile(jnp.concatenate(parts, axis=1), (1, LANES // HEAD_DIM))
    return per_head([cos, cos, ones]), per_head([-sin, zh, zeros]), per_head([zh, sin, zeros])


def _pad_heads(w, axis):
    shape = list(w.shape)
    shape[axis:axis + 1] = [GLA_HEADS, GLA_DV]
    w = w.reshape(shape)
    pad = [(0, 0)] * w.ndim
    pad[axis + 1] = (0, LANES - GLA_DV)
    shape[axis:axis + 2] = [GLA_V_PAD]
    return jnp.pad(w, pad).reshape(shape)


def _pack_w_in(w):
    z = lambda width: jnp.zeros(w.shape[:-1] + (width,), w.dtype)
    o_v = 2 * GLA_QK
    o_r = o_v + GLA_WIDTH
    o_glr = o_r + GLA_WIDTH
    return jnp.concatenate([
        w[..., 0:GLA_QK], z(GLA_QK_PAD - GLA_QK), w[..., GLA_QK:o_v], z(GLA_QK_PAD - GLA_QK),
        _pad_heads(w[..., o_v:o_r], 2), _pad_heads(w[..., o_r:o_glr], 2),
        w[..., o_glr + GLA_GATE_RANK:], w[..., o_glr:o_glr + GLA_GATE_RANK], z(GLR_PAD - GLA_GATE_RANK)],
        axis=-1).astype(BF16)


def _block_diag(w):
    layers, nb, bs, _ = w.shape
    eye = jnp.eye(nb, dtype=w.dtype)
    return jnp.einsum('lnij,nm->lnimj', w, eye).reshape(layers, nb * bs, nb * bs)


def kernel(x, norm_mix, w_in, gla_w_gate, gla_b_gate, gla_norm, lru_conv_w, lru_conv_b, lru_w_a, lru_b_a,
           lru_w_x, lru_b_x, lru_lambda, lru_norm, swa_norm, w_out, norm_ffn, ffn_w_up, ffn_conv_w, ffn_conv_b,
           ffn_w_down, norm_final):
    batch, seq, _ = x.shape
    depth = w_in.shape[0]
    assert seq % TOKEN_TILE == 0 and seq == SWA_BLOCK * SWA_RES
    cos, sa, sb = _rope_tables(seq)
    xf = x.reshape(batch * seq, D_MODEL)
    rows = lambda v: v.reshape(depth, 1, -1)
    wg = jnp.pad(gla_w_gate, ((0, 0), (0, GLR_PAD - GLA_GATE_RANK), (0, GLA_QK_PAD - GLA_QK))).astype(BF16)
    bg = rows(jnp.pad(gla_b_gate, ((0, 0), (0, GLA_QK_PAD - GLA_QK))))
    w_in_p = _pack_w_in(w_in)
    gla_gn = rows(_pad_heads(jnp.tile(gla_norm, (1, GLA_HEADS)), 1))
    wa = _block_diag(lru_w_a).astype(BF16)
    wx = _block_diag(lru_w_x).astype(BF16)
    wo = w_out.astype(BF16)
    wo_a = _pad_heads(wo[:, :GLA_WIDTH], 1)
    wo_b = wo[:, GLA_WIDTH:GLA_WIDTH + LRU_WIDTH]
    wo_c = wo[:, GLA_WIDTH + LRU_WIDTH:]
    w_up = ffn_w_up.astype(BF16)
    w_dn = ffn_w_down.astype(BF16)
    vecs = [rows(v) for v in (norm_mix, lru_conv_b, lru_b_a, lru_b_x, lru_lambda, lru_norm, swa_norm, norm_ffn,
                              ffn_conv_b)]
    g_mix, cb_lru, ba, bx, lam, g_lru, g_swa, g_ffn, cb_ffn = vecs
    g_fin = norm_final.reshape(1, -1)
    lru = (lru_conv_w, cb_lru, wa, ba, wx, bx, lam, g_lru)
    for l in range(depth):
        qe, ke, qi, kd, dec, av, ar, o_b, cq, ck, cv = _inproj(xf, g_mix, w_in_p, wg, bg, cos, sa, sb, lru, seq, l)
        o_a = _gla(qe, ke, qi, kd, dec, av, ar, gla_gn, seq, l)
        o_c = _swa(cq, ck, cv, g_swa, seq, l)
        xf = _ffn(xf, o_a, o_b, o_c, wo_a, wo_b, wo_c, g_ffn, w_up, ffn_conv_w, cb_ffn, w_dn, g_fin, seq, l,
                  l == depth - 1)
    return xf.reshape(batch, seq, D_MODEL)
```

```python
import functools

import jax
import jax.numpy as jnp
import numpy as np
from jax import lax
from jax.experimental import pallas as pl
from jax.experimental.pallas import tpu as pltpu

F32 = jnp.float32
BF16 = jnp.bfloat16

D_MODEL = 1024
EPS = 1e-6
HEAD_DIM = 64
LANES = 128
SUBLANES = 8
GLA_HEADS = 4
GLA_DK = 48
GLA_DV = 96
GLA_QK = GLA_HEADS * GLA_DK
GLA_QK_PAD = 256
GLA_WIDTH = GLA_HEADS * GLA_DV
GLA_V_PAD = GLA_HEADS * LANES
GLA_GATE_RANK = 16
GLA_GATE_TEMP = 16.0
GLA_CHUNK = 64
LRU_WIDTH = 256
LRU_BLOCKS = 8
LRU_C = 8.0
LRU_CONV = 4
SWA_HEADS = 6
SWA_WIDTH = SWA_HEADS * HEAD_DIM
SWA_BLOCK = 128
SWA_SLABS = SWA_WIDTH // LANES
SWA_RES = 16
ROPE_THETA = 500000.0
ROPE_DIM = HEAD_DIM // 4
FFN_DIM = 2816
FFN_CONV = 3
FFN_CHUNK = 256
GLR_PAD = 128

_C_AQ = 0
_C_AK = _C_AQ + GLA_QK_PAD
_C_AV = _C_AK + GLA_QK_PAD
_C_AR = _C_AV + GLA_V_PAD
_C_BG = _C_AR + GLA_V_PAD
_C_BI = _C_BG + LRU_WIDTH
_C_CQ = _C_BI + LRU_WIDTH
_C_CK = _C_CQ + SWA_WIDTH
_C_CV = _C_CK + SWA_WIDTH
_C_GLR = _C_CV + SWA_WIDTH
IN_COLS_PACKED = _C_GLR + GLR_PAD

TOKEN_TILE = 512
V7X_VMEM_BYTES = 64 * 1024 * 1024
NEG_BIG = -1e30
LOG2E = float(np.log2(np.e))


def _params(semantics, vmem_mib):
    assert vmem_mib * 1024 * 1024 < V7X_VMEM_BYTES
    return pltpu.CompilerParams(dimension_semantics=semantics,
                                vmem_limit_bytes=vmem_mib * 1024 * 1024)


def _resident(shape, layer=None):
    nd = len(shape)
    if layer is None:
        return pl.BlockSpec(shape, lambda *_: (0,) * nd, pipeline_mode=pl.Buffered(1))
    return pl.BlockSpec((None,) + tuple(shape), lambda *_: (layer,) + (0,) * nd, pipeline_mode=pl.Buffered(1))


def _rms_scale(x):
    return lax.rsqrt(jnp.mean(x * x, axis=-1, keepdims=True) + EPS)


def _gelu_tanh(x):
    c = float(np.sqrt(2.0 / np.pi))
    return x * (0.5 + 0.5 * jnp.tanh(x * (c + (0.044715 * c) * (x * x))))


def _dot(a, b):
    return jnp.dot(a, b, preferred_element_type=F32)


def _dot_nt(a, b):
    return lax.dot_general(a, b, (((1,), (1,)), ((), ())), preferred_element_type=F32)


def _dot_tn(a, b):
    return lax.dot_general(a, b, (((0,), (0,)), ((), ())), preferred_element_type=F32)


def _inproj_kernel(x_ref, gn_ref, w_ref, wg_ref, bg_ref, cos_ref, sa_ref, sb_ref,
                   cw_ref, cb_ref, wa_ref, ba_ref, wx_ref, bx_ref, lam_ref, gnb_ref,
                   qe_ref, ke_ref, qi_ref, kd_ref, dec_ref, av_ref, ar_ref, ob_ref, cq_ref, ck_ref, cv_ref,
                   xtail_ref, hc_ref, rg_ref, rg4_ref, *, tiles_per_seq):
    tm = x_ref.shape[0]
    c = GLA_CHUNK

    @pl.when(pl.program_id(0) % tiles_per_seq == 0)
    def _():
        xtail_ref[...] = jnp.zeros_like(xtail_ref)
        hc_ref[...] = jnp.zeros_like(hc_ref)

    x = x_ref[...]
    h = (x * _rms_scale(x) * gn_ref[...]).astype(BF16)

    def proj(c0, width):
        return _dot(h, w_ref[:, c0:c0 + width])

    tile = 256
    sub = c
    row_sub = lax.broadcasted_iota(jnp.int32, (sub, 1), 0)
    silu = lambda t: (t * jax.nn.sigmoid(t)).astype(BF16)

    def scan_steps(update):
        shift = 1
        while shift < sub:
            if shift < SUBLANES:
                shifted = lambda v, fill, s=shift: jnp.where(row_sub >= s, pltpu.roll(v, s, 0), fill)
            else:
                shifted = lambda v, fill, s=shift: jnp.concatenate(
                    [jnp.full((s, v.shape[1]), fill, F32), v[:sub - s]], axis=0)
            update(shifted)
            shift *= 2

    p_glr = proj(_C_GLR, GLR_PAD)
    p_bi = proj(_C_BI, LRU_WIDTH)
    z = _dot(p_glr.astype(BF16), wg_ref[...]) + bg_ref[...]
    p_aq = proj(_C_AQ, GLA_QK_PAD)

    ext = jnp.concatenate([xtail_ref[...], p_bi], axis=0)
    xtail_ref[...] = p_bi[tm - SUBLANES:tm, :]
    cw = cw_ref[...]
    u = cb_ref[...] + p_bi * cw[LRU_CONV - 1:LRU_CONV, :]
    for kk in range(LRU_CONV - 1):
        u = u + pltpu.roll(ext, LRU_CONV - 1 - kk, 0)[SUBLANES:, :] * cw[kk:kk + 1, :]
    ub = u.astype(BF16)
    rg_pre = _dot(ub, wa_ref[...]) + ba_ref[...]
    ig_pre = _dot(ub, wx_ref[...]) + bx_ref[...]
    p_ak = proj(_C_AK, GLA_QK_PAD)
    p_bg = proj(_C_BG, LRU_WIDTH)

    def value_tile(k):
        if k == 1:
            av_ref[:, :tile] = proj(_C_AV, tile).astype(BF16)
        elif k == 3:
            av_ref[:, tile:] = proj(_C_AV + tile, tile).astype(BF16)
        elif k == 5:
            ar_ref[:, :tile] = silu(proj(_C_AR, tile))
        elif k == 7:
            ar_ref[:, tile:] = silu(proj(_C_AR + tile, tile))

    for k in range(tm // sub):
        rows = slice(k * sub, (k + 1) * sub)
        zk = z[rows]
        bsum = [(jnp.minimum(zk, 0.0) - jnp.log(1.0 + jnp.exp(-jnp.abs(zk)))) * (1.0 / GLA_GATE_TEMP)]

        def add_back(shifted, bsum=bsum):
            bsum[0] = bsum[0] + shifted(bsum[0], 0.0)

        scan_steps(add_back)
        b = bsum[0]
        b_mid = b[sub // 2 - 1:sub // 2, :]
        b_last = b[sub - 1:sub, :]
        q = p_aq[rows] * (GLA_DK ** -0.5)
        kx = p_ak[rows]
        qe_ref[rows, :] = (q * jnp.exp(b - b_mid)).astype(BF16)
        ke_ref[rows, :] = (kx * jnp.exp(b_mid - b)).astype(BF16)
        qi_ref[rows, :] = (q * jnp.exp(b)).astype(BF16)
        kd_ref[rows, :] = (kx * jnp.exp(b_last - b)).astype(BF16)
        dec_ref[k] = jnp.exp(b_last)
        value_tile(k)

    neg_lam = -lam_ref[...]
    softplus = jnp.maximum(neg_lam, 0.0) + jnp.log(1.0 + jnp.exp(-jnp.abs(neg_lam)))
    carry = hc_ref[...]
    swa_proj = {}
    for k in range(tm // sub):
        rows = slice(k * sub, (k + 1) * sub)
        log_a = (-LRU_C) * jax.nn.sigmoid(rg_pre[rows]) * softplus
        th = jnp.tanh(log_a)
        state = [jnp.exp(log_a), jnp.sqrt(-2.0 * th / (1.0 - th)) * (jax.nn.sigmoid(ig_pre[rows]) * u[rows])]

        def combine(shifted, state=state):
            a, hs = state
            state[1] = hs + a * shifted(hs, 0.0)
            state[0] = a * shifted(a, 1.0)

        scan_steps(combine)
        a, hs = state
        hs = hs + a * carry
        carry = hs[sub - 1:sub, :]
        y = hs * _gelu_tanh(p_bg[rows])
        ob_ref[rows, :] = (y * _rms_scale(y) * gnb_ref[...]).astype(ob_ref.dtype)
        if k % 2 == 1:
            swa_proj[k // 2] = proj(_C_CQ + (k // 2) * tile, tile)
    hc_ref[...] = carry
    p_c0, p_c1, p_c2, p_c3 = (swa_proj[i] for i in range(4))

    cos = cos_ref[...]
    sa = sa_ref[...]
    sb = sb_ref[...]

    def rope(t):
        return t * cos + pltpu.roll(t, LANES - ROPE_DIM // 2, 1) * sa + pltpu.roll(t, ROPE_DIM // 2, 1) * sb

    def regroup(out_ref, s, slot, val):
        rg_ref[slot] = val
        for a in range(4):
            rg4_ref[slot, a] = rg_ref[slot, pl.ds(a, tm // 4, stride=4), :]
        for a in range(4):
            for b in range(4):
                out_ref[4 * b + a, s] = rg4_ref[slot, a, pl.ds(b, tm // SWA_RES, stride=4), :]

    q_scale = HEAD_DIM ** -0.5 * LOG2E
    regroup(cq_ref, 0, 0, rope(p_c0[:, :LANES] * q_scale))
    regroup(cq_ref, 1, 1, rope(p_c0[:, LANES:] * q_scale))
    p_c4 = proj(_C_CQ + 4 * tile, LANES)
    regroup(cq_ref, 2, 2, rope(p_c1[:, :LANES] * q_scale))
    regroup(ck_ref, 0, 3, rope(p_c1[:, LANES:]))
    regroup(ck_ref, 1, 4, rope(p_c2[:, :LANES]))
    regroup(ck_ref, 2, 5, rope(p_c2[:, LANES:]))
    regroup(cv_ref, 0, 6, p_c3[:, :LANES])
    regroup(cv_ref, 1, 7, p_c3[:, LANES:])
    regroup(cv_ref, 2, 8, p_c4)


def _inproj(x, gn, w, wg, bg, cos, sa, sb, lru, seq, layer):
    n = x.shape[0]
    tm = TOKEN_TILE
    tiles_per_seq = seq // tm
    row = lambda width: pl.BlockSpec((tm, width), lambda i: (i, 0))
    grouped = pl.BlockSpec((SWA_RES, SWA_SLABS, tm // SWA_RES, LANES), lambda i: (0, 0, i, 0))
    table = pl.BlockSpec((tm, LANES), lambda i: (i % tiles_per_seq, 0))
    vec = _resident((1, LRU_WIDTH), layer)
    mat = _resident((LRU_WIDTH, LRU_WIDTH), layer)
    sds = jax.ShapeDtypeStruct
    grouped_shape = sds((SWA_RES, SWA_SLABS, n // SWA_RES, LANES), F32)
    out_shapes = (
        sds((n, GLA_QK_PAD), BF16), sds((n, GLA_QK_PAD), BF16), sds((n, GLA_QK_PAD), BF16),
        sds((n, GLA_QK_PAD), BF16), sds((n // GLA_CHUNK, 1, GLA_QK_PAD), F32),
        sds((n, GLA_V_PAD), BF16), sds((n, GLA_V_PAD), BF16), sds((n, LRU_WIDTH), BF16),
        grouped_shape, grouped_shape, grouped_shape,
    )
    return pl.pallas_call(
        functools.partial(_inproj_kernel, tiles_per_seq=tiles_per_seq),
        grid=(n // tm,),
        in_specs=[row(D_MODEL), _resident((1, D_MODEL), layer), _resident((D_MODEL, IN_COLS_PACKED), layer),
                  _resident((GLR_PAD, GLA_QK_PAD), layer), _resident((1, GLA_QK_PAD), layer), table, table, table,
                  _resident((LRU_CONV, LRU_WIDTH), layer), vec, mat, vec, mat, vec, vec, vec],
        out_specs=(row(GLA_QK_PAD), row(GLA_QK_PAD), row(GLA_QK_PAD), row(GLA_QK_PAD),
                   pl.BlockSpec((tm // GLA_CHUNK, 1, GLA_QK_PAD), lambda i: (i, 0, 0)),
                   row(GLA_V_PAD), row(GLA_V_PAD), row(LRU_WIDTH), grouped, grouped, grouped),
        out_shape=out_shapes,
        scratch_shapes=[pltpu.VMEM((SUBLANES, LRU_WIDTH), F32), pltpu.VMEM((1, LRU_WIDTH), F32),
                        pltpu.VMEM((3 * SWA_SLABS, tm, LANES), F32),
                        pltpu.VMEM((3 * SWA_SLABS, 4, tm // 4, LANES), F32)],
        compiler_params=_params(("arbitrary",), 48),
        name="inproj",
    )(x, gn, w, wg, bg, cos, sa, sb, *lru)


def _gla_kernel(qe_ref, ke_ref, qi_ref, kd_ref, dec_ref, v_ref, r_ref, gn_ref, o_ref, st_ref, *, seq):
    c = GLA_CHUNK
    st_ref[...] = jnp.zeros_like(st_ref)

    lane_q = lax.broadcasted_iota(jnp.int32, (1, GLA_QK_PAD), 1)
    q_masks = [((lane_q >= h * GLA_DK) & (lane_q < (h + 1) * GLA_DK)).astype(BF16) for h in range(GLA_HEADS)]
    ri = lax.broadcasted_iota(jnp.int32, (GLA_HEADS * c, c), 0) & (c - 1)
    ci = lax.broadcasted_iota(jnp.int32, (GLA_HEADS * c, c), 1)
    causal = ci <= ri
    gn = gn_ref[...]

    per_step = 16
    heads = [slice(h * LANES, (h + 1) * LANES) for h in range(GLA_HEADS)]

    def step(i, carry):
        ids = [per_step * i + u for u in range(per_step)]
        rows = [pl.ds(pl.multiple_of(ci * c, c), c) for ci in ids]
        vbs = [v_ref[r, :] for r in rows]
        scores = [_dot_nt(jnp.concatenate([qe_ref[r, :] * m for m in q_masks], axis=0), ke_ref[r, :])
                  for r in rows]
        incs = []
        for r, vb in zip(rows, vbs):
            kd = kd_ref[r, :]
            incs.append([_dot_tn(vb[:, sl], kd * q_masks[h]) for h, sl in enumerate(heads)])
        st = [st_ref[sl, :] for sl in heads]
        for ci, r, vb, s, inc in zip(ids, rows, vbs, scores, incs):
            o_inter = _dot_nt(qi_ref[r, :], jnp.concatenate(st, axis=0).astype(BF16))
            s = jnp.where(causal, s, 0.0).astype(BF16)
            dec = dec_ref[ci]
            gate = r_ref[r, :]
            for h, sl in enumerate(heads):
                o = o_inter[:, sl] + _dot(s[h * c:(h + 1) * c, :], vb[:, sl])
                st[h] = st[h] * dec + inc[h]
                ms = jnp.sum(o * o, axis=-1, keepdims=True) * (1.0 / GLA_DV)
                y = o * lax.rsqrt(ms + EPS) * gn[:, sl] * gate[:, sl].astype(F32)
                o_ref[r, sl] = y.astype(o_ref.dtype)
        for h, sl in enumerate(heads):
            st_ref[sl, :] = st[h]
        return carry

    lax.fori_loop(0, seq // (c * per_step), step, 0)


def _gla(qe, ke, qi, kd, dec, av, ar, gn, seq, layer):
    n = qe.shape[0]
    blk = lambda width: pl.BlockSpec((seq, width), lambda b: (b, 0))
    return pl.pallas_call(
        functools.partial(_gla_kernel, seq=seq),
        grid=(n // seq,),
        in_specs=[blk(GLA_QK_PAD), blk(GLA_QK_PAD), blk(GLA_QK_PAD), blk(GLA_QK_PAD),
                  pl.BlockSpec((seq // GLA_CHUNK, 1, GLA_QK_PAD), lambda b: (b, 0, 0)),
                  blk(GLA_V_PAD), blk(GLA_V_PAD), _resident((1, GLA_V_PAD), layer)],
        out_specs=blk(GLA_V_PAD),
        out_shape=jax.ShapeDtypeStruct((n, GLA_V_PAD), BF16),
        scratch_shapes=[pltpu.VMEM((GLA_V_PAD, GLA_QK_PAD), F32)],
        compiler_params=_params(("parallel",), 40),
        name="gla",
    )(qe, ke, qi, kd, dec, av, ar, gn)


def _swa_kernel(qd_ref, kd_ref, vd_ref, gn_ref, o_ref, acc_ref, m_ref, l_ref, out4_ref, *, seq):
    blk = SWA_BLOCK
    res = SWA_RES
    nj = seq // res
    assert nj == blk
    lane = lax.broadcasted_iota(jnp.int32, (1, LANES), 1)
    head0 = lane < HEAD_DIM

    def permuted_masks(groups):
        q_per = blk // groups
        k_per = 2 * blk // groups
        qi = lax.broadcasted_iota(jnp.int32, (blk, 2 * blk), 0)
        ki = lax.broadcasted_iota(jnp.int32, (blk, 2 * blk), 1)
        tq = (qi % q_per) * groups + qi // q_per
        tk = (ki % k_per) * groups + ki // k_per
        two_blocks = (tk >= tq) & (tk <= tq + blk)
        first_block = tk <= tq
        return two_blocks, first_block

    def attend_all(units):
        both = lambda t: jnp.where(head0, t[:blk], t[blk:])
        scores = []
        for qs, kcat, _, _ in units:
            q2 = jnp.concatenate([jnp.where(head0, qs, 0.0), jnp.where(head0, 0.0, qs)], axis=0)
            scores.append(_dot_nt(q2.astype(BF16), kcat.astype(BF16)))
        soft = []
        for s, (_, _, _, mask) in zip(scores, units):
            s = jnp.where(jnp.concatenate([mask, mask], axis=0), s, NEG_BIG)
            m = jnp.max(s, axis=-1, keepdims=True)
            soft.append((m, jnp.exp2(s - m).astype(BF16)))
        outs = []
        for (m, p), (_, _, vcat, _) in zip(soft, units):
            v_ones = jnp.concatenate([vcat.astype(BF16), jnp.ones(vcat.shape, BF16)], axis=1)
            pv = _dot(p, v_ones)
            outs.append((both(m), both(pv[:, LANES:]), both(pv[:, :LANES])))
        return outs

    def merge(old, new):
        (m_old, l_old, acc_old), (m_new, l_new, pv) = old, new
        m_tot = jnp.maximum(m_old, m_new)
        w_old = jnp.exp2(m_old - m_tot)
        w_new = jnp.exp2(m_new - m_tot)
        return m_tot, w_old * l_old + w_new * l_new, w_old * acc_old + w_new * pv

    qi1 = lax.broadcasted_iota(jnp.int32, (blk, blk), 0)
    kj1 = lax.broadcasted_iota(jnp.int32, (blk, blk), 1)
    causal = kj1 <= qi1

    per_step = 4

    def dil16(i, carry):
        units = [(per_step * i + u, s) for u in range(per_step) for s in range(SWA_SLABS)]
        loaded = [(qd_ref[c, s], kd_ref[c, s], vd_ref[c, s]) for c, s in units]
        results = attend_all([(qs, ks, vs, causal) for qs, ks, vs in loaded])
        for (c, s), (m_new, l_new, pv) in zip(units, results):
            m_ref[c, s] = m_new
            l_ref[c, s] = l_new
            acc_ref[c, s] = pv
        return carry

    lax.fori_loop(0, res // per_step, dil16, 0)

    def dilated(groups):
        n_blocks = nj * groups // blk
        n_res = res // groups
        q_per = blk // groups
        k_per = 2 * blk // groups
        two_blocks, first_block = permuted_masks(groups)
        per_step = 2
        nb_bits = n_blocks.bit_length() - 1
        assert n_blocks == 1 << nb_bits

        def step(i, carry):
            loaded = []
            for u in range(per_step):
                it = per_step * i + u
                r = lax.shift_right_logical(it, nb_bits)
                bi = it & (n_blocks - 1)
                q_rows = pl.ds(pl.multiple_of(bi * q_per, q_per), q_per)
                k_rows = pl.ds(pl.multiple_of(jnp.maximum(bi - 1, 0) * q_per, q_per), k_per)
                mask = (two_blocks & (bi > 0)) | (first_block & (bi == 0))
                gather = lambda ref, s, rows, r=r: jnp.concatenate(
                    [ref[a * n_res + r, s, rows, :] for a in range(groups)], axis=0)
                for s in range(SWA_SLABS):
                    old = (gather(m_ref, s, q_rows), gather(l_ref, s, q_rows), gather(acc_ref, s, q_rows))
                    loaded.append((r, s, q_rows, mask, gather(qd_ref, s, q_rows), gather(kd_ref, s, k_rows),
                                   gather(vd_ref, s, k_rows), old))
            news = attend_all([(qs, kcat, vcat, mask) for _, _, _, mask, qs, kcat, vcat, _ in loaded])
            results = [merge(unit[-1], new) for unit, new in zip(loaded, news)]
            for (r, s, q_rows, *_), (m_new, l_new, acc_new) in zip(loaded, results):
                for a in range(groups):
                    piece = slice(a * q_per, (a + 1) * q_per)
                    m_ref[a * n_res + r, s, q_rows, :] = m_new[piece]
                    l_ref[a * n_res + r, s, q_rows, :] = l_new[piece]
                    acc_ref[a * n_res + r, s, q_rows, :] = acc_new[piece]
            return carry

        lax.fori_loop(0, n_res * n_blocks // per_step, step, 0)

    dilated(4)
    dilated(16)

    def finish(a, carry):
        for b in range(4):
            c = 4 * b + a
            outs = [acc_ref[c, s] / l_ref[c, s] for s in range(SWA_SLABS)]
            ss = sum(jnp.sum(o * o, axis=-1, keepdims=True) for o in outs)
            scale = lax.rsqrt(ss * (1.0 / SWA_WIDTH) + EPS)
            for s in range(SWA_SLABS):
                y = outs[s] * scale * gn_ref[:, s * LANES:(s + 1) * LANES]
                out4_ref[s, pl.ds(b, nj, stride=4), :] = y
        for s in range(SWA_SLABS):
            o_ref[s, pl.ds(a, 4 * nj, stride=4), :] = out4_ref[s]
        return carry

    lax.fori_loop(0, 4, finish, 0)


def _swa(cq, ck, cv, gn, seq, layer):
    n = cq.shape[2] * SWA_RES
    nj = seq // SWA_RES
    grouped = pl.BlockSpec((SWA_RES, SWA_SLABS, nj, LANES), lambda b: (0, 0, b, 0))
    state = pltpu.VMEM((SWA_RES, SWA_SLABS, nj, LANES), F32)
    return pl.pallas_call(
        functools.partial(_swa_kernel, seq=seq),
        grid=(n // seq,),
        in_specs=[grouped, grouped, grouped, _resident((1, SWA_WIDTH), layer)],
        out_specs=pl.BlockSpec((SWA_SLABS, seq, LANES), lambda b: (0, b, 0)),
        out_shape=jax.ShapeDtypeStruct((SWA_SLABS, n, LANES), F32),
        scratch_shapes=[state] * 3 + [pltpu.VMEM((SWA_SLABS, 4 * nj, LANES), F32)],
        compiler_params=_params(("parallel",), 48),
        name="swa",
    )(cq, ck, cv, gn)


def _ffn_kernel(x_ref, oa_ref, ob_ref, oc_ref, woa_ref, wob_ref, woc_ref, gn_ref, wup_ref, cw_ref, cb_ref,
                wdn_ref, gfin_ref, out_ref, carry_ref, act_ref, *, tiles_per_seq, final):
    tm = x_ref.shape[0]

    @pl.when(pl.program_id(0) % tiles_per_seq == 0)
    def _():
        carry_ref[...] = jnp.zeros_like(carry_ref)

    oc = jnp.concatenate([oc_ref[s].astype(BF16) for s in range(SWA_SLABS)], axis=1)
    x1 = x_ref[...] + _dot(oa_ref[...], woa_ref[...]) + _dot(ob_ref[...], wob_ref[...]) + _dot(oc, woc_ref[...])
    h = (x1 * _rms_scale(x1) * gn_ref[...]).astype(BF16)

    def conv_up(c0):
        cols = slice(c0, c0 + FFN_CHUNK)
        up = _dot(h, wup_ref[:, cols])
        ext = jnp.concatenate([carry_ref[:, cols], up], axis=0)
        carry_ref[:, cols] = up[tm - SUBLANES:tm, :]
        cw = cw_ref[:, cols]
        return (cb_ref[:, cols] + up * cw[2:3, :] + pltpu.roll(ext, 2, 0)[SUBLANES:, :] * cw[0:1, :]
                + pltpu.roll(ext, 1, 0)[SUBLANES:, :] * cw[1:2, :])

    for j in range(FFN_DIM // FFN_CHUNK):
        val = conv_up(j * FFN_CHUNK)
        gate = conv_up(FFN_DIM + j * FFN_CHUNK)
        act_ref[:, j * FFN_CHUNK:(j + 1) * FFN_CHUNK] = (_gelu_tanh(gate) * val).astype(BF16)
    x2 = x1 + _dot(act_ref[...], wdn_ref[...])
    if final:
        x2 = x2 * _rms_scale(x2) * gfin_ref[...]
    out_ref[...] = x2


def _ffn(x, oa, ob, oc, woa, wob, woc, gn, wup, cw, cb, wdn, gfin, seq, layer, final):
    n = x.shape[0]
    tm = TOKEN_TILE
    row = lambda width: pl.BlockSpec((tm, width), lambda i: (i, 0))
    return pl.pallas_call(
        functools.partial(_ffn_kernel, tiles_per_seq=seq // tm, final=final),
        grid=(n // tm,),
        in_specs=[row(D_MODEL), row(GLA_V_PAD), row(LRU_WIDTH),
                  pl.BlockSpec((SWA_SLABS, tm, LANES), lambda i: (0, i, 0)),
                  _resident((GLA_V_PAD, D_MODEL), layer), _resident((LRU_WIDTH, D_MODEL), layer),
                  _resident((SWA_WIDTH, D_MODEL), layer), _resident((1, D_MODEL), layer),
                  _resident((D_MODEL, 2 * FFN_DIM), layer), _resident((FFN_CONV, 2 * FFN_DIM), layer),
                  _resident((1, 2 * FFN_DIM), layer), _resident((FFN_DIM, D_MODEL), layer), _resident((1, D_MODEL))],
        out_specs=row(D_MODEL),
        out_shape=jax.ShapeDtypeStruct((n, D_MODEL), F32),
        scratch_shapes=[pltpu.VMEM((SUBLANES, 2 * FFN_DIM), F32), pltpu.VMEM((tm, FFN_DIM), BF16)],
        compiler_params=_params(("arbitrary",), 56),
        name="ffn",
    )(x, oa, ob, oc, woa, wob, woc, gn, wup, cw, cb, wdn, gfin)


def _rope_tables(seq):
    half = ROPE_DIM // 2
    inv = ROPE_THETA ** (-jnp.arange(0, ROPE_DIM, 2, dtype=F32) / ROPE_DIM)
    ang = jnp.arange(seq, dtype=F32)[:, None] * inv[None, :]
    cos, sin = jnp.cos(ang), jnp.sin(ang)
    ones = jnp.ones((seq, HEAD_DIM - ROPE_DIM), F32)
    zeros = jnp.zeros((seq, HEAD_DIM - ROPE_DIM), F32)
    zh = jnp.zeros((seq, half), F32)
    per_head = lambda parts: jnp.tile(jnp.concatenate(parts, axis=1), (1, LANES // HEAD_DIM))
    return per_head([cos, cos, ones]), per_head([-sin, zh, zeros]), per_head([zh, sin, zeros])


def _pad_heads(w, axis):
    shape = list(w.shape)
    shape[axis:axis + 1] = [GLA_HEADS, GLA_DV]
    w = w.reshape(shape)
    pad = [(0, 0)] * w.ndim
    pad[axis + 1] = (0, LANES - GLA_DV)
    shape[axis:axis + 2] = [GLA_V_PAD]
    return jnp.pad(w, pad).reshape(shape)


def _pack_w_in(w):
    z = lambda width: jnp.zeros(w.shape[:-1] + (width,), w.dtype)
    o_v = 2 * GLA_QK
    o_r = o_v + GLA_WIDTH
    o_glr = o_r + GLA_WIDTH
    return jnp.concatenate([
        w[..., 0:GLA_QK], z(GLA_QK_PAD - GLA_QK), w[..., GLA_QK:o_v], z(GLA_QK_PAD - GLA_QK),
        _pad_heads(w[..., o_v:o_r], 2), _pad_heads(w[..., o_r:o_glr], 2),
        w[..., o_glr + GLA_GATE_RANK:], w[..., o_glr:o_glr + GLA_GATE_RANK], z(GLR_PAD - GLA_GATE_RANK)],
        axis=-1).astype(BF16)


def _block_diag(w):
    layers, nb, bs, _ = w.shape
    eye = jnp.eye(nb, dtype=w.dtype)
    return jnp.einsum('lnij,nm->lnimj', w, eye).reshape(layers, nb * bs, nb * bs)


def kernel(x, norm_mix, w_in, gla_w_gate, gla_b_gate, gla_norm, lru_conv_w, lru_conv_b, lru_w_a, lru_b_a,
           lru_w_x, lru_b_x, lru_lambda, lru_norm, swa_norm, w_out, norm_ffn, ffn_w_up, ffn_conv_w, ffn_conv_b,
           ffn_w_down, norm_final):
    batch, seq, _ = x.shape
    depth = w_in.shape[0]
    assert seq % TOKEN_TILE == 0 and seq == SWA_BLOCK * SWA_RES
    cos, sa, sb = _rope_tables(seq)
    xf = x.reshape(batch * seq, D_MODEL)
    rows = lambda v: v.reshape(depth, 1, -1)
    wg = jnp.pad(gla_w_gate, ((0, 0), (0, GLR_PAD - GLA_GATE_RANK), (0, GLA_QK_PAD - GLA_QK))).astype(BF16)
    bg = rows(jnp.pad(gla_b_gate, ((0, 0), (0, GLA_QK_PAD - GLA_QK))))
    w_in_p = _pack_w_in(w_in)
    gla_gn = rows(_pad_heads(jnp.tile(gla_norm, (1, GLA_HEADS)), 1))
    wa = _block_diag(lru_w_a).astype(BF16)
    wx = _block_diag(lru_w_x).astype(BF16)
    wo = w_out.astype(BF16)
    wo_a = _pad_heads(wo[:, :GLA_WIDTH], 1)
    wo_b = wo[:, GLA_WIDTH:GLA_WIDTH + LRU_WIDTH]
    wo_c = wo[:, GLA_WIDTH + LRU_WIDTH:]
    w_up = ffn_w_up.astype(BF16)
    w_dn = ffn_w_down.astype(BF16)
    vecs = [rows(v) for v in (norm_mix, lru_conv_b, lru_b_a, lru_b_x, lru_lambda, lru_norm, swa_norm, norm_ffn,
                              ffn_conv_b)]
    g_mix, cb_lru, ba, bx, lam, g_lru, g_swa, g_ffn, cb_ffn = vecs
    g_fin = norm_final.reshape(1, -1)
    lru = (lru_conv_w, cb_lru, wa, ba, wx, bx, lam, g_lru)
    for l in range(depth):
        qe, ke, qi, kd, dec, av, ar, o_b, cq, ck, cv = _inproj(xf, g_mix, w_in_p, wg, bg, cos, sa, sb, lru, seq, l)
        o_a = _gla(qe, ke, qi, kd, dec, av, ar, gla_gn, seq, l)
        o_c = _swa(cq, ck, cv, g_swa, seq, l)
        xf = _ffn(xf, o_a, o_b, o_c, wo_a, wo_b, wo_c, g_ffn, w_up, ffn_conv_w, cb_ffn, w_dn, g_fin, seq, l,
                  l == depth - 1)
    return xf.reshape(batch, seq, D_MODEL)
```

```python
import functools

import jax
import jax.numpy as jnp
import numpy as np
from jax import lax
from jax.experimental import pallas as pl
from jax.experimental.pallas import tpu as pltpu

F32 = jnp.float32
BF16 = jnp.bfloat16

D_MODEL = 1024
EPS = 1e-6
HEAD_DIM = 64
LANES = 128
SUBLANES = 8
GLA_HEADS = 4
GLA_DK = 48
GLA_DV = 96
GLA_QK = GLA_HEADS * GLA_DK
GLA_QK_PAD = 256
GLA_WIDTH = GLA_HEADS * GLA_DV
GLA_V_PAD = GLA_HEADS * LANES
GLA_GATE_RANK = 16
GLA_GATE_TEMP = 16.0
GLA_CHUNK = 64
LRU_WIDTH = 256
LRU_BLOCKS = 8
LRU_C = 8.0
LRU_CONV = 4
SWA_HEADS = 6
SWA_WIDTH = SWA_HEADS * HEAD_DIM
SWA_BLOCK = 128
SWA_SLABS = SWA_WIDTH // LANES
SWA_RES = 16
ROPE_THETA = 500000.0
ROPE_DIM = HEAD_DIM // 4
FFN_DIM = 2816
FFN_CONV = 3
FFN_CHUNK = 256
GLR_PAD = 128

_C_AQ = 0
_C_AK = _C_AQ + GLA_QK_PAD
_C_AV = _C_AK + GLA_QK_PAD
_C_AR = _C_AV + GLA_V_PAD
_C_BG = _C_AR + GLA_V_PAD
_C_BI = _C_BG + LRU_WIDTH
_C_CQ = _C_BI + LRU_WIDTH
_C_CK = _C_CQ + SWA_WIDTH
_C_CV = _C_CK + SWA_WIDTH
_C_GLR = _C_CV + SWA_WIDTH
IN_COLS_PACKED = _C_GLR + GLR_PAD

TOKEN_TILE = 512
V7X_VMEM_BYTES = 64 * 1024 * 1024
NEG_BIG = -1e30
LOG2E = float(np.log2(np.e))


def _params(semantics, vmem_mib):
    assert vmem_mib * 1024 * 1024 < V7X_VMEM_BYTES
    return pltpu.CompilerParams(dimension_semantics=semantics,
                                vmem_limit_bytes=vmem_mib * 1024 * 1024)


def _resident(shape, layer=None):
    nd = len(shape)
    if layer is None:
        return pl.BlockSpec(shape, lambda *_: (0,) * nd, pipeline_mode=pl.Buffered(1))
    return pl.BlockSpec((None,) + tuple(shape), lambda *_: (layer,) + (0,) * nd, pipeline_mode=pl.Buffered(1))


def _rms_scale(x):
    return lax.rsqrt(jnp.mean(x * x, axis=-1, keepdims=True) + EPS)


def _gelu_tanh(x):
    c = float(np.sqrt(2.0 / np.pi))
    return x * (0.5 + 0.5 * jnp.tanh(x * (c + (0.044715 * c) * (x * x))))


def _dot(a, b):
    return jnp.dot(a, b, preferred_element_type=F32)


def _dot_nt(a, b):
    return lax.dot_general(a, b, (((1,), (1,)), ((), ())), preferred_element_type=F32)


def _dot_tn(a, b):
    return lax.dot_general(a, b, (((0,), (0,)), ((), ())), preferred_element_type=F32)


def _inproj_kernel(x_ref, gn_ref, w_ref, wg_ref, bg_ref, cos_ref, sa_ref, sb_ref,
                   cw_ref, cb_ref, wa_ref, ba_ref, wx_ref, bx_ref, lam_ref, gnb_ref,
                   qe_ref, ke_ref, qi_ref, kd_ref, dec_ref, av_ref, ar_ref, ob_ref, cq_ref, ck_ref, cv_ref,
                   xtail_ref, hc_ref, rg_ref, rg4_ref, *, tiles_per_seq):
    tm = x_ref.shape[0]
    c = GLA_CHUNK

    @pl.when(pl.program_id(0) % tiles_per_seq == 0)
    def _():
        xtail_ref[...] = jnp.zeros_like(xtail_ref)
        hc_ref[...] = jnp.zeros_like(hc_ref)

    x = x_ref[...]
    h = (x * _rms_scale(x) * gn_ref[...]).astype(BF16)

    def proj(c0, width):
        return _dot(h, w_ref[:, c0:c0 + width])

    tile = 256
    sub = c
    row_sub = lax.broadcasted_iota(jnp.int32, (sub, 1), 0)
    silu = lambda t: (t * jax.nn.sigmoid(t)).astype(BF16)

    def scan_steps(update):
        shift = 1
        while shift < sub:
            if shift < SUBLANES:
                shifted = lambda v, fill, s=shift: jnp.where(row_sub >= s, pltpu.roll(v, s, 0), fill)
            else:
                shifted = lambda v, fill, s=shift: jnp.concatenate(
                    [jnp.full((s, v.shape[1]), fill, F32), v[:sub - s]], axis=0)
            update(shifted)
            shift *= 2

    p_glr = proj(_C_GLR, GLR_PAD)
    p_bi = proj(_C_BI, LRU_WIDTH)
    z = _dot(p_glr.astype(BF16), wg_ref[...]) + bg_ref[...]
    p_aq = proj(_C_AQ, GLA_QK_PAD)

    ext = jnp.concatenate([xtail_ref[...], p_bi], axis=0)
    xtail_ref[...] = p_bi[tm - SUBLANES:tm, :]
    cw = cw_ref[...]
    u = cb_ref[...] + p_bi * cw[LRU_CONV - 1:LRU_CONV, :]
    for kk in range(LRU_CONV - 1):
        u = u + pltpu.roll(ext, LRU_CONV - 1 - kk, 0)[SUBLANES:, :] * cw[kk:kk + 1, :]
    ub = u.astype(BF16)
    rg_pre = _dot(ub, wa_ref[...]) + ba_ref[...]
    ig_pre = _dot(ub, wx_ref[...]) + bx_ref[...]
    p_ak = proj(_C_AK, GLA_QK_PAD)
    p_bg = proj(_C_BG, LRU_WIDTH)

    def value_tile(k):
        if k == 1:
            av_ref[:, :tile] = proj(_C_AV, tile).astype(BF16)
        elif k == 3:
            av_ref[:, tile:] = proj(_C_AV + tile, tile).astype(BF16)
        elif k == 5:
            ar_ref[:, :tile] = silu(proj(_C_AR, tile))
        elif k == 7:
            ar_ref[:, tile:] = silu(proj(_C_AR + tile, tile))

    for k in range(tm // sub):
        rows = slice(k * sub, (k + 1) * sub)
        zk = z[rows]
        bsum = [(jnp.minimum(zk, 0.0) - jnp.log(1.0 + jnp.exp(-jnp.abs(zk)))) * (1.0 / GLA_GATE_TEMP)]

        def add_back(shifted, bsum=bsum):
            bsum[0] = bsum[0] + shifted(bsum[0], 0.0)

        scan_steps(add_back)
        b = bsum[0]
        b_mid = b[sub // 2 - 1:sub // 2, :]
        b_last = b[sub - 1:sub, :]
        q = p_aq[rows] * (GLA_DK ** -0.5)
        kx = p_ak[rows]
        qe_ref[rows, :] = (q * jnp.exp(b - b_mid)).astype(BF16)
        ke_ref[rows, :] = (kx * jnp.exp(b_mid - b)).astype(BF16)
        qi_ref[rows, :] = (q * jnp.exp(b)).astype(BF16)
        kd_ref[rows, :] = (kx * jnp.exp(b_last - b)).astype(BF16)
        dec_ref[k] = jnp.exp(b_last)
        value_tile(k)

    neg_lam = -lam_ref[...]
    softplus = jnp.maximum(neg_lam, 0.0) + jnp.log(1.0 + jnp.exp(-jnp.abs(neg_lam)))
    carry = hc_ref[...]
    swa_proj = {}
    for k in range(tm // sub):
        rows = slice(k * sub, (k + 1) * sub)
        log_a = (-LRU_C) * jax.nn.sigmoid(rg_pre[rows]) * softplus
        th = jnp.tanh(log_a)
        state = [jnp.exp(log_a), jnp.sqrt(-2.0 * th / (1.0 - th)) * (jax.nn.sigmoid(ig_pre[rows]) * u[rows])]

        def combine(shifted, state=state):
            a, hs = state
            state[1] = hs + a * shifted(hs, 0.0)
            state[0] = a * shifted(a, 1.0)

        scan_steps(combine)
        a, hs = state
        hs = hs + a * carry
        carry = hs[sub - 1:sub, :]
        y = hs * _gelu_tanh(p_bg[rows])
        ob_ref[rows, :] = (y * _rms_scale(y) * gnb_ref[...]).astype(ob_ref.dtype)
        if k % 2 == 1:
            swa_proj[k // 2] = proj(_C_CQ + (k // 2) * tile, tile)
    hc_ref[...] = carry
    p_c0, p_c1, p_c2, p_c3 = (swa_proj[i] for i in range(4))

    cos = cos_ref[...]
    sa = sa_ref[...]
    sb = sb_ref[...]

    def rope(t):
        return t * cos + pltpu.roll(t, LANES - ROPE_DIM // 2, 1) * sa + pltpu.roll(t, ROPE_DIM // 2, 1) * sb

    def regroup(out_ref, s, slot, val):
        rg_ref[slot] = val
        for a in range(4):
            rg4_ref[slot, a] = rg_ref[slot, pl.ds(a, tm // 4, stride=4), :]
        for a in range(4):
            for b in range(4):
                out_ref[4 * b + a, s] = rg4_ref[slot, a, pl.ds(b, tm // SWA_RES, stride=4), :]

    q_scale = HEAD_DIM ** -0.5 * LOG2E
    regroup(cq_ref, 0, 0, rope(p_c0[:, :LANES] * q_scale))
    regroup(cq_ref, 1, 1, rope(p_c0[:, LANES:] * q_scale))
    p_c4 = proj(_C_CQ + 4 * tile, LANES)
    regroup(cq_ref, 2, 2, rope(p_c1[:, :LANES] * q_scale))
    regroup(ck_ref, 0, 3, rope(p_c1[:, LANES:]))
    regroup(ck_ref, 1, 4, rope(p_c2[:, :LANES]))
    regroup(ck_ref, 2, 5, rope(p_c2[:, LANES:]))
    regroup(cv_ref, 0, 6, p_c3[:, :LANES])
    regroup(cv_ref, 1, 7, p_c3[:, LANES:])
    regroup(cv_ref, 2, 8, p_c4)


def _inproj(x, gn, w, wg, bg, cos, sa, sb, lru, seq, layer):
    n = x.shape[0]
    tm = TOKEN_TILE
    tiles_per_seq = seq // tm
    row = lambda width: pl.BlockSpec((tm, width), lambda i: (i, 0))
    grouped = pl.BlockSpec((SWA_RES, SWA_SLABS, tm // SWA_RES, LANES), lambda i: (0, 0, i, 0))
    table = pl.BlockSpec((tm, LANES), lambda i: (i % tiles_per_seq, 0))
    vec = _resident((1, LRU_WIDTH), layer)
    mat = _resident((LRU_WIDTH, LRU_WIDTH), layer)
    sds = jax.ShapeDtypeStruct
    grouped_shape = sds((SWA_RES, SWA_SLABS, n // SWA_RES, LANES), F32)
    out_shapes = (
        sds((n, GLA_QK_PAD), BF16), sds((n, GLA_QK_PAD), BF16), sds((n, GLA_QK_PAD), BF16),
        sds((n, GLA_QK_PAD), BF16), sds((n // GLA_CHUNK, 1, GLA_QK_PAD), F32),
        sds((n, GLA_V_PAD), BF16), sds((n, GLA_V_PAD), BF16), sds((n, LRU_WIDTH), BF16),
        grouped_shape, grouped_shape, grouped_shape,
    )
    return pl.pallas_call(
        functools.partial(_inproj_kernel, tiles_per_seq=tiles_per_seq),
        grid=(n // tm,),
        in_specs=[row(D_MODEL), _resident((1, D_MODEL), layer), _resident((D_MODEL, IN_COLS_PACKED), layer),
                  _resident((GLR_PAD, GLA_QK_PAD), layer), _resident((1, GLA_QK_PAD), layer), table, table, table,
                  _resident((LRU_CONV, LRU_WIDTH), layer), vec, mat, vec, mat, vec, vec, vec],
        out_specs=(row(GLA_QK_PAD), row(GLA_QK_PAD), row(GLA_QK_PAD), row(GLA_QK_PAD),
                   pl.BlockSpec((tm // GLA_CHUNK, 1, GLA_QK_PAD), lambda i: (i, 0, 0)),
                   row(GLA_V_PAD), row(GLA_V_PAD), row(LRU_WIDTH), grouped, grouped, grouped),
        out_shape=out_shapes,
        scratch_shapes=[pltpu.VMEM((SUBLANES, LRU_WIDTH), F32), pltpu.VMEM((1, LRU_WIDTH), F32),
                        pltpu.VMEM((3 * SWA_SLABS, tm, LANES), F32),
                        pltpu.VMEM((3 * SWA_SLABS, 4, tm // 4, LANES), F32)],
        compiler_params=_params(("arbitrary",), 48),
        name="inproj",
    )(x, gn, w, wg, bg, cos, sa, sb, *lru)


def _gla_kernel(qe_ref, ke_ref, qi_ref, kd_ref, dec_ref, v_ref, r_ref, gn_ref, o_ref, st_ref, *, seq):
    c = GLA_CHUNK
    st_ref[...] = jnp.zeros_like(st_ref)

    lane_q = lax.broadcasted_iota(jnp.int32, (1, GLA_QK_PAD), 1)
    q_masks = [((lane_q >= h * GLA_DK) & (lane_q < (h + 1) * GLA_DK)).astype(BF16) for h in range(GLA_HEADS)]
    ri = lax.broadcasted_iota(jnp.int32, (GLA_HEADS * c, c), 0) & (c - 1)
    ci = lax.broadcasted_iota(jnp.int32, (GLA_HEADS * c, c), 1)
    causal = ci <= ri
    gn = gn_ref[...]

    per_step = 16
    heads = [slice(h * LANES, (h + 1) * LANES) for h in range(GLA_HEADS)]

    def step(i, carry):
        ids = [per_step * i + u for u in range(per_step)]
        rows = [pl.ds(pl.multiple_of(ci * c, c), c) for ci in ids]
        vbs = [v_ref[r, :] for r in rows]
        scores = [_dot_nt(jnp.concatenate([qe_ref[r, :] * m for m in q_masks], axis=0), ke_ref[r, :])
                  for r in rows]
        incs = []
        for r, vb in zip(rows, vbs):
            kd = kd_ref[r, :]
            incs.append([_dot_tn(vb[:, sl], kd * q_masks[h]) for h, sl in enumerate(heads)])
        st = [st_ref[sl, :] for sl in heads]
        for ci, r, vb, s, inc in zip(ids, rows, vbs, scores, incs):
            o_inter = _dot_nt(qi_ref[r, :], jnp.concatenate(st, axis=0).astype(BF16))
            s = jnp.where(causal, s, 0.0).astype(BF16)
            dec = dec_ref[ci]
            gate = r_ref[r, :]
            for h, sl in enumerate(heads):
                o = o_inter[:, sl] + _dot(s[h * c:(h + 1) * c, :], vb[:, sl])
                st[h] = st[h] * dec + inc[h]
                ms = jnp.sum(o * o, axis=-1, keepdims=True) * (1.0 / GLA_DV)
                y = o * lax.rsqrt(ms + EPS) * gn[:, sl] * gate[:, sl].astype(F32)
                o_ref[r, sl] = y.astype(o_ref.dtype)
        for h, sl in enumerate(heads):
            st_ref[sl, :] = st[h]
        return carry

    lax.fori_loop(0, seq // (c * per_step), step, 0)


def _gla(qe, ke, qi, kd, dec, av, ar, gn, seq, layer):
    n = qe.shape[0]
    blk = lambda width: pl.BlockSpec((seq, width), lambda b: (b, 0))
    return pl.pallas_call(
        functools.partial(_gla_kernel, seq=seq),
        grid=(n // seq,),
        in_specs=[blk(GLA_QK_PAD), blk(GLA_QK_PAD), blk(GLA_QK_PAD), blk(GLA_QK_PAD),
                  pl.BlockSpec((seq // GLA_CHUNK, 1, GLA_QK_PAD), lambda b: (b, 0, 0)),
                  blk(GLA_V_PAD), blk(GLA_V_PAD), _resident((1, GLA_V_PAD), layer)],
        out_specs=blk(GLA_V_PAD),
        out_shape=jax.ShapeDtypeStruct((n, GLA_V_PAD), BF16),
        scratch_shapes=[pltpu.VMEM((GLA_V_PAD, GLA_QK_PAD), F32)],
        compiler_params=_params(("parallel",), 40),
        name="gla",
    )(qe, ke, qi, kd, dec, av, ar, gn)


def _swa_kernel(qd_ref, kd_ref, vd_ref, gn_ref, o_ref, acc_ref, m_ref, l_ref, out4_ref, *, seq):
    blk = SWA_BLOCK
    res = SWA_RES
    nj = seq // res
    assert nj == blk
    lane = lax.broadcasted_iota(jnp.int32, (1, LANES), 1)
    head0 = lane < HEAD_DIM

    def permuted_masks(groups):
        q_per = blk // groups
        k_per = 2 * blk // groups
        qi = lax.broadcasted_iota(jnp.int32, (blk, 2 * blk), 0)
        ki = lax.broadcasted_iota(jnp.int32, (blk, 2 * blk), 1)
        tq = (qi % q_per) * groups + qi // q_per
        tk = (ki % k_per) * groups + ki // k_per
        two_blocks = (tk >= tq) & (tk <= tq + blk)
        first_block = tk <= tq
        return two_blocks, first_block

    def attend_all(units):
        both = lambda t: jnp.where(head0, t[:blk], t[blk:])
        scores = []
        for qs, kcat, _, _ in units:
            q2 = jnp.concatenate([jnp.where(head0, qs, 0.0), jnp.where(head0, 0.0, qs)], axis=0)
            scores.append(_dot_nt(q2.astype(BF16), kcat.astype(BF16)))
        soft = []
        for s, (_, _, _, mask) in zip(scores, units):
            s = jnp.where(jnp.concatenate([mask, mask], axis=0), s, NEG_BIG)
            m = jnp.max(s, axis=-1, keepdims=True)
            soft.append((m, jnp.exp2(s - m).astype(BF16)))
        outs = []
        for (m, p), (_, _, vcat, _) in zip(soft, units):
            v_ones = jnp.concatenate([vcat.astype(BF16), jnp.ones(vcat.shape, BF16)], axis=1)
            pv = _dot(p, v_ones)
            outs.append((both(m), both(pv[:, LANES:]), both(pv[:, :LANES])))
        return outs

    def merge(old, new):
        (m_old, l_old, acc_old), (m_new, l_new, pv) = old, new
        m_tot = jnp.maximum(m_old, m_new)
        w_old = jnp.exp2(m_old - m_tot)
        w_new = jnp.exp2(m_new - m_tot)
        return m_tot, w_old * l_old + w_new * l_new, w_old * acc_old + w_new * pv

    qi1 = lax.broadcasted_iota(jnp.int32, (blk, blk), 0)
    kj1 = lax.broadcasted_iota(jnp.int32, (blk, blk), 1)
    causal = kj1 <= qi1

    per_step = 4

    def dil16(i, carry):
        units = [(per_step * i + u, s) for u in range(per_step) for s in range(SWA_SLABS)]
        loaded = [(qd_ref[c, s], kd_ref[c, s], vd_ref[c, s]) for c, s in units]
        results = attend_all([(qs, ks, vs, causal) for qs, ks, vs in loaded])
        for (c, s), (m_new, l_new, pv) in zip(units, results):
            m_ref[c, s] = m_new
            l_ref[c, s] = l_new
            acc_ref[c, s] = pv
        return carry

    lax.fori_loop(0, res // per_step, dil16, 0)

    def dilated(groups):
        n_blocks = nj * groups // blk
        n_res = res // groups
        q_per = blk // groups
        k_per = 2 * blk // groups
        two_blocks, first_block = permuted_masks(groups)
        per_step = 2
        nb_bits = n_blocks.bit_length() - 1
        assert n_blocks == 1 << nb_bits

        def step(i, carry):
            loaded = []
            for u in range(per_step):
                it = per_step * i + u
                r = lax.shift_right_logical(it, nb_bits)
                bi = it & (n_blocks - 1)
                q_rows = pl.ds(pl.multiple_of(bi * q_per, q_per), q_per)
                k_rows = pl.ds(pl.multiple_of(jnp.maximum(bi - 1, 0) * q_per, q_per), k_per)
                mask = (two_blocks & (bi > 0)) | (first_block & (bi == 0))
                gather = lambda ref, s, rows, r=r: jnp.concatenate(
                    [ref[a * n_res + r, s, rows, :] for a in range(groups)], axis=0)
                for s in range(SWA_SLABS):
                    old = (gather(m_ref, s, q_rows), gather(l_ref, s, q_rows), gather(acc_ref, s, q_rows))
                    loaded.append((r, s, q_rows, mask, gather(qd_ref, s, q_rows), gather(kd_ref, s, k_rows),
                                   gather(vd_ref, s, k_rows), old))
            news = attend_all([(qs, kcat, vcat, mask) for _, _, _, mask, qs, kcat, vcat, _ in loaded])
            results = [merge(unit[-1], new) for unit, new in zip(loaded, news)]
            for (r, s, q_rows, *_), (m_new, l_new, acc_new) in zip(loaded, results):
                for a in range(groups):
                    piece = slice(a * q_per, (a + 1) * q_per)
                    m_ref[a * n_res + r, s, q_rows, :] = m_new[piece]
                    l_ref[a * n_res + r, s, q_rows, :] = l_new[piece]
                    acc_ref[a * n_res + r, s, q_rows, :] = acc_new[piece]
            return carry

        lax.fori_loop(0, n_res * n_blocks // per_step, step, 0)

    dilated(4)
    dilated(16)

    def finish(a, carry):
        for b in range(4):
            c = 4 * b + a
            outs = [acc_ref[c, s] / l_ref[c, s] for s in range(SWA_SLABS)]
            ss = sum(jnp.sum(o * o, axis=-1, keepdims=True) for o in outs)
            scale = lax.rsqrt(ss * (1.0 / SWA_WIDTH) + EPS)
            for s in range(SWA_SLABS):
                y = outs[s] * scale * gn_ref[:, s * LANES:(s + 1) * LANES]
                out4_ref[s, pl.ds(b, nj, stride=4), :] = y
        for s in range(SWA_SLABS):
            o_ref[s, pl.ds(a, 4 * nj, stride=4), :] = out4_ref[s]
        return carry

    lax.fori_loop(0, 4, finish, 0)


def _swa(cq, ck, cv, gn, seq, layer):
    n = cq.shape[2] * SWA_RES
    nj = seq // SWA_RES
    grouped = pl.BlockSpec((SWA_RES, SWA_SLABS, nj, LANES), lambda b: (0, 0, b, 0))
    state = pltpu.VMEM((SWA_RES, SWA_SLABS, nj, LANES), F32)
    return pl.pallas_call(
        functools.partial(_swa_kernel, seq=seq),
        grid=(n // seq,),
        in_specs=[grouped, grouped, grouped, _resident((1, SWA_WIDTH), layer)],
        out_specs=pl.BlockSpec((SWA_SLABS, seq, LANES), lambda b: (0, b, 0)),
        out_shape=jax.ShapeDtypeStruct((SWA_SLABS, n, LANES), F32),
        scratch_shapes=[state] * 3 + [pltpu.VMEM((SWA_SLABS, 4 * nj, LANES), F32)],
        compiler_params=_params(("parallel",), 48),
        name="swa",
    )(cq, ck, cv, gn)


def _ffn_kernel(x_ref, oa_ref, ob_ref, oc_ref, woa_ref, wob_ref, woc_ref, gn_ref, wup_ref, cw_ref, cb_ref,
                wdn_ref, gfin_ref, out_ref, carry_ref, act_ref, *, tiles_per_seq, final):
    tm = x_ref.shape[0]

    @pl.when(pl.program_id(0) % tiles_per_seq == 0)
    def _():
        carry_ref[...] = jnp.zeros_like(carry_ref)

    oc = jnp.concatenate([oc_ref[s].astype(BF16) for s in range(SWA_SLABS)], axis=1)
    x1 = x_ref[...] + _dot(oa_ref[...], woa_ref[...]) + _dot(ob_ref[...], wob_ref[...]) + _dot(oc, woc_ref[...])
    h = (x1 * _rms_scale(x1) * gn_ref[...]).astype(BF16)

    def conv_up(c0):
        cols = slice(c0, c0 + FFN_CHUNK)
        up = _dot(h, wup_ref[:, cols])
        ext = jnp.concatenate([carry_ref[:, cols], up], axis=0)
        carry_ref[:, cols] = up[tm - SUBLANES:tm, :]
        cw = cw_ref[:, cols]
        return (cb_ref[:, cols] + up * cw[2:3, :] + pltpu.roll(ext, 2, 0)[SUBLANES:, :] * cw[0:1, :]
                + pltpu.roll(ext, 1, 0)[SUBLANES:, :] * cw[1:2, :])

    for j in range(FFN_DIM // FFN_CHUNK):
        gate = _gelu_tanh(conv_up(FFN_DIM + j * FFN_CHUNK))
        val = conv_up(j * FFN_CHUNK)
        act_ref[:, j * FFN_CHUNK:(j + 1) * FFN_CHUNK] = (gate * val).astype(BF16)
    x2 = x1 + _dot(act_ref[...], wdn_ref[...])
    if final:
        x2 = x2 * _rms_scale(x2) * gfin_ref[...]
    out_ref[...] = x2


def _ffn(x, oa, ob, oc, woa, wob, woc, gn, wup, cw, cb, wdn, gfin, seq, layer, final):
    n = x.shape[0]
    tm = TOKEN_TILE
    row = lambda width: pl.BlockSpec((tm, width), lambda i: (i, 0))
    return pl.pallas_call(
        functools.partial(_ffn_kernel, tiles_per_seq=seq // tm, final=final),
        grid=(n // tm,),
        in_specs=[row(D_MODEL), row(GLA_V_PAD), row(LRU_WIDTH),
                  pl.BlockSpec((SWA_SLABS, tm, LANES), lambda i: (0, i, 0)),
                  _resident((GLA_V_PAD, D_MODEL), layer), _resident((LRU_WIDTH, D_MODEL), layer),
                  _resident((SWA_WIDTH, D_MODEL), layer), _resident((1, D_MODEL), layer),
                  _resident((D_MODEL, 2 * FFN_DIM), layer), _resident((FFN_CONV, 2 * FFN_DIM), layer),
                  _resident((1, 2 * FFN_DIM), layer), _resident((FFN_DIM, D_MODEL), layer), _resident((1, D_MODEL))],
        out_specs=row(D_MODEL),
        out_shape=jax.ShapeDtypeStruct((n, D_MODEL), F32),
        scratch_shapes=[pltpu.VMEM((SUBLANES, 2 * FFN_DIM), F32), pltpu.VMEM((tm, FFN_DIM), BF16)],
        compiler_params=_params(("arbitrary",), 56),
        name="ffn",
    )(x, oa, ob, oc, woa, wob, woc, gn, wup, cw, cb, wdn, gfin)


def _rope_tables(seq):
    half = ROPE_DIM // 2
    inv = ROPE_THETA ** (-jnp.arange(0, ROPE_DIM, 2, dtype=F32) / ROPE_DIM)
    ang = jnp.arange(seq, dtype=F32)[:, None] * inv[None, :]
    cos, sin = jnp.cos(ang), jnp.sin(ang)
    ones = jnp.ones((seq, HEAD_DIM - ROPE_DIM), F32)
    zeros = jnp.zeros((seq, HEAD_DIM - ROPE_DIM), F32)
    zh = jnp.zeros((seq, half), F32)
    per_head = lambda parts: jnp.tile(jnp.concatenate(parts, axis=1), (1, LANES // HEAD_DIM))
    return per_head([cos, cos, ones]), per_head([-sin, zh, zeros]), per_head([zh, sin, zeros])


def _pad_heads(w, axis):
    shape = list(w.shape)
    shape[axis:axis + 1] = [GLA_HEADS, GLA_DV]
    w = w.reshape(shape)
    pad = [(0, 0)] * w.ndim
    pad[axis + 1] = (0, LANES - GLA_DV)
    shape[axis:axis + 2] = [GLA_V_PAD]
    return jnp.pad(w, pad).reshape(shape)


def _pack_w_in(w):
    z = lambda width: jnp.zeros(w.shape[:-1] + (width,), w.dtype)
    o_v = 2 * GLA_QK
    o_r = o_v + GLA_WIDTH
    o_glr = o_r + GLA_WIDTH
    return jnp.concatenate([
        w[..., 0:GLA_QK], z(GLA_QK_PAD - GLA_QK), w[..., GLA_QK:o_v], z(GLA_QK_PAD - GLA_QK),
        _pad_heads(w[..., o_v:o_r], 2), _pad_heads(w[..., o_r:o_glr], 2),
        w[..., o_glr + GLA_GATE_RANK:], w[..., o_glr:o_glr + GLA_GATE_RANK], z(GLR_PAD - GLA_GATE_RANK)],
        axis=-1).astype(BF16)


def _block_diag(w):
    layers, nb, bs, _ = w.shape
    eye = jnp.eye(nb, dtype=w.dtype)
    return jnp.einsum('lnij,nm->lnimj', w, eye).reshape(layers, nb * bs, nb * bs)


def kernel(x, norm_mix, w_in, gla_w_gate, gla_b_gate, gla_norm, lru_conv_w, lru_conv_b, lru_w_a, lru_b_a,
           lru_w_x, lru_b_x, lru_lambda, lru_norm, swa_norm, w_out, norm_ffn, ffn_w_up, ffn_conv_w, ffn_conv_b,
           ffn_w_down, norm_final):
    batch, seq, _ = x.shape
    depth = w_in.shape[0]
    assert seq % TOKEN_TILE == 0 and seq == SWA_BLOCK * SWA_RES
    cos, sa, sb = _rope_tables(seq)
    xf = x.reshape(batch * seq, D_MODEL)
    rows = lambda v: v.reshape(depth, 1, -1)
    wg = jnp.pad(gla_w_gate, ((0, 0), (0, GLR_PAD - GLA_GATE_RANK), (0, GLA_QK_PAD - GLA_QK))).astype(BF16)
    bg = rows(jnp.pad(gla_b_gate, ((0, 0), (0, GLA_QK_PAD - GLA_QK))))
    w_in_p = _pack_w_in(w_in)
    gla_gn = rows(_pad_heads(jnp.tile(gla_norm, (1, GLA_HEADS)), 1))
    wa = _block_diag(lru_w_a).astype(BF16)
    wx = _block_diag(lru_w_x).astype(BF16)
    wo = w_out.astype(BF16)
    wo_a = _pad_heads(wo[:, :GLA_WIDTH], 1)
    wo_b = wo[:, GLA_WIDTH:GLA_WIDTH + LRU_WIDTH]
    wo_c = wo[:, GLA_WIDTH + LRU_WIDTH:]
    w_up = ffn_w_up.astype(BF16)
    w_dn = ffn_w_down.astype(BF16)
    vecs = [rows(v) for v in (norm_mix, lru_conv_b, lru_b_a, lru_b_x, lru_lambda, lru_norm, swa_norm, norm_ffn,
                              ffn_conv_b)]
    g_mix, cb_lru, ba, bx, lam, g_lru, g_swa, g_ffn, cb_ffn = vecs
    g_fin = norm_final.reshape(1, -1)
    lru = (lru_conv_w, cb_lru, wa, ba, wx, bx, lam, g_lru)
    for l in range(depth):
        qe, ke, qi, kd, dec, av, ar, o_b, cq, ck, cv = _inproj(xf, g_mix, w_in_p, wg, bg, cos, sa, sb, lru, seq, l)
        o_a = _gla(qe, ke, qi, kd, dec, av, ar, gla_gn, seq, l)
        o_c = _swa(cq, ck, cv, g_swa, seq, l)
        xf = _ffn(xf, o_a, o_b, o_c, wo_a, wo_b, wo_c, g_ffn, w_up, ffn_conv_w, cb_ffn, w_dn, g_fin, seq, l,
                  l == depth - 1)
    return xf.reshape(batch, seq, D_MODEL)
```

```python
import functools

import jax
import jax.numpy as jnp
import numpy as np
from jax import lax
from jax.experimental import pallas as pl
from jax.experimental.pallas import tpu as pltpu

F32 = jnp.float32
BF16 = jnp.bfloat16

D_MODEL = 1024
EPS = 1e-6
HEAD_DIM = 64
LANES = 128
SUBLANES = 8
GLA_HEADS = 4
GLA_DK = 48
GLA_DV = 96
GLA_QK = GLA_HEADS * GLA_DK
GLA_QK_PAD = 256
GLA_WIDTH = GLA_HEADS * GLA_DV
GLA_V_PAD = GLA_HEADS * LANES
GLA_GATE_RANK = 16
GLA_GATE_TEMP = 16.0
GLA_CHUNK = 64
LRU_WIDTH = 256
LRU_BLOCKS = 8
LRU_C = 8.0
LRU_CONV = 4
SWA_HEADS = 6
SWA_WIDTH = SWA_HEADS * HEAD_DIM
SWA_BLOCK = 128
SWA_SLABS = SWA_WIDTH // LANES
SWA_RES = 16
ROPE_THETA = 500000.0
ROPE_DIM = HEAD_DIM // 4
FFN_DIM = 2816
FFN_CONV = 3
FFN_CHUNK = 256
GLR_PAD = 128

_C_AQ = 0
_C_AK = _C_AQ + GLA_QK_PAD
_C_AV = _C_AK + GLA_QK_PAD
_C_AR = _C_AV + GLA_V_PAD
_C_BG = _C_AR + GLA_V_PAD
_C_BI = _C_BG + LRU_WIDTH
_C_CQ = _C_BI + LRU_WIDTH
_C_CK = _C_CQ + SWA_WIDTH
_C_CV = _C_CK + SWA_WIDTH
_C_GLR = _C_CV + SWA_WIDTH
IN_COLS_PACKED = _C_GLR + GLR_PAD

TOKEN_TILE = 512
V7X_VMEM_BYTES = 64 * 1024 * 1024
NEG_BIG = -1e30
LOG2E = float(np.log2(np.e))


def _params(semantics, vmem_mib):
    assert vmem_mib * 1024 * 1024 < V7X_VMEM_BYTES
    return pltpu.CompilerParams(dimension_semantics=semantics,
                                vmem_limit_bytes=vmem_mib * 1024 * 1024)


def _resident(shape, layer=None):
    nd = len(shape)
    if layer is None:
        return pl.BlockSpec(shape, lambda *_: (0,) * nd, pipeline_mode=pl.Buffered(1))
    return pl.BlockSpec((None,) + tuple(shape), lambda *_: (layer,) + (0,) * nd, pipeline_mode=pl.Buffered(1))


def _rms_scale(x):
    return lax.rsqrt(jnp.mean(x * x, axis=-1, keepdims=True) + EPS)


def _gelu_tanh(x):
    c = float(np.sqrt(2.0 / np.pi))
    return x * (0.5 + 0.5 * jnp.tanh(x * (c + (0.044715 * c) * (x * x))))


def _dot(a, b):
    return jnp.dot(a, b, preferred_element_type=F32)


def _dot_nt(a, b):
    return lax.dot_general(a, b, (((1,), (1,)), ((), ())), preferred_element_type=F32)


def _dot_tn(a, b):
    return lax.dot_general(a, b, (((0,), (0,)), ((), ())), preferred_element_type=F32)


def _inproj_kernel(x_ref, gn_ref, w_ref, wg_ref, bg_ref, cos_ref, sa_ref, sb_ref,
                   cw_ref, cb_ref, wa_ref, ba_ref, wx_ref, bx_ref, lam_ref, gnb_ref,
                   qe_ref, ke_ref, qi_ref, kd_ref, dec_ref, av_ref, ar_ref, ob_ref, cq_ref, ck_ref, cv_ref,
                   xtail_ref, hc_ref, rg_ref, rg4_ref, *, tiles_per_seq):
    tm = x_ref.shape[0]
    c = GLA_CHUNK

    @pl.when(pl.program_id(0) % tiles_per_seq == 0)
    def _():
        xtail_ref[...] = jnp.zeros_like(xtail_ref)
        hc_ref[...] = jnp.zeros_like(hc_ref)

    x = x_ref[...]
    h = (x * _rms_scale(x) * gn_ref[...]).astype(BF16)

    def proj(c0, width):
        return _dot(h, w_ref[:, c0:c0 + width])

    tile = 256
    sub = c
    row_sub = lax.broadcasted_iota(jnp.int32, (sub, 1), 0)
    silu = lambda t: (t * jax.nn.sigmoid(t)).astype(BF16)

    def scan_steps(update):
        shift = 1
        while shift < sub:
            if shift < SUBLANES:
                shifted = lambda v, fill, s=shift: jnp.where(row_sub >= s, pltpu.roll(v, s, 0), fill)
            else:
                shifted = lambda v, fill, s=shift: jnp.concatenate(
                    [jnp.full((s, v.shape[1]), fill, F32), v[:sub - s]], axis=0)
            update(shifted)
            shift *= 2

    p_glr = proj(_C_GLR, GLR_PAD)
    p_bi = proj(_C_BI, LRU_WIDTH)
    z = _dot(p_glr.astype(BF16), wg_ref[...]) + bg_ref[...]
    p_aq = proj(_C_AQ, GLA_QK_PAD)

    ext = jnp.concatenate([xtail_ref[...], p_bi], axis=0)
    xtail_ref[...] = p_bi[tm - SUBLANES:tm, :]
    cw = cw_ref[...]
    u = cb_ref[...] + p_bi * cw[LRU_CONV - 1:LRU_CONV, :]
    for kk in range(LRU_CONV - 1):
        u = u + pltpu.roll(ext, LRU_CONV - 1 - kk, 0)[SUBLANES:, :] * cw[kk:kk + 1, :]
    ub = u.astype(BF16)
    rg_pre = _dot(ub, wa_ref[...]) + ba_ref[...]
    ig_pre = _dot(ub, wx_ref[...]) + bx_ref[...]
    p_ak = proj(_C_AK, GLA_QK_PAD)
    p_bg = proj(_C_BG, LRU_WIDTH)

    def value_tile(k):
        if k == 1:
            av_ref[:, :tile] = proj(_C_AV, tile).astype(BF16)
        elif k == 3:
            av_ref[:, tile:] = proj(_C_AV + tile, tile).astype(BF16)
        elif k == 5:
            ar_ref[:, :tile] = silu(proj(_C_AR, tile))
        elif k == 7:
            ar_ref[:, tile:] = silu(proj(_C_AR + tile, tile))

    for k in range(tm // sub):
        rows = slice(k * sub, (k + 1) * sub)
        zk = z[rows]
        bsum = [(jnp.minimum(zk, 0.0) - jnp.log(1.0 + jnp.exp(-jnp.abs(zk)))) * (1.0 / GLA_GATE_TEMP)]

        def add_back(shifted, bsum=bsum):
            bsum[0] = bsum[0] + shifted(bsum[0], 0.0)

        scan_steps(add_back)
        b = bsum[0]
        b_mid = b[sub // 2 - 1:sub // 2, :]
        b_last = b[sub - 1:sub, :]
        q = p_aq[rows] * (GLA_DK ** -0.5)
        kx = p_ak[rows]
        qe_ref[rows, :] = (q * jnp.exp(b - b_mid)).astype(BF16)
        ke_ref[rows, :] = (kx * jnp.exp(b_mid - b)).astype(BF16)
        qi_ref[rows, :] = (q * jnp.exp(b)).astype(BF16)
        kd_ref[rows, :] = (kx * jnp.exp(b_last - b)).astype(BF16)
        dec_ref[k] = jnp.exp(b_last)
        value_tile(k)

    neg_lam = -lam_ref[...]
    softplus = jnp.maximum(neg_lam, 0.0) + jnp.log(1.0 + jnp.exp(-jnp.abs(neg_lam)))
    carry = hc_ref[...]
    swa_proj = {}
    for k in range(tm // sub):
        rows = slice(k * sub, (k + 1) * sub)
        log_a = (-LRU_C) * jax.nn.sigmoid(rg_pre[rows]) * softplus
        th = jnp.tanh(log_a)
        state = [jnp.exp(log_a), jnp.sqrt(-2.0 * th / (1.0 - th)) * (jax.nn.sigmoid(ig_pre[rows]) * u[rows])]

        def combine(shifted, state=state):
            a, hs = state
            state[1] = hs + a * shifted(hs, 0.0)
            state[0] = a * shifted(a, 1.0)

        scan_steps(combine)
        a, hs = state
        hs = hs + a * carry
        carry = hs[sub - 1:sub, :]
        y = hs * _gelu_tanh(p_bg[rows])
        ob_ref[rows, :] = (y * _rms_scale(y) * gnb_ref[...]).astype(ob_ref.dtype)
        if k % 2 == 1:
            swa_proj[k // 2] = proj(_C_CQ + (k // 2) * tile, tile)
    hc_ref[...] = carry
    p_c0, p_c1, p_c2, p_c3 = (swa_proj[i] for i in range(4))

    cos = cos_ref[...]
    sa = sa_ref[...]
    sb = sb_ref[...]

    def rope(t):
        return t * cos + pltpu.roll(t, LANES - ROPE_DIM // 2, 1) * sa + pltpu.roll(t, ROPE_DIM // 2, 1) * sb

    def regroup(out_ref, s, slot, val):
        rg_ref[slot] = val
        for a in range(4):
            rg4_ref[slot, a] = rg_ref[slot, pl.ds(a, tm // 4, stride=4), :]
        for a in range(4):
            for b in range(4):
                out_ref[4 * b + a, s] = rg4_ref[slot, a, pl.ds(b, tm // SWA_RES, stride=4), :]

    q_scale = HEAD_DIM ** -0.5 * LOG2E
    regroup(cq_ref, 0, 0, rope(p_c0[:, :LANES] * q_scale))
    regroup(cq_ref, 1, 1, rope(p_c0[:, LANES:] * q_scale))
    p_c4 = proj(_C_CQ + 4 * tile, LANES)
    regroup(cq_ref, 2, 2, rope(p_c1[:, :LANES] * q_scale))
    regroup(ck_ref, 0, 3, rope(p_c1[:, LANES:]))
    regroup(ck_ref, 1, 4, rope(p_c2[:, :LANES]))
    regroup(ck_ref, 2, 5, rope(p_c2[:, LANES:]))
    regroup(cv_ref, 0, 6, p_c3[:, :LANES])
    regroup(cv_ref, 1, 7, p_c3[:, LANES:])
    regroup(cv_ref, 2, 8, p_c4)


def _inproj(x, gn, w, wg, bg, cos, sa, sb, lru, seq, layer):
    n = x.shape[0]
    tm = TOKEN_TILE
    tiles_per_seq = seq // tm
    row = lambda width: pl.BlockSpec((tm, width), lambda i: (i, 0))
    grouped = pl.BlockSpec((SWA_RES, SWA_SLABS, tm // SWA_RES, LANES), lambda i: (0, 0, i, 0))
    table = pl.BlockSpec((tm, LANES), lambda i: (i % tiles_per_seq, 0))
    vec = _resident((1, LRU_WIDTH), layer)
    mat = _resident((LRU_WIDTH, LRU_WIDTH), layer)
    sds = jax.ShapeDtypeStruct
    grouped_shape = sds((SWA_RES, SWA_SLABS, n // SWA_RES, LANES), F32)
    out_shapes = (
        sds((n, GLA_QK_PAD), BF16), sds((n, GLA_QK_PAD), BF16), sds((n, GLA_QK_PAD), BF16),
        sds((n, GLA_QK_PAD), BF16), sds((n // GLA_CHUNK, 1, GLA_QK_PAD), F32),
        sds((n, GLA_V_PAD), BF16), sds((n, GLA_V_PAD), BF16), sds((n, LRU_WIDTH), BF16),
        grouped_shape, grouped_shape, grouped_shape,
    )
    return pl.pallas_call(
        functools.partial(_inproj_kernel, tiles_per_seq=tiles_per_seq),
        grid=(n // tm,),
        in_specs=[row(D_MODEL), _resident((1, D_MODEL), layer), _resident((D_MODEL, IN_COLS_PACKED), layer),
                  _resident((GLR_PAD, GLA_QK_PAD), layer), _resident((1, GLA_QK_PAD), layer), table, table, table,
                  _resident((LRU_CONV, LRU_WIDTH), layer), vec, mat, vec, mat, vec, vec, vec],
        out_specs=(row(GLA_QK_PAD), row(GLA_QK_PAD), row(GLA_QK_PAD), row(GLA_QK_PAD),
                   pl.BlockSpec((tm // GLA_CHUNK, 1, GLA_QK_PAD), lambda i: (i, 0, 0)),
                   row(GLA_V_PAD), row(GLA_V_PAD), row(LRU_WIDTH), grouped, grouped, grouped),
        out_shape=out_shapes,
        scratch_shapes=[pltpu.VMEM((SUBLANES, LRU_WIDTH), F32), pltpu.VMEM((1, LRU_WIDTH), F32),
                        pltpu.VMEM((3 * SWA_SLABS, tm, LANES), F32),
                        pltpu.VMEM((3 * SWA_SLABS, 4, tm // 4, LANES), F32)],
        compiler_params=_params(("arbitrary",), 48),
        name="inproj",
    )(x, gn, w, wg, bg, cos, sa, sb, *lru)


def _gla_kernel(qe_ref, ke_ref, qi_ref, kd_ref, dec_ref, v_ref, r_ref, gn_ref, o_ref, st_ref, *, seq):
    c = GLA_CHUNK
    st_ref[...] = jnp.zeros_like(st_ref)

    lane_q = lax.broadcasted_iota(jnp.int32, (1, GLA_QK_PAD), 1)
    q_masks = [((lane_q >= h * GLA_DK) & (lane_q < (h + 1) * GLA_DK)).astype(BF16) for h in range(GLA_HEADS)]
    ri = lax.broadcasted_iota(jnp.int32, (GLA_HEADS * c, c), 0) & (c - 1)
    ci = lax.broadcasted_iota(jnp.int32, (GLA_HEADS * c, c), 1)
    causal = ci <= ri
    gn = gn_ref[...]

    per_step = 32
    heads = [slice(h * LANES, (h + 1) * LANES) for h in range(GLA_HEADS)]

    def step(i, carry):
        ids = [per_step * i + u for u in range(per_step)]
        rows = [pl.ds(pl.multiple_of(ci * c, c), c) for ci in ids]
        vbs = [v_ref[r, :] for r in rows]
        scores = [_dot_nt(jnp.concatenate([qe_ref[r, :] * m for m in q_masks], axis=0), ke_ref[r, :])
                  for r in rows]
        incs = []
        for r, vb in zip(rows, vbs):
            kd = kd_ref[r, :]
            incs.append([_dot_tn(vb[:, sl], kd * q_masks[h]) for h, sl in enumerate(heads)])
        st = [st_ref[sl, :] for sl in heads]
        for ci, r, vb, s, inc in zip(ids, rows, vbs, scores, incs):
            o_inter = _dot_nt(qi_ref[r, :], jnp.concatenate(st, axis=0).astype(BF16))
            s = jnp.where(causal, s, 0.0).astype(BF16)
            dec = dec_ref[ci]
            gate = r_ref[r, :]
            for h, sl in enumerate(heads):
                o = o_inter[:, sl] + _dot(s[h * c:(h + 1) * c, :], vb[:, sl])
                st[h] = st[h] * dec + inc[h]
                ms = jnp.sum(o * o, axis=-1, keepdims=True) * (1.0 / GLA_DV)
                y = o * lax.rsqrt(ms + EPS) * gn[:, sl] * gate[:, sl].astype(F32)
                o_ref[r, sl] = y.astype(o_ref.dtype)
        for h, sl in enumerate(heads):
            st_ref[sl, :] = st[h]
        return carry

    lax.fori_loop(0, seq // (c * per_step), step, 0)


def _gla(qe, ke, qi, kd, dec, av, ar, gn, seq, layer):
    n = qe.shape[0]
    blk = lambda width: pl.BlockSpec((seq, width), lambda b: (b, 0))
    return pl.pallas_call(
        functools.partial(_gla_kernel, seq=seq),
        grid=(n // seq,),
        in_specs=[blk(GLA_QK_PAD), blk(GLA_QK_PAD), blk(GLA_QK_PAD), blk(GLA_QK_PAD),
                  pl.BlockSpec((seq // GLA_CHUNK, 1, GLA_QK_PAD), lambda b: (b, 0, 0)),
                  blk(GLA_V_PAD), blk(GLA_V_PAD), _resident((1, GLA_V_PAD), layer)],
        out_specs=blk(GLA_V_PAD),
        out_shape=jax.ShapeDtypeStruct((n, GLA_V_PAD), BF16),
        scratch_shapes=[pltpu.VMEM((GLA_V_PAD, GLA_QK_PAD), F32)],
        compiler_params=_params(("parallel",), 40),
        name="gla",
    )(qe, ke, qi, kd, dec, av, ar, gn)


def _swa_kernel(qd_ref, kd_ref, vd_ref, gn_ref, o_ref, acc_ref, m_ref, l_ref, out4_ref, *, seq):
    blk = SWA_BLOCK
    res = SWA_RES
    nj = seq // res
    assert nj == blk
    lane = lax.broadcasted_iota(jnp.int32, (1, LANES), 1)
    head0 = lane < HEAD_DIM

    def permuted_masks(groups):
        q_per = blk // groups
        k_per = 2 * blk // groups
        qi = lax.broadcasted_iota(jnp.int32, (blk, 2 * blk), 0)
        ki = lax.broadcasted_iota(jnp.int32, (blk, 2 * blk), 1)
        tq = (qi % q_per) * groups + qi // q_per
        tk = (ki % k_per) * groups + ki // k_per
        two_blocks = (tk >= tq) & (tk <= tq + blk)
        first_block = tk <= tq
        return two_blocks, first_block

    def attend_all(units):
        both = lambda t: jnp.where(head0, t[:blk], t[blk:])
        scores = []
        for qs, kcat, _, _ in units:
            q2 = jnp.concatenate([jnp.where(head0, qs, 0.0), jnp.where(head0, 0.0, qs)], axis=0)
            scores.append(_dot_nt(q2.astype(BF16), kcat.astype(BF16)))
        soft = []
        for s, (_, _, _, mask) in zip(scores, units):
            s = jnp.where(jnp.concatenate([mask, mask], axis=0), s, NEG_BIG)
            m = jnp.max(s, axis=-1, keepdims=True)
            soft.append((m, jnp.exp2(s - m).astype(BF16)))
        outs = []
        for (m, p), (_, _, vcat, _) in zip(soft, units):
            v_ones = jnp.concatenate([vcat.astype(BF16), jnp.ones(vcat.shape, BF16)], axis=1)
            pv = _dot(p, v_ones)
            outs.append((both(m), both(pv[:, LANES:]), both(pv[:, :LANES])))
        return outs

    def merge(old, new):
        (m_old, l_old, acc_old), (m_new, l_new, pv) = old, new
        m_tot = jnp.maximum(m_old, m_new)
        w_old = jnp.exp2(m_old - m_tot)
        w_new = jnp.exp2(m_new - m_tot)
        return m_tot, w_old * l_old + w_new * l_new, w_old * acc_old + w_new * pv

    qi1 = lax.broadcasted_iota(jnp.int32, (blk, blk), 0)
    kj1 = lax.broadcasted_iota(jnp.int32, (blk, blk), 1)
    causal = kj1 <= qi1

    per_step = 4

    def dil16(i, carry):
        units = [(per_step * i + u, s) for u in range(per_step) for s in range(SWA_SLABS)]
        loaded = [(qd_ref[c, s], kd_ref[c, s], vd_ref[c, s]) for c, s in units]
        results = attend_all([(qs, ks, vs, causal) for qs, ks, vs in loaded])
        for (c, s), (m_new, l_new, pv) in zip(units, results):
            m_ref[c, s] = m_new
            l_ref[c, s] = l_new
            acc_ref[c, s] = pv
        return carry

    lax.fori_loop(0, res // per_step, dil16, 0)

    def dilated(groups):
        n_blocks = nj * groups // blk
        n_res = res // groups
        q_per = blk // groups
        k_per = 2 * blk // groups
        two_blocks, first_block = permuted_masks(groups)
        per_step = 2
        nb_bits = n_blocks.bit_length() - 1
        assert n_blocks == 1 << nb_bits

        def step(i, carry):
            loaded = []
            for u in range(per_step):
                it = per_step * i + u
                r = lax.shift_right_logical(it, nb_bits)
                bi = it & (n_blocks - 1)
                q_rows = pl.ds(pl.multiple_of(bi * q_per, q_per), q_per)
                k_rows = pl.ds(pl.multiple_of(jnp.maximum(bi - 1, 0) * q_per, q_per), k_per)
                mask = (two_blocks & (bi > 0)) | (first_block & (bi == 0))
                gather = lambda ref, s, rows, r=r: jnp.concatenate(
                    [ref[a * n_res + r, s, rows, :] for a in range(groups)], axis=0)
                for s in range(SWA_SLABS):
                    old = (gather(m_ref, s, q_rows), gather(l_ref, s, q_rows), gather(acc_ref, s, q_rows))
                    loaded.append((r, s, q_rows, mask, gather(qd_ref, s, q_rows), gather(kd_ref, s, k_rows),
                                   gather(vd_ref, s, k_rows), old))
            news = attend_all([(qs, kcat, vcat, mask) for _, _, _, mask, qs, kcat, vcat, _ in loaded])
            results = [merge(unit[-1], new) for unit, new in zip(loaded, news)]
            for (r, s, q_rows, *_), (m_new, l_new, acc_new) in zip(loaded, results):
                for a in range(groups):
                    piece = slice(a * q_per, (a + 1) * q_per)
                    m_ref[a * n_res + r, s, q_rows, :] = m_new[piece]
                    l_ref[a * n_res + r, s, q_rows, :] = l_new[piece]
                    acc_ref[a * n_res + r, s, q_rows, :] = acc_new[piece]
            return carry

        lax.fori_loop(0, n_res * n_blocks // per_step, step, 0)

    dilated(4)
    dilated(16)

    def finish(a, carry):
        for b in range(4):
            c = 4 * b + a
            outs = [acc_ref[c, s] / l_ref[c, s] for s in range(SWA_SLABS)]
            ss = sum(jnp.sum(o * o, axis=-1, keepdims=True) for o in outs)
            scale = lax.rsqrt(ss * (1.0 / SWA_WIDTH) + EPS)
            for s in range(SWA_SLABS):
                y = outs[s] * scale * gn_ref[:, s * LANES:(s + 1) * LANES]
                out4_ref[s, pl.ds(b, nj, stride=4), :] = y
        for s in range(SWA_SLABS):
            o_ref[s, pl.ds(a, 4 * nj, stride=4), :] = out4_ref[s]
        return carry

    lax.fori_loop(0, 4, finish, 0)


def _swa(cq, ck, cv, gn, seq, layer):
    n = cq.shape[2] * SWA_RES
    nj = seq // SWA_RES
    grouped = pl.BlockSpec((SWA_RES, SWA_SLABS, nj, LANES), lambda b: (0, 0, b, 0))
    state = pltpu.VMEM((SWA_RES, SWA_SLABS, nj, LANES), F32)
    return pl.pallas_call(
        functools.partial(_swa_kernel, seq=seq),
        grid=(n // seq,),
        in_specs=[grouped, grouped, grouped, _resident((1, SWA_WIDTH), layer)],
        out_specs=pl.BlockSpec((SWA_SLABS, seq, LANES), lambda b: (0, b, 0)),
        out_shape=jax.ShapeDtypeStruct((SWA_SLABS, n, LANES), F32),
        scratch_shapes=[state] * 3 + [pltpu.VMEM((SWA_SLABS, 4 * nj, LANES), F32)],
        compiler_params=_params(("parallel",), 48),
        name="swa",
    )(cq, ck, cv, gn)


def _ffn_kernel(x_ref, oa_ref, ob_ref, oc_ref, woa_ref, wob_ref, woc_ref, gn_ref, wup_ref, cw_ref, cb_ref,
                wdn_ref, gfin_ref, out_ref, carry_ref, act_ref, *, tiles_per_seq, final):
    tm = x_ref.shape[0]

    @pl.when(pl.program_id(0) % tiles_per_seq == 0)
    def _():
        carry_ref[...] = jnp.zeros_like(carry_ref)

    oc = jnp.concatenate([oc_ref[s].astype(BF16) for s in range(SWA_SLABS)], axis=1)
    x1 = x_ref[...] + _dot(oa_ref[...], woa_ref[...]) + _dot(ob_ref[...], wob_ref[...]) + _dot(oc, woc_ref[...])
    h = (x1 * _rms_scale(x1) * gn_ref[...]).astype(BF16)

    def conv_up(c0):
        cols = slice(c0, c0 + FFN_CHUNK)
        up = _dot(h, wup_ref[:, cols])
        ext = jnp.concatenate([carry_ref[:, cols], up], axis=0)
        carry_ref[:, cols] = up[tm - SUBLANES:tm, :]
        cw = cw_ref[:, cols]
        return (cb_ref[:, cols] + up * cw[2:3, :] + pltpu.roll(ext, 2, 0)[SUBLANES:, :] * cw[0:1, :]
                + pltpu.roll(ext, 1, 0)[SUBLANES:, :] * cw[1:2, :])

    for j in range(FFN_DIM // FFN_CHUNK):
        gate = _gelu_tanh(conv_up(FFN_DIM + j * FFN_CHUNK))
        val = conv_up(j * FFN_CHUNK)
        act_ref[:, j * FFN_CHUNK:(j + 1) * FFN_CHUNK] = (gate * val).astype(BF16)
    x2 = x1 + _dot(act_ref[...], wdn_ref[...])
    if final:
        x2 = x2 * _rms_scale(x2) * gfin_ref[...]
    out_ref[...] = x2


def _ffn(x, oa, ob, oc, woa, wob, woc, gn, wup, cw, cb, wdn, gfin, seq, layer, final):
    n = x.shape[0]
    tm = TOKEN_TILE
    row = lambda width: pl.BlockSpec((tm, width), lambda i: (i, 0))
    return pl.pallas_call(
        functools.partial(_ffn_kernel, tiles_per_seq=seq // tm, final=final),
        grid=(n // tm,),
        in_specs=[row(D_MODEL), row(GLA_V_PAD), row(LRU_WIDTH),
                  pl.BlockSpec((SWA_SLABS, tm, LANES), lambda i: (0, i, 0)),
                  _resident((GLA_V_PAD, D_MODEL), layer), _resident((LRU_WIDTH, D_MODEL), layer),
                  _resident((SWA_WIDTH, D_MODEL), layer), _resident((1, D_MODEL), layer),
                  _resident((D_MODEL, 2 * FFN_DIM), layer), _resident((FFN_CONV, 2 * FFN_DIM), layer),
                  _resident((1, 2 * FFN_DIM), layer), _resident((FFN_DIM, D_MODEL), layer), _resident((1, D_MODEL))],
        out_specs=row(D_MODEL),
        out_shape=jax.ShapeDtypeStruct((n, D_MODEL), F32),
        scratch_shapes=[pltpu.VMEM((SUBLANES, 2 * FFN_DIM), F32), pltpu.VMEM((tm, FFN_DIM), BF16)],
        compiler_params=_params(("arbitrary",), 56),
        name="ffn",
    )(x, oa, ob, oc, woa, wob, woc, gn, wup, cw, cb, wdn, gfin)


def _rope_tables(seq):
    half = ROPE_DIM // 2
    inv = ROPE_THETA ** (-jnp.arange(0, ROPE_DIM, 2, dtype=F32) / ROPE_DIM)
    ang = jnp.arange(seq, dtype=F32)[:, None] * inv[None, :]
    cos, sin = jnp.cos(ang), jnp.sin(ang)
    ones = jnp.ones((seq, HEAD_DIM - ROPE_DIM), F32)
    zeros = jnp.zeros((seq, HEAD_DIM - ROPE_DIM), F32)
    zh = jnp.zeros((seq, half), F32)
    per_head = lambda parts: jnp.tile(jnp.concatenate(parts, axis=1), (1, LANES // HEAD_DIM))
    return per_head([cos, cos, ones]), per_head([-sin, zh, zeros]), per_head([zh, sin, zeros])


def _pad_heads(w, axis):
    shape = list(w.shape)
    shape[axis:axis + 1] = [GLA_HEADS, GLA_DV]
    w = w.reshape(shape)
    pad = [(0, 0)] * w.ndim
    pad[axis + 1] = (0, LANES - GLA_DV)
    shape[axis:axis + 2] = [GLA_V_PAD]
    return jnp.pad(w, pad).reshape(shape)


def _pack_w_in(w):
    z = lambda width: jnp.zeros(w.shape[:-1] + (width,), w.dtype)
    o_v = 2 * GLA_QK
    o_r = o_v + GLA_WIDTH
    o_glr = o_r + GLA_WIDTH
    return jnp.concatenate([
        w[..., 0:GLA_QK], z(GLA_QK_PAD - GLA_QK), w[..., GLA_QK:o_v], z(GLA_QK_PAD - GLA_QK),
        _pad_heads(w[..., o_v:o_r], 2), _pad_heads(w[..., o_r:o_glr], 2),
        w[..., o_glr + GLA_GATE_RANK:], w[..., o_glr:o_glr + GLA_GATE_RANK], z(GLR_PAD - GLA_GATE_RANK)],
        axis=-1).astype(BF16)


def _block_diag(w):
    layers, nb, bs, _ = w.shape
    eye = jnp.eye(nb, dtype=w.dtype)
    return jnp.einsum('lnij,nm->lnimj', w, eye).reshape(layers, nb * bs, nb * bs)


def kernel(x, norm_mix, w_in, gla_w_gate, gla_b_gate, gla_norm, lru_conv_w, lru_conv_b, lru_w_a, lru_b_a,
           lru_w_x, lru_b_x, lru_lambda, lru_norm, swa_norm, w_out, norm_ffn, ffn_w_up, ffn_conv_w, ffn_conv_b,
           ffn_w_down, norm_final):
    batch, seq, _ = x.shape
    depth = w_in.shape[0]
    assert seq % TOKEN_TILE == 0 and seq == SWA_BLOCK * SWA_RES
    cos, sa, sb = _rope_tables(seq)
    xf = x.reshape(batch * seq, D_MODEL)
    rows = lambda v: v.reshape(depth, 1, -1)
    wg = jnp.pad(gla_w_gate, ((0, 0), (0, GLR_PAD - GLA_GATE_RANK), (0, GLA_QK_PAD - GLA_QK))).astype(BF16)
    bg = rows(jnp.pad(gla_b_gate, ((0, 0), (0, GLA_QK_PAD - GLA_QK))))
    w_in_p = _pack_w_in(w_in)
    gla_gn = rows(_pad_heads(jnp.tile(gla_norm, (1, GLA_HEADS)), 1))
    wa = _block_diag(lru_w_a).astype(BF16)
    wx = _block_diag(lru_w_x).astype(BF16)
    wo = w_out.astype(BF16)
    wo_a = _pad_heads(wo[:, :GLA_WIDTH], 1)
    wo_b = wo[:, GLA_WIDTH:GLA_WIDTH + LRU_WIDTH]
    wo_c = wo[:, GLA_WIDTH + LRU_WIDTH:]
    w_up = ffn_w_up.astype(BF16)
    w_dn = ffn_w_down.astype(BF16)
    vecs = [rows(v) for v in (norm_mix, lru_conv_b, lru_b_a, lru_b_x, lru_lambda, lru_norm, swa_norm, norm_ffn,
                              ffn_conv_b)]
    g_mix, cb_lru, ba, bx, lam, g_lru, g_swa, g_ffn, cb_ffn = vecs
    g_fin = norm_final.reshape(1, -1)
    lru = (lru_conv_w, cb_lru, wa, ba, wx, bx, lam, g_lru)
    for l in range(depth):
        qe, ke, qi, kd, dec, av, ar, o_b, cq, ck, cv = _inproj(xf, g_mix, w_in_p, wg, bg, cos, sa, sb, lru, seq, l)
        o_a = _gla(qe, ke, qi, kd, dec, av, ar, gla_gn, seq, l)
        o_c = _swa(cq, ck, cv, g_swa, seq, l)
        xf = _ffn(xf, o_a, o_b, o_c, wo_a, wo_b, wo_c, g_ffn, w_up, ffn_conv_w, cb_ffn, w_dn, g_fin, seq, l,
                  l == depth - 1)
    return xf.reshape(batch, seq, D_MODEL)
```
